```python
import math
import jax, jax.numpy as jnp
from jax import lax
import numpy as np

D_MODEL = 2048
BATCH = 4
SEQ = 2048
DEPTH = 1
DEC_BATCH = 128
DEC_SEQ = 8
PAST_LEN = 2048
PAGE_SIZE = 128

HEAD_DIM = 128
ATTN_WIDTH = D_MODEL // 2
N_HEADS = ATTN_WIDTH // HEAD_DIM
N_KV_HEADS = N_HEADS // 4
GQA_REP = N_HEADS // N_KV_HEADS
KV_WIDTH = N_KV_HEADS * HEAD_DIM
CONV_DIM = D_MODEL - ATTN_WIDTH
CONV_WIDTH = 3
CMP_BLOCK = 32
CMP_STRIDE = 16
CMP_RATIO = CMP_BLOCK // CMP_STRIDE
CMP_HIDDEN = 256
SEL_BLOCK = 64
SEL_TOPK = 8
WINDOW = 512
N_BRANCH = 3
N_KV_COMP = 4
Q_BLOCK = 128
N_EXPERTS = 32
TOP_K = 4
D_FF = D_MODEL
SWIGLU_LIMIT = 7.0
SWIGLU_ALPHA = 1.702
MOE_BLOCK = 128
LN_EPS = 1e-5
NEG = -1e30
BIG = 1e30
DN_ALPHA = (2.0 * DEPTH) ** 0.25
DN_BETA = (8.0 * DEPTH) ** -0.25
OFF_KV = ATTN_WIDTH
OFF_WIN = OFF_KV + N_KV_COMP * KV_WIDTH
OFF_GATE = OFF_WIN + 2 * KV_WIDTH
OFF_CONV = OFF_GATE + N_BRANCH * N_HEADS
IN_WIDTH = OFF_CONV + 3 * CONV_DIM

kernel_name = 'nsa_shortconv_moe_hybrid_decode'


def alibi_slopes(n):
    def pow2(m):
        start = 2.0 ** (-8.0 / m)
        return [start ** (i + 1) for i in range(m)]
    if math.log2(n).is_integer():
        s = pow2(n)
    else:
        c = 2 ** math.floor(math.log2(n))
        s = pow2(c) + pow2(2 * c)[0::2][:n - c]
    return np.asarray(s, np.float32)


def layer_norm(x, g, b):
    xf = x.astype(jnp.float32)
    mu = xf.mean(-1, keepdims=True)
    var = jnp.mean(jnp.square(xf - mu), -1, keepdims=True)
    return ((xf - mu) * lax.rsqrt(var + LN_EPS) * g + b).astype(x.dtype)


def masked_softmax(s, mask):
    s = jnp.where(mask, s.astype(jnp.float32), NEG)
    e = jnp.where(mask, jnp.exp(s - s.max(-1, keepdims=True)), 0.0)
    den = e.sum(-1, keepdims=True)
    return e / jnp.where(den > 0, den, 1.0)


def last_rows(a, n):
    if a.shape[1] < n:
        a = jnp.pad(a, [(0, 0), (n - a.shape[1], 0)] + [(0, 0)] * (a.ndim - 2))
    return a[:, a.shape[1] - n:]


def compress_blocks(raw, pe, w1, w2):
    B, L, G, hd = raw.shape
    n_cmp = (L - CMP_BLOCK) // CMP_STRIDE + 1
    n_chunk = n_cmp + CMP_RATIO - 1
    chunks = raw[:, :n_chunk * CMP_STRIDE].reshape(B, n_chunk, CMP_STRIDE, G, hd)
    chunks = chunks.transpose(0, 1, 3, 2, 4).reshape(B, n_chunk, G, CMP_STRIDE * hd)
    w1p = w1.reshape(CMP_RATIO, CMP_STRIDE * hd, CMP_HIDDEN)
    part = jnp.einsum('bcgf,jfh->jbcgh', chunks, w1p)
    pre = pe.reshape(-1) @ w1
    for j in range(CMP_RATIO):
        pre = pre + part[j, :, j:j + n_cmp]
    return jax.nn.gelu(pre) @ w2


def sparse_branches(q, k_sel, v_sel, sel_idx, k_win, v_win, q_pos0, kw_pos0, slopes):
    B, T = q.shape[:2]
    L = k_sel.shape[1]
    qb = math.gcd(T, Q_BLOCK)
    nqb = T // qb
    k_bg = k_sel.transpose(0, 2, 1, 3)
    v_bg = v_sel.transpose(0, 2, 1, 3)
    pad = ((0, 0), (WINDOW, 0), (0, 0), (0, 0))
    kw_pad = jnp.pad(k_win, pad)
    vw_pad = jnp.pad(v_win, pad)
    b_ix = jnp.arange(B)[:, None, None, None]
    g_ix = jnp.arange(N_KV_HEADS)[None, None, :, None]
    sl = slopes[:, :, None]

    def block(args):
        qi, q_b, idx_b = args
        p0 = q_pos0 + qi * qb
        t = p0 + jnp.arange(qb)
        pos = (idx_b[..., None] * SEL_BLOCK + jnp.arange(SEL_BLOCK)).reshape(B, qb, N_KV_HEADS, -1)
        posc = jnp.minimum(pos, L - 1)
        k_g = k_bg[b_ix, g_ix, posc]
        v_g = v_bg[b_ix, g_ix, posc]
        dist = (t[None, :, None, None] - pos).astype(jnp.float32)[:, :, :, None, :]
        s = jnp.einsum('btgrd,btgnd->btgrn', q_b, k_g).astype(jnp.float32) - sl * dist
        mask = ((pos <= t[None, :, None, None]) & (pos < L))[:, :, :, None, :]
        p = masked_softmax(s, mask)
        o_sel = jnp.einsum('btgrn,btgnd->btgrd', p.astype(v_g.dtype), v_g)
        kw_b = lax.dynamic_slice_in_dim(kw_pad, p0 - kw_pos0, WINDOW + qb, axis=1)
        vw_b = lax.dynamic_slice_in_dim(vw_pad, p0 - kw_pos0, WINDOW + qb, axis=1)
        kpos = p0 - WINDOW + jnp.arange(WINDOW + qb)
        rel = t[:, None] - kpos[None, :]
        s = jnp.einsum('btgrd,bngd->btgrn', q_b, kw_b).astype(jnp.float32) - sl * rel.astype(jnp.float32)[:, None, None, :]
        mask = ((kpos[None, :] >= kw_pos0) & (rel >= 0) & (rel < WINDOW))[:, None, None, :]
        p = masked_softmax(s, mask)
        o_win = jnp.einsum('btgrn,bngd->btgrd', p.astype(vw_b.dtype), vw_b)
        return o_sel, o_win

    q_blocks = q.reshape(B, nqb, qb, N_KV_HEADS, GQA_REP, HEAD_DIM).swapaxes(0, 1)
    idx_blocks = sel_idx.reshape(B, nqb, qb, N_KV_HEADS, -1).swapaxes(0, 1)
    o_sel, o_win = lax.map(block, (jnp.arange(nqb), q_blocks, idx_blocks))
    o_sel = o_sel.swapaxes(0, 1).reshape(B, T, N_KV_HEADS, GQA_REP, HEAD_DIM)
    o_win = o_win.swapaxes(0, 1).reshape(B, T, N_KV_HEADS, GQA_REP, HEAD_DIM)
    return o_sel, o_win


def nsa_attention(q, kv_full, win_keys, gate_logits, q_pos0, kw_pos0,
                  cmp_k_pe, cmp_k_w1, cmp_k_w2, cmp_v_pe, cmp_v_w1, cmp_v_w2):
    B, T = q.shape[:2]
    L = kv_full.shape[1]
    q = (q * HEAD_DIM ** -0.5).reshape(B, T, N_KV_HEADS, GQA_REP, HEAD_DIM)
    slopes = jnp.asarray(alibi_slopes(N_HEADS).reshape(N_KV_HEADS, GQA_REP))
    t_pos = q_pos0 + jnp.arange(T)
    k_c = compress_blocks(kv_full[:, :, 0], cmp_k_pe, cmp_k_w1, cmp_k_w2)
    v_c = compress_blocks(kv_full[:, :, 1], cmp_v_pe, cmp_v_w1, cmp_v_w2)
    n_cmp = k_c.shape[1]
    c_end = jnp.arange(n_cmp) * CMP_STRIDE + CMP_BLOCK - 1
    rel_c = (t_pos[:, None] - c_end[None, :])
    s_c = jnp.einsum('btgrd,bngd->btgrn', q, k_c).astype(jnp.float32) - slopes[:, :, None] * rel_c.astype(jnp.float32)[:, None, None, :]
    p_c = masked_softmax(s_c, (rel_c >= 0)[:, None, None, :])
    o_c = jnp.einsum('btgrn,bngd->btgrd', p_c.astype(v_c.dtype), v_c)
    n_sel = -(-L // SEL_BLOCK)
    ci = jnp.arange(n_cmp)[:, None] * CMP_STRIDE
    sj = jnp.arange(n_sel)[None, :] * SEL_BLOCK
    overlap = jnp.clip(jnp.minimum(ci + CMP_BLOCK, sj + SEL_BLOCK) - jnp.maximum(ci, sj), 0, None).astype(jnp.float32)
    imp = jnp.einsum('btgrn,nj->btgj', p_c, overlap)
    blk = jnp.arange(n_sel)[None, :]
    cur = (t_pos // SEL_BLOCK)[:, None]
    valid = (blk <= cur)[:, None, :]
    forced = ((blk == 0) | (blk == cur) | (blk == cur - 1))[:, None, :]
    score = jnp.where(valid, jnp.where(forced, BIG, imp), NEG)
    _, sel_idx = lax.top_k(score, min(SEL_TOPK, n_sel))
    o_s, o_w = sparse_branches(q, kv_full[:, :, 2], kv_full[:, :, 3], sel_idx,
                               win_keys[:, :, 0], win_keys[:, :, 1], q_pos0, kw_pos0, slopes)
    g = jax.nn.sigmoid(gate_logits.astype(jnp.float32)).reshape(B, T, N_KV_HEADS, GQA_REP, N_BRANCH).astype(q.dtype)
    o = g[..., 0:1] * o_c + g[..., 1:2] * o_s + g[..., 2:3] * o_w
    return o.reshape(B, T, ATTN_WIDTH)


def moe(x, w_router, b_router, w_gu, b_gu, w_down, b_down):
    B, T, D = x.shape
    xt = x.reshape(-1, D)
    n = xt.shape[0]
    logits = (xt @ w_router).astype(jnp.float32) + b_router.astype(jnp.float32)
    top_v, top_e = lax.top_k(logits, TOP_K)
    gate = jax.nn.softmax(top_v, axis=-1)
    e_flat = top_e.reshape(-1)
    tok_flat = jnp.repeat(jnp.arange(n), TOP_K)
    g_flat = gate.reshape(-1)
    order = jnp.argsort(e_flat)
    e_s, tok_s, g_s = e_flat[order], tok_flat[order], g_flat[order]
    counts = jnp.bincount(e_flat, length=N_EXPERTS)
    starts = jnp.cumsum(counts) - counts
    padded = (counts + MOE_BLOCK - 1) // MOE_BLOCK * MOE_BLOCK
    pends = jnp.cumsum(padded)
    pstarts = pends - padded
    dest = pstarts[e_s] + (jnp.arange(n * TOP_K) - starts[e_s])
    n_blocks = -(-(n * TOP_K) // MOE_BLOCK) + N_EXPERTS
    rows = jnp.zeros((n_blocks * MOE_BLOCK, D), x.dtype).at[dest].set(xt[tok_s])
    block_e = jnp.minimum(jnp.searchsorted(pends, jnp.arange(n_blocks) * MOE_BLOCK, side='right'), N_EXPERTS - 1)

    def expert_block(args):
        xb, e = args
        h = xb @ w_gu[e] + b_gu[e]
        hg = jnp.minimum(h[:, :D_FF], SWIGLU_LIMIT)
        hu = jnp.clip(h[:, D_FF:], -SWIGLU_LIMIT, SWIGLU_LIMIT)
        a = hg * jax.nn.sigmoid(SWIGLU_ALPHA * hg) * (hu + 1.0)
        return a @ w_down[e] + b_down[e]

    yb = lax.map(expert_block, (rows.reshape(n_blocks, MOE_BLOCK, D), block_e))
    y_rows = yb.reshape(-1, D)[dest]
    y = jnp.zeros((n, D), x.dtype).at[tok_s].add(y_rows * g_s[:, None].astype(x.dtype))
    return y.reshape(B, T, D)


def decoder_layer(x, pos0, kv_past, win_past, conv_past, win_buf,
                  w_in, w_out, conv_w, cmp_k_pe, cmp_k_w1, cmp_k_w2, cmp_v_pe, cmp_v_w1, cmp_v_w2,
                  ln1_g, ln1_b, w_router, b_router, w_gu, b_gu, w_down, b_down, ln2_g, ln2_b):
    B, T, _ = x.shape
    proj = x @ w_in
    q = proj[..., :OFF_KV]
    kv_new = proj[..., OFF_KV:OFF_WIN].reshape(B, T, N_KV_COMP, N_KV_HEADS, HEAD_DIM)
    win_new = proj[..., OFF_WIN:OFF_GATE].reshape(B, T, 2, N_KV_HEADS, HEAD_DIM)
    gate_logits = proj[..., OFF_GATE:OFF_CONV]
    b_gate, c_gate, h_conv = jnp.split(proj[..., OFF_CONV:], 3, axis=-1)
    if kv_past is None:
        kv_full, win_keys, kw_pos0 = kv_new, win_new, pos0
        conv_prev = jnp.zeros((B, CONV_WIDTH - 1, CONV_DIM), x.dtype)
    else:
        kv_full = jnp.concatenate([kv_past, kv_new], axis=1)
        win_keys = jnp.concatenate([win_past, win_new], axis=1)
        kw_pos0 = pos0 - win_past.shape[1]
        conv_prev = conv_past
    o_attn = nsa_attention(q, kv_full, win_keys, gate_logits, pos0, kw_pos0,
                           cmp_k_pe, cmp_k_w1, cmp_k_w2, cmp_v_pe, cmp_v_w1, cmp_v_w2)
    u_ext = jnp.concatenate([conv_prev, c_gate * h_conv], axis=1)
    conv_y = conv_w[0] * u_ext[:, 0:T]
    for k in range(1, CONV_WIDTH):
        conv_y = conv_y + conv_w[k] * u_ext[:, k:k + T]
    mix = jnp.concatenate([o_attn, b_gate * conv_y], axis=-1) @ w_out
    x1 = layer_norm(DN_ALPHA * x + mix, ln1_g, ln1_b)
    x2 = layer_norm(DN_ALPHA * x1 + moe(x1, w_router, b_router, w_gu, b_gu, w_down, b_down), ln2_g, ln2_b)
    return x2, kv_new, last_rows(win_keys, win_buf), u_ext[:, -(CONV_WIDTH - 1):]


def setup_inputs(seed: int = 0) -> dict:
    key = jax.random.key(seed)
    ks = jax.random.split(key, 24)
    f32 = jnp.float32
    n_pages = PAST_LEN // PAGE_SIZE
    n_used = DEC_BATCH * n_pages
    n_pool = n_used + -(-n_used // 4)
    win_buf = min(WINDOW, PAST_LEN)

    def nrm(k, shape, scale):
        return scale * jax.random.normal(k, shape, f32)

    col_scale = np.concatenate([
        np.ones(ATTN_WIDTH),
        np.ones(KV_WIDTH), np.full(KV_WIDTH, DN_BETA),
        np.ones(KV_WIDTH), np.full(KV_WIDTH, DN_BETA),
        np.ones(KV_WIDTH), np.full(KV_WIDTH, DN_BETA),
        np.ones(N_BRANCH * N_HEADS),
        np.ones(2 * CONV_DIM), np.full(CONV_DIM, DN_BETA),
    ]).astype(np.float32)
    page_table = jax.random.permutation(ks[5], n_pool)[:n_used].reshape(DEC_BATCH, n_pages).astype(jnp.int32)
    return {
        'x_prompt': nrm(ks[0], (BATCH, SEQ, D_MODEL), 1.0),
        'x_sample': nrm(ks[1], (DEC_BATCH, DEC_SEQ, D_MODEL), 1.0),
        'cache_kv': nrm(ks[2], (DEPTH, n_pool, PAGE_SIZE, N_KV_COMP, N_KV_HEADS, HEAD_DIM), 1.0),
        'state_win': nrm(ks[3], (DEPTH, DEC_BATCH, win_buf, 2, N_KV_HEADS, HEAD_DIM), 1.0),
        'state_conv': nrm(ks[4], (DEPTH, DEC_BATCH, CONV_WIDTH - 1, CONV_DIM), 0.5),
        'page_table': page_table,
        'w_in': nrm(ks[6], (DEPTH, D_MODEL, IN_WIDTH), D_MODEL ** -0.5) * jnp.asarray(col_scale),
        'w_out': nrm(ks[7], (DEPTH, D_MODEL, D_MODEL), DN_BETA * D_MODEL ** -0.5),
        'conv_w': nrm(ks[8], (DEPTH, CONV_WIDTH, CONV_DIM), CONV_WIDTH ** -0.5),
        'cmp_k_pe': nrm(ks[9], (DEPTH, CMP_BLOCK, HEAD_DIM), 0.1),
        'cmp_k_w1': nrm(ks[10], (DEPTH, CMP_BLOCK * HEAD_DIM, CMP_HIDDEN), (CMP_BLOCK * HEAD_DIM) ** -0.5),
        'cmp_k_w2': nrm(ks[11], (DEPTH, CMP_HIDDEN, HEAD_DIM), CMP_HIDDEN ** -0.5),
        'cmp_v_pe': nrm(ks[12], (DEPTH, CMP_BLOCK, HEAD_DIM), 0.1),
        'cmp_v_w1': nrm(ks[13], (DEPTH, CMP_BLOCK * HEAD_DIM, CMP_HIDDEN), (CMP_BLOCK * HEAD_DIM) ** -0.5),
        'cmp_v_w2': nrm(ks[14], (DEPTH, CMP_HIDDEN, HEAD_DIM), CMP_HIDDEN ** -0.5),
        'ln1_g': 1.0 + nrm(ks[15], (DEPTH, D_MODEL), 0.01),
        'ln1_b': nrm(ks[16], (DEPTH, D_MODEL), 0.01),
        'w_router': nrm(ks[17], (DEPTH, D_MODEL, N_EXPERTS), D_MODEL ** -0.5),
        'b_router': nrm(ks[18], (DEPTH, N_EXPERTS), 0.01),
        'w_gu': nrm(ks[19], (DEPTH, N_EXPERTS, D_MODEL, 2 * D_FF), DN_BETA * D_MODEL ** -0.5),
        'b_gu': nrm(ks[20], (DEPTH, N_EXPERTS, 2 * D_FF), 0.01),
        'w_down': nrm(ks[21], (DEPTH, N_EXPERTS, D_FF, D_MODEL), DN_BETA * D_FF ** -0.5),
        'b_down': nrm(ks[22], (DEPTH, N_EXPERTS, D_MODEL), 0.01),
        'ln2_g': 1.0 + nrm(ks[23], (DEPTH, D_MODEL), 0.01),
        'ln2_b': nrm(jax.random.fold_in(ks[23], 1), (DEPTH, D_MODEL), 0.01),
    }


def reference(x_prompt, x_sample, cache_kv, state_win, state_conv, page_table,
              w_in, w_out, conv_w, cmp_k_pe, cmp_k_w1, cmp_k_w2, cmp_v_pe, cmp_v_w1, cmp_v_w2,
              ln1_g, ln1_b, w_router, b_router, w_gu, b_gu, w_down, b_down, ln2_g, ln2_b):
    dec_batch, n_pages = page_table.shape
    past_len = n_pages * cache_kv.shape[2]
    win_buf = state_win.shape[2]
    y_prompt, y_sample = x_prompt, x_sample
    kv_p, kv_s, win_p, win_s, conv_p, conv_s = [], [], [], [], [], []
    for layer in range(DEPTH):
        params = (w_in[layer], w_out[layer], conv_w[layer], cmp_k_pe[layer], cmp_k_w1[layer], cmp_k_w2[layer],
                  cmp_v_pe[layer], cmp_v_w1[layer], cmp_v_w2[layer], ln1_g[layer], ln1_b[layer],
                  w_router[layer], b_router[layer], w_gu[layer], b_gu[layer], w_down[layer], b_down[layer],
                  ln2_g[layer], ln2_b[layer])
        kv_past = cache_kv[layer][page_table].reshape(dec_batch, past_len, N_KV_COMP, N_KV_HEADS, HEAD_DIM)
        y_prompt, a, b, c = decoder_layer(y_prompt, 0, None, None, None, win_buf, *params)
        y_sample, d, e, f = decoder_layer(y_sample, past_len, kv_past, state_win[layer], state_conv[layer], win_buf, *params)
        kv_p.append(a); win_p.append(b); conv_p.append(c)
        kv_s.append(d); win_s.append(e); conv_s.append(f)
    kv_prompt, kv_sample = jnp.stack(kv_p), jnp.stack(kv_s)
    win_prompt, win_sample = jnp.stack(win_p), jnp.stack(win_s)
    conv_prompt, conv_sample = jnp.stack(conv_p), jnp.stack(conv_s)
    return (y_prompt, y_sample, kv_prompt, kv_sample, win_prompt, win_sample, conv_prompt, conv_sample)
```

```python
import functools
import math

import numpy as np
import jax
import jax.numpy as jnp
from jax import lax
from jax.experimental import pallas as pl
from jax.experimental.pallas import tpu as pltpu

F32 = jnp.float32
BF16 = jnp.bfloat16
I32 = jnp.int32

HEAD_DIM = 128
N_HEADS = 8
N_KV_HEADS = 2
GQA_REP = N_HEADS // N_KV_HEADS
N_KV_COMP = 4
KV_WIDTH = N_KV_HEADS * HEAD_DIM
ATTN_WIDTH = N_HEADS * HEAD_DIM
CONV_WIDTH = 3
CMP_BLOCK = 32
CMP_STRIDE = 16
CMP_RATIO = CMP_BLOCK // CMP_STRIDE
CMP_HIDDEN = 256
SEL_BLOCK = 64
SEL_TOPK = 8
WINDOW = 512
N_BRANCH = 3
Q_BLOCK = 128
TOP_K = 4
SWIGLU_LIMIT = 7.0
SWIGLU_ALPHA = 1.702
MOE_BLOCK = 128
LN_EPS = 1e-5
NEG = -1e30
BIG = 1e30
LANE = 128
SUBLANE = 8
VMEM_LIMIT = 56 * 1024 * 1024
SEL_CHUNK = 512

CB_KV = ATTN_WIDTH // LANE
CB_WIN = CB_KV + N_KV_COMP * N_KV_HEADS
CB_CONV = CB_WIN + 2 * N_KV_HEADS
N_GATE = N_BRANCH * N_HEADS


def _alibi_slopes(n):
    def pow2(m):
        start = 2.0 ** (-8.0 / m)
        return [start ** (i + 1) for i in range(m)]
    if math.log2(n).is_integer():
        s = pow2(n)
    else:
        c = 2 ** math.floor(math.log2(n))
        s = pow2(c) + pow2(2 * c)[0::2][:n - c]
    return [float(np.float32(v)) for v in s]


SLOPES = _alibi_slopes(N_HEADS)


def _pick(n, target, mult=SUBLANE):
    for d in range(min(n, target), 0, -1):
        if n % d == 0 and d % mult == 0:
            return d
    raise ValueError((n, target, mult))


def _cparams(sem):
    return pltpu.CompilerParams(dimension_semantics=sem, vmem_limit_bytes=VMEM_LIMIT)


def _proj_kernel(x_ref, w_ref, o_ref):
    o_ref[...] = jnp.dot(x_ref[...].astype(BF16), w_ref[...], preferred_element_type=F32)


def _proj(x, w, tm, tn):
    m, k = x.shape
    n = w.shape[1]
    return pl.pallas_call(
        _proj_kernel,
        out_shape=jax.ShapeDtypeStruct((m, n), F32),
        grid=(m // tm, n // tn),
        in_specs=[pl.BlockSpec((tm, k), lambda i, j: (i, 0)),
                  pl.BlockSpec((k, tn), lambda i, j: (0, j))],
        out_specs=pl.BlockSpec((tm, tn), lambda i, j: (i, j)),
        compiler_params=_cparams(("parallel", "arbitrary")),
    )(x, w)


def _pe_bias_kernel(pe_ref, w1_ref, o_ref):
    o_ref[...] = jnp.dot(pe_ref[...].astype(BF16), w1_ref[...], preferred_element_type=F32)


def _pe_bias(pe, w1_bf):
    flat = jnp.broadcast_to(pe.reshape(1, -1), (SUBLANE, pe.size))
    return pl.pallas_call(
        _pe_bias_kernel,
        out_shape=jax.ShapeDtypeStruct((SUBLANE, CMP_HIDDEN), F32),
    )(flat, w1_bf)


def _compress_kernel(ids_ref, *refs, n_pages):
    pages = refs[:n_pages]
    w1k_ref, w1v_ref, bk_ref, bv_ref, w2k_ref, w2v_ref, kc_ref, vc_ref, raw = refs[n_pages:]
    n_chunk = n_pages * (LANE // CMP_STRIDE)
    for cb in range(2 * N_KV_HEADS):
        for p in range(n_pages):
            raw[cb, p * LANE:(p + 1) * LANE, :] = pages[p][0, :, cb * HEAD_DIM:(cb + 1) * HEAD_DIM]
    for kv, (w1_ref, b_ref, w2_ref, o_ref) in enumerate(
            ((w1k_ref, bk_ref, w2k_ref, kc_ref), (w1v_ref, bv_ref, w2v_ref, vc_ref))):
        acc = jnp.zeros((N_KV_HEADS * n_chunk, CMP_RATIO * CMP_HIDDEN), F32)
        for sp in range(CMP_STRIDE // 2):
            per_g = []
            for g in range(N_KV_HEADS):
                halves = [raw[kv * N_KV_HEADS + g, pl.ds(s, n_chunk, stride=CMP_STRIDE), :]
                          for s in (2 * sp, 2 * sp + 1)]
                per_g.append(jnp.concatenate(halves, axis=1))
            lhs = jnp.concatenate(per_g, axis=0).astype(BF16)
            acc = acc + jnp.dot(lhs, w1_ref[pl.ds(sp * 2 * HEAD_DIM, 2 * HEAD_DIM), :],
                                preferred_element_type=F32)
        for g in range(N_KV_HEADS):
            part0 = acc[g * n_chunk:(g + 1) * n_chunk, :CMP_HIDDEN]
            part1 = acc[g * n_chunk:(g + 1) * n_chunk, CMP_HIDDEN:]
            part1 = jnp.concatenate([part1[1:], part1[:1]], axis=0)
            pre = (b_ref[0:1, :] + part0) + part1
            o_ref[0, g] = jnp.dot(jax.nn.gelu(pre).astype(BF16), w2_ref[...],
                                  preferred_element_type=F32)


def _compress(page_ids, src, col_block, w1k, w1v, bk, bv, w2k, w2v):
    nb, n_pages = page_ids.shape
    n_chunk = n_pages * (LANE // CMP_STRIDE)
    width = 2 * KV_WIDTH

    def page_map(p):
        return lambda b, ids: (ids[b, p], 0, col_block)

    full = lambda shape: pl.BlockSpec(shape, lambda b, ids: (0,) * len(shape))
    in_specs = [pl.BlockSpec((1, LANE, width), page_map(p)) for p in range(n_pages)]
    in_specs += [full(w1k.shape), full(w1v.shape), full(bk.shape), full(bv.shape),
                 full(w2k.shape), full(w2v.shape)]
    out_spec = pl.BlockSpec((1, N_KV_HEADS, n_chunk, HEAD_DIM), lambda b, ids: (b, 0, 0, 0))
    out_sd = jax.ShapeDtypeStruct((nb, N_KV_HEADS, n_chunk, HEAD_DIM), F32)
    return pl.pallas_call(
        functools.partial(_compress_kernel, n_pages=n_pages),
        out_shape=(out_sd, out_sd),
        grid_spec=pltpu.PrefetchScalarGridSpec(
            num_scalar_prefetch=1, grid=(nb,), in_specs=in_specs,
            out_specs=(out_spec, out_spec),
            scratch_shapes=[pltpu.VMEM((2 * N_KV_HEADS, n_pages * LANE, HEAD_DIM), F32)]),
        compiler_params=_cparams(("parallel",)),
    )(page_ids, *([src] * n_pages), w1k, w1v, bk, bv, w2k, w2v)


def _nt_dot(a, b):
    return lax.dot_general(a, b, (((1,), (1,)), ((), ())), preferred_element_type=F32)


def _masked_softmax(s, mask):
    s = jnp.where(mask, s, NEG)
    e = jnp.where(mask, jnp.exp(s - jnp.max(s, axis=-1, keepdims=True)), 0.0)
    den = jnp.sum(e, axis=-1, keepdims=True)
    return e / jnp.where(den > 0, den, 1.0)


def _stack_heads(q, tq):
    q = q * (HEAD_DIM ** -0.5)
    return jnp.concatenate([q[:, r * HEAD_DIM:(r + 1) * HEAD_DIM] for r in range(GQA_REP)],
                           axis=0).astype(BF16)


def _row_consts(g, tq, p0):
    rows = GQA_REP * tq
    ridx = lax.broadcasted_iota(I32, (rows, 1), 0)
    slope = jnp.zeros((rows, 1), F32)
    for r in range(GQA_REP):
        slope = jnp.where(ridx // tq == r, SLOPES[g * GQA_REP + r], slope)
    t = p0 + ridx % tq
    return slope, t


def _cmp_branch(q4, kc, vc, slope, t, tq):
    n_chunk = kc.shape[0]
    s = _nt_dot(q4, kc.astype(BF16))
    c_end = lax.broadcasted_iota(I32, (1, n_chunk), 1) * CMP_STRIDE + (CMP_BLOCK - 1)
    rel = t - c_end
    s = s - slope * rel.astype(F32)
    p = _masked_softmax(s, rel >= 0)
    o_c = jnp.dot(p.astype(BF16), vc.astype(BF16), preferred_element_type=F32)
    p_sum = p[0:tq]
    for r in range(1, GQA_REP):
        p_sum = p_sum + p[r * tq:(r + 1) * tq]
    return o_c, p_sum


def _select_blocks(p_sum, t_q, n_sel):
    tq, n_chunk = p_sum.shape
    ci = lax.broadcasted_iota(I32, (n_chunk, LANE), 0) * CMP_STRIDE
    sj = lax.broadcasted_iota(I32, (n_chunk, LANE), 1) * SEL_BLOCK
    overlap = jnp.maximum(jnp.minimum(ci + CMP_BLOCK, sj + SEL_BLOCK) - jnp.maximum(ci, sj), 0)
    overlap = overlap.astype(F32).astype(BF16)
    p_hi = p_sum.astype(BF16)
    p_lo = (p_sum - p_hi.astype(F32)).astype(BF16)
    imp = (jnp.dot(p_hi, overlap, preferred_element_type=F32)
           + jnp.dot(p_lo, overlap, preferred_element_type=F32))
    lane = lax.broadcasted_iota(I32, (tq, LANE), 1)
    cur = t_q // SEL_BLOCK
    valid = (lane <= cur) & (lane < n_sel)
    forced = (lane == 0) | (lane == cur) | (lane == cur - 1)
    score = jnp.where(valid, jnp.where(forced, BIG, imp), -1.0)
    lo = jnp.zeros((tq, 1), I32)
    hi = jnp.zeros((tq, 1), I32)
    for _ in range(min(SEL_TOPK, n_sel)):
        m = jnp.max(score, axis=-1, keepdims=True)
        idx = jnp.min(jnp.where(score == m, lane, LANE), axis=-1, keepdims=True)
        ok = m >= 0.0
        bit = lax.shift_left(jnp.ones_like(idx), idx & 31)
        lo = lo | jnp.where(ok & (idx < 32), bit, 0)
        hi = hi | jnp.where(ok & (idx >= 32), bit, 0)
        score = jnp.where(lane == idx, -1.0, score)
    return lo, hi


def _sel_mask(lo4, hi4, pos, t, n_sel):
    blk = pos // SEL_BLOCK
    word = lo4 if n_sel <= 32 else jnp.where(blk < 32, lo4, hi4)
    bits = lax.shift_right_logical(jnp.broadcast_to(word, (word.shape[0], pos.shape[1])),
                                   jnp.broadcast_to(blk & 31, (word.shape[0], pos.shape[1])))
    return (pos <= t) & ((bits & 1) == 1)


def _rep_rows(x):
    return jnp.concatenate([x] * GQA_REP, axis=0)


def _gated_sum(sig, g, o_c, o_s, o_w, tq):
    outs = []
    for r in range(GQA_REP):
        h = g * GQA_REP + r
        sl = slice(r * tq, (r + 1) * tq)
        g0 = sig[:, N_BRANCH * h + 0:N_BRANCH * h + 1]
        g1 = sig[:, N_BRANCH * h + 1:N_BRANCH * h + 2]
        g2 = sig[:, N_BRANCH * h + 2:N_BRANCH * h + 3]
        outs.append(g0 * o_c[sl] + g1 * o_s[sl] + g2 * o_w[sl])
    return outs


def _prompt_attn_kernel(q_ref, gate_ref, kc_ref, vc_ref, ks_ref, vs_ref, kw_ref, vw_ref,
                        o_ref, ksb, vsb, kwb, vwb, *, tq, seq_len):
    g = pl.program_id(1)
    i = pl.program_id(2)
    n_sel = -(-seq_len // SEL_BLOCK)

    @pl.when(i == 0)
    def _():
        ksb[...] = ks_ref[...].astype(BF16)
        vsb[...] = vs_ref[...].astype(BF16)
        kwb[...] = kw_ref[...].astype(BF16)
        vwb[...] = vw_ref[...].astype(BF16)

    p0 = i * tq
    q4 = _stack_heads(q_ref[...], tq)
    rows = GQA_REP * tq
    ridx = lax.broadcasted_iota(I32, (rows, 1), 0)
    t = p0 + ridx % tq
    slope = jnp.zeros((rows, 1), F32)
    for gg in range(N_KV_HEADS):
        for r in range(GQA_REP):
            slope = jnp.where((ridx // tq == r) & (g == gg), SLOPES[gg * GQA_REP + r], slope)

    o_c, p_sum = _cmp_branch(q4, kc_ref[0, 0], vc_ref[0, 0], slope, t, tq)
    lo, hi = _select_blocks(p_sum, t[0:tq], n_sel)
    lo4, hi4 = _rep_rows(lo), _rep_rows(hi)

    def chunk(c, carry):
        m, l, acc = carry
        start = pl.multiple_of(c * SEL_CHUNK, SEL_CHUNK)
        k = ksb[pl.ds(start, SEL_CHUNK), :]
        v = vsb[pl.ds(start, SEL_CHUNK), :]
        pos = start + lax.broadcasted_iota(I32, (1, SEL_CHUNK), 1)
        s = _nt_dot(q4, k) - slope * (t - pos).astype(F32)
        mask = _sel_mask(lo4, hi4, pos, t, n_sel)
        s = jnp.where(mask, s, NEG)
        m_new = jnp.maximum(m, jnp.max(s, axis=-1, keepdims=True))
        a = jnp.exp(m - m_new)
        e = jnp.where(mask, jnp.exp(s - m_new), 0.0)
        l = a * l + jnp.sum(e, axis=-1, keepdims=True)
        acc = a * acc + jnp.dot(e.astype(BF16), v, preferred_element_type=F32)
        return m_new, l, acc

    n_chunks = (p0 + tq - 1) // SEL_CHUNK + 1
    m0 = jnp.full((rows, 1), NEG, F32)
    l0 = jnp.zeros((rows, 1), F32)
    a0 = jnp.zeros((rows, HEAD_DIM), F32)
    _, l, acc = lax.fori_loop(0, n_chunks, chunk, (m0, l0, a0))
    o_s = acc / jnp.where(l > 0, l, 1.0)

    n_win = WINDOW + tq
    start = pl.multiple_of(jnp.maximum(p0 - WINDOW, 0), tq)
    kw = kwb[pl.ds(start, n_win), :]
    vw = vwb[pl.ds(start, n_win), :]
    kpos = start + lax.broadcasted_iota(I32, (1, n_win), 1)
    rel = t - kpos
    s = _nt_dot(q4, kw) - slope * rel.astype(F32)
    p = _masked_softmax(s, (rel >= 0) & (rel < WINDOW))
    o_w = jnp.dot(p.astype(BF16), vw, preferred_element_type=F32)

    sig = jax.nn.sigmoid(gate_ref[...])
    for gg in range(N_KV_HEADS):
        @pl.when(g == gg)
        def _(gg=gg):
            outs = _gated_sum(sig, gg, o_c, o_s, o_w, tq)
            for r in range(GQA_REP):
                o_ref[:, r * HEAD_DIM:(r + 1) * HEAD_DIM] = outs[r]


def _prompt_attention(proj, kc, vc, n_batch, seq_len, n_cols_blocks):
    tq = Q_BLOCK
    nqb = seq_len // tq
    gate_cb = n_cols_blocks - 1
    qw = GQA_REP * HEAD_DIM
    kv_spec = lambda cb: pl.BlockSpec((seq_len, HEAD_DIM), lambda b, g, i: (b, cb + g))
    c_spec = pl.BlockSpec((1, 1) + kc.shape[2:], lambda b, g, i: (b, g, 0, 0))
    return pl.pallas_call(
        functools.partial(_prompt_attn_kernel, tq=tq, seq_len=seq_len),
        out_shape=jax.ShapeDtypeStruct((n_batch * seq_len, ATTN_WIDTH), F32),
        grid=(n_batch, N_KV_HEADS, nqb),
        in_specs=[pl.BlockSpec((tq, qw), lambda b, g, i: (b * nqb + i, g)),
                  pl.BlockSpec((tq, LANE), lambda b, g, i: (b * nqb + i, gate_cb)),
                  c_spec, c_spec,
                  kv_spec(CB_KV + 2 * N_KV_HEADS), kv_spec(CB_KV + 3 * N_KV_HEADS),
                  kv_spec(CB_WIN), kv_spec(CB_WIN + N_KV_HEADS)],
        out_specs=pl.BlockSpec((tq, qw), lambda b, g, i: (b * nqb + i, g)),
        scratch_shapes=[pltpu.VMEM((seq_len, HEAD_DIM), BF16)] * 4,
        compiler_params=_cparams(("parallel", "parallel", "arbitrary")),
    )(proj, proj, kc, vc, proj, proj, proj, proj)


def _sample_attn_kernel(pt_ref, *refs, n_pages, tq, past_len, win_buf):
    pages = refs[:n_pages]
    q_ref, kvn_ref, winn_ref, gate_ref, kc_ref, vc_ref, win_ref, o_ref = refs[n_pages:]
    seq_len = past_len + tq
    n_sel = -(-seq_len // SEL_BLOCK)
    pad = LANE - tq
    sig = jax.nn.sigmoid(gate_ref[...])
    zpad = jnp.zeros((pad, HEAD_DIM), BF16)

    for g in range(N_KV_HEADS):
        q4 = _stack_heads(q_ref[:, g * GQA_REP * HEAD_DIM:(g + 1) * GQA_REP * HEAD_DIM], tq)
        slope, t = _row_consts(g, tq, past_len)
        o_c, p_sum = _cmp_branch(q4, kc_ref[0, g], vc_ref[0, g], slope, t, tq)
        lo, hi = _select_blocks(p_sum, t[0:tq], n_sel)
        lo4, hi4 = _rep_rows(lo), _rep_rows(hi)

        kcol = g * HEAD_DIM
        vcol = KV_WIDTH + g * HEAD_DIM
        k = jnp.concatenate(
            [p[0, :, kcol:kcol + HEAD_DIM].astype(BF16) for p in pages]
            + [kvn_ref[:, 2 * KV_WIDTH + kcol:2 * KV_WIDTH + kcol + HEAD_DIM].astype(BF16), zpad],
            axis=0)
        v = jnp.concatenate(
            [p[0, :, vcol:vcol + HEAD_DIM].astype(BF16) for p in pages]
            + [kvn_ref[:, 2 * KV_WIDTH + vcol:2 * KV_WIDTH + vcol + HEAD_DIM].astype(BF16), zpad],
            axis=0)
        n_keys = k.shape[0]
        pos = lax.broadcasted_iota(I32, (1, n_keys), 1)
        s = _nt_dot(q4, k) - slope * (t - pos).astype(F32)
        p = _masked_softmax(s, _sel_mask(lo4, hi4, pos, t, n_sel))
        o_s = jnp.dot(p.astype(BF16), v, preferred_element_type=F32)

        kw = jnp.concatenate([win_ref[0, :, kcol:kcol + HEAD_DIM].astype(BF16),
                              winn_ref[:, kcol:kcol + HEAD_DIM].astype(BF16), zpad], axis=0)
        vw = jnp.concatenate([win_ref[0, :, vcol:vcol + HEAD_DIM].astype(BF16),
                              winn_ref[:, vcol:vcol + HEAD_DIM].astype(BF16), zpad], axis=0)
        kpos = (past_len - win_buf) + lax.broadcasted_iota(I32, (1, kw.shape[0]), 1)
        rel = t - kpos
        s = _nt_dot(q4, kw) - slope * rel.astype(F32)
        p = _masked_softmax(s, (rel >= 0) & (rel < WINDOW))
        o_w = jnp.dot(p.astype(BF16), vw, preferred_element_type=F32)

        outs = _gated_sum(sig, g, o_c, o_s, o_w, tq)
        for r in range(GQA_REP):
            h = g * GQA_REP + r
            o_ref[:, h * HEAD_DIM:(h + 1) * HEAD_DIM] = outs[r]


def _sample_attention(page_table, cache, proj, kc, vc, state_win, n_prompt_rows, tq, n_cols_blocks):
    db, n_pages = page_table.shape
    page = cache.shape[1]
    past_len = n_pages * page
    win_buf = state_win.shape[1]
    r0 = n_prompt_rows // tq
    gate_cb = n_cols_blocks - 1

    def page_map(p):
        return lambda b, pt: (pt[b, p], 0, 1)

    in_specs = [pl.BlockSpec((1, page, 2 * KV_WIDTH), page_map(p)) for p in range(n_pages)]
    in_specs += [
        pl.BlockSpec((tq, ATTN_WIDTH), lambda b, pt: (r0 + b, 0)),
        pl.BlockSpec((tq, N_KV_COMP * KV_WIDTH), lambda b, pt: (r0 + b, 1)),
        pl.BlockSpec((tq, 2 * KV_WIDTH), lambda b, pt: (r0 + b, CB_WIN * LANE // (2 * KV_WIDTH))),
        pl.BlockSpec((tq, LANE), lambda b, pt: (r0 + b, gate_cb)),
        pl.BlockSpec((1,) + kc.shape[1:], lambda b, pt: (b, 0, 0, 0)),
        pl.BlockSpec((1,) + vc.shape[1:], lambda b, pt: (b, 0, 0, 0)),
        pl.BlockSpec((1, win_buf, 2 * KV_WIDTH), lambda b, pt: (b, 0, 0)),
    ]
    return pl.pallas_call(
        functools.partial(_sample_attn_kernel, n_pages=n_pages, tq=tq, past_len=past_len,
                          win_buf=win_buf),
        out_shape=jax.ShapeDtypeStruct((db * tq, ATTN_WIDTH), F32),
        grid_spec=pltpu.PrefetchScalarGridSpec(
            num_scalar_prefetch=1, grid=(db,), in_specs=in_specs,
            out_specs=pl.BlockSpec((tq, ATTN_WIDTH), lambda b, pt: (b, 0))),
        compiler_params=_cparams(("parallel",)),
    )(page_table, *([cache] * n_pages), proj, proj, proj, proj, kc, vc, state_win)


def _conv_kernel(*refs, seq_len, has_state):
    if has_state:
        b_ref, c_ref, h_ref, w_ref, s1_ref, s2_ref, y_ref, u_ref = refs
    else:
        b_ref, c_ref, h_ref, w_ref, y_ref, u_ref = refs
    u = c_ref[...] * h_ref[...]
    rows = u.shape[0]
    t = lax.broadcasted_iota(I32, (rows, 1), 0) % seq_len
    p1 = jnp.where(t >= 1, pltpu.roll(u, 1, 0), 0.0)
    p2 = jnp.where(t >= 2, pltpu.roll(u, 2, 0), 0.0)
    if has_state:
        p1 = p1 + s1_ref[...]
        p2 = p2 + s2_ref[...]
    y = w_ref[0:1, :] * p2
    y = y + w_ref[1:2, :] * p1
    y = y + w_ref[2:3, :] * u
    y_ref[...] = b_ref[...] * y
    u_ref[...] = u[rows - u_ref.shape[0]:, :]


def _short_conv(proj, conv_w, row_block0, n_blocks, block_rows, seq_len, state=None):
    conv_dim = conv_w.shape[1]
    n_ct = conv_dim // LANE
    tail = SUBLANE if state is None else block_rows
    spec = lambda cb: pl.BlockSpec((block_rows, LANE), lambda r, c: (row_block0 + r, cb + c))
    in_specs = [spec(CB_CONV), spec(CB_CONV + n_ct), spec(CB_CONV + 2 * n_ct),
                pl.BlockSpec((CONV_WIDTH, LANE), lambda r, c: (0, c))]
    args = [proj, proj, proj, conv_w]
    if state is not None:
        st_spec = pl.BlockSpec((block_rows, LANE), lambda r, c: (r, c))
        in_specs += [st_spec, st_spec]
        args += list(state)
    return pl.pallas_call(
        functools.partial(_conv_kernel, seq_len=seq_len, has_state=state is not None),
        out_shape=(jax.ShapeDtypeStruct((n_blocks * block_rows, conv_dim), F32),
                   jax.ShapeDtypeStruct((n_blocks * tail, conv_dim), F32)),
        grid=(n_blocks, n_ct),
        in_specs=in_specs,
        out_specs=(pl.BlockSpec((block_rows, LANE), lambda r, c: (r, c)),
                   pl.BlockSpec((tail, LANE), lambda r, c: (r, c))),
        compiler_params=_cparams(("parallel", "parallel")),
    )(*args)


def _layer_norm(x, g, b):
    mu = jnp.mean(x, axis=-1, keepdims=True)
    var = jnp.mean(jnp.square(x - mu), axis=-1, keepdims=True)
    return (x - mu) * lax.rsqrt(var + LN_EPS) * g + b


def _mix_kernel(oa_ref, oc_ref, x_ref, wa_ref, wc_ref, g_ref, b_ref, wr_ref, br_ref,
                x1_ref, x1b_ref, e_ref, gt_ref, *, dn_alpha, n_experts):
    mix = jnp.dot(oa_ref[...].astype(BF16), wa_ref[...], preferred_element_type=F32)
    mix = mix + jnp.dot(oc_ref[...].astype(BF16), wc_ref[...], preferred_element_type=F32)
    x1 = _layer_norm(dn_alpha * x_ref[...] + mix, g_ref[...], b_ref[...])
    x1_ref[...] = x1
    x1b_ref[...] = x1.astype(BF16)
    logits = jnp.dot(x1, wr_ref[...], preferred_element_type=F32,
                     precision=lax.Precision.HIGHEST) + br_ref[...]
    tm = logits.shape[0]
    lane = lax.broadcasted_iota(I32, (tm, LANE), 1)
    score = jnp.where(lane < n_experts, logits, -jnp.inf)
    e_out = jnp.zeros((tm, LANE), I32)
    v_out = jnp.zeros((tm, LANE), F32)
    v0 = None
    den = jnp.zeros((tm, 1), F32)
    for k in range(TOP_K):
        m = jnp.max(score, axis=-1, keepdims=True)
        idx = jnp.min(jnp.where(score == m, lane, LANE), axis=-1, keepdims=True)
        if k == 0:
            v0 = m
        ex = jnp.exp(m - v0)
        den = den + ex
        e_out = jnp.where(lane == k, idx, e_out)
        v_out = jnp.where(lane == k, ex, v_out)
        score = jnp.where(lane == idx, -jnp.inf, score)
    e_ref[...] = e_out
    gt_ref[...] = v_out / den


def _mix_ln_router(o_attn, o_conv, x, w_out_a, w_out_c, ln_g, ln_b, w_r, b_r, dn_alpha, n_experts):
    n, d = x.shape
    tm = _pick(n, 256)
    row = lambda w: pl.BlockSpec((tm, w), lambda i: (i, 0))
    full = lambda a: pl.BlockSpec(a.shape, lambda i: (0, 0))
    return pl.pallas_call(
        functools.partial(_mix_kernel, dn_alpha=dn_alpha, n_experts=n_experts),
        out_shape=(jax.ShapeDtypeStruct((n, d), F32), jax.ShapeDtypeStruct((n, d), BF16),
                   jax.ShapeDtypeStruct((n, LANE), I32), jax.ShapeDtypeStruct((n, LANE), F32)),
        grid=(n // tm,),
        in_specs=[row(o_attn.shape[1]), row(o_conv.shape[1]), row(d), full(w_out_a), full(w_out_c),
                  full(ln_g), full(ln_b), full(w_r), full(b_r)],
        out_specs=(row(d), row(d), row(LANE), row(LANE)),
        compiler_params=_cparams(("parallel",)),
    )(o_attn, o_conv, x, w_out_a, w_out_c, ln_g, ln_b, w_r, b_r)


def _gate_up_kernel(be_ref, nu_ref, x_ref, wg_ref, wu_ref, bg_ref, bu_ref, a_ref):
    m = pl.program_id(1)

    @pl.when(m < nu_ref[0])
    def _():
        x = x_ref[...]
        hg = jnp.dot(x, wg_ref[0], preferred_element_type=F32) + bg_ref[0]
        hu = jnp.dot(x, wu_ref[0], preferred_element_type=F32) + bu_ref[0]
        hg = jnp.minimum(hg, SWIGLU_LIMIT)
        hu = jnp.clip(hu, -SWIGLU_LIMIT, SWIGLU_LIMIT)
        a_ref[...] = (hg * jax.nn.sigmoid(SWIGLU_ALPHA * hg) * (hu + 1.0)).astype(BF16)

    @pl.when(m >= nu_ref[0])
    def _():
        a_ref[...] = jnp.zeros_like(a_ref)


def _down_kernel(be_ref, nu_ref, a_ref, w_ref, b_ref, y_ref):
    m = pl.program_id(1)

    @pl.when(m < nu_ref[0])
    def _():
        y_ref[...] = jnp.dot(a_ref[...], w_ref[0], preferred_element_type=F32) + b_ref[0]

    @pl.when(m >= nu_ref[0])
    def _():
        y_ref[...] = jnp.zeros_like(y_ref)


def _experts(block_e, n_used, xr, w_gu, b_gu, w_down, b_down):
    rows, d = xr.shape
    d_ff = w_down.shape[1]
    n_blocks = rows // MOE_BLOCK
    tf = _pick(d_ff, 1024, LANE)
    nf = d_ff // tf
    a = pl.pallas_call(
        _gate_up_kernel,
        out_shape=jax.ShapeDtypeStruct((rows, d_ff), BF16),
        grid_spec=pltpu.PrefetchScalarGridSpec(
            num_scalar_prefetch=2, grid=(nf, n_blocks),
            in_specs=[pl.BlockSpec((MOE_BLOCK, d), lambda f, m, be, nu: (m, 0)),
                      pl.BlockSpec((1, d, tf), lambda f, m, be, nu: (be[m], 0, f)),
                      pl.BlockSpec((1, d, tf), lambda f, m, be, nu: (be[m], 0, nf + f)),
                      pl.BlockSpec((1, 1, tf), lambda f, m, be, nu: (be[m], 0, f)),
                      pl.BlockSpec((1, 1, tf), lambda f, m, be, nu: (be[m], 0, nf + f))],
            out_specs=pl.BlockSpec((MOE_BLOCK, tf), lambda f, m, be, nu: (m, f))),
        compiler_params=_cparams(("parallel", "arbitrary")),
    )(block_e, n_used, xr, w_gu, w_gu, b_gu, b_gu)
    tn = _pick(d, 1024, LANE)
    return pl.pallas_call(
        _down_kernel,
        out_shape=jax.ShapeDtypeStruct((rows, d), F32),
        grid_spec=pltpu.PrefetchScalarGridSpec(
            num_scalar_prefetch=2, grid=(d // tn, n_blocks),
            in_specs=[pl.BlockSpec((MOE_BLOCK, d_ff), lambda j, m, be, nu: (m, 0)),
                      pl.BlockSpec((1, d_ff, tn), lambda j, m, be, nu: (be[m], 0, j)),
                      pl.BlockSpec((1, 1, tn), lambda j, m, be, nu: (be[m], 0, j))],
            out_specs=pl.BlockSpec((MOE_BLOCK, tn), lambda j, m, be, nu: (m, j))),
        compiler_params=_cparams(("parallel", "arbitrary")),
    )(block_e, n_used, a, w_down, b_down)


def _combine_kernel(x1_ref, y_ref, gt_ref, g_ref, b_ref, o_ref, *, dn_alpha):
    gt = gt_ref[...]
    moe = y_ref[0] * gt[:, 0:1]
    for k in range(1, TOP_K):
        moe = moe + y_ref[k] * gt[:, k:k + 1]
    o_ref[...] = _layer_norm(dn_alpha * x1_ref[...] + moe, g_ref[...], b_ref[...])


def _combine_ln(x1, yk, gates, ln_g, ln_b, dn_alpha):
    n, d = x1.shape
    tm = _pick(n, 256)
    return pl.pallas_call(
        functools.partial(_combine_kernel, dn_alpha=dn_alpha),
        out_shape=jax.ShapeDtypeStruct((n, d), F32),
        grid=(n // tm,),
        in_specs=[pl.BlockSpec((tm, d), lambda i: (i, 0)),
                  pl.BlockSpec((TOP_K, tm, d), lambda i: (0, i, 0)),
                  pl.BlockSpec((tm, LANE), lambda i: (i, 0)),
                  pl.BlockSpec((1, d), lambda i: (0, 0)),
                  pl.BlockSpec((1, d), lambda i: (0, 0))],
        out_specs=pl.BlockSpec((tm, d), lambda i: (i, 0)),
        compiler_params=_cparams(("parallel",)),
    )(x1, yk, gates, ln_g, ln_b)


def _route(top_e, n_experts):
    n = top_e.shape[0]
    e_flat = top_e.reshape(-1)
    tok_flat = jnp.repeat(jnp.arange(n, dtype=I32), TOP_K)
    order = jnp.argsort(e_flat, stable=True)
    e_s, tok_s = e_flat[order], tok_flat[order]
    counts = jnp.bincount(e_flat, length=n_experts).astype(I32)
    starts = jnp.cumsum(counts) - counts
    padded = (counts + MOE_BLOCK - 1) // MOE_BLOCK * MOE_BLOCK
    pends = jnp.cumsum(padded)
    pstarts = pends - padded
    dest = pstarts[e_s] + (jnp.arange(n * TOP_K, dtype=I32) - starts[e_s])
    n_blocks = -(-(n * TOP_K) // MOE_BLOCK) + n_experts
    row_tok = jnp.zeros((n_blocks * MOE_BLOCK,), I32).at[dest].set(tok_s)
    pos = jnp.zeros((n * TOP_K,), I32).at[order].set(dest).reshape(n, TOP_K)
    block_e = jnp.minimum(
        jnp.searchsorted(pends, jnp.arange(n_blocks, dtype=I32) * MOE_BLOCK, side='right'),
        n_experts - 1).astype(I32)
    n_used = (pends[-1:] // MOE_BLOCK).astype(I32)
    return row_tok, pos, block_e, n_used


def _layer(x_prompt, x_sample, cache, state_win, state_conv, page_table,
           w_in, w_out, conv_w, cmp_k_pe, cmp_k_w1, cmp_k_w2, cmp_v_pe, cmp_v_w1, cmp_v_w2,
           ln1_g, ln1_b, w_router, b_router, w_gu, b_gu, w_down, b_down, ln2_g, ln2_b, depth):
    nb, seq_len, d = x_prompt.shape
    db, dec_seq, _ = x_sample.shape
    n_pool, page = cache.shape[:2]
    n_pages = page_table.shape[1]
    past_len = n_pages * page
    win_buf = state_win.shape[1]
    conv_dim = conv_w.shape[1]
    n_experts = w_router.shape[1]
    n_p, n_s = nb * seq_len, db * dec_seq
    dn_alpha = (2.0 * depth) ** 0.25
    assert page == LANE and dec_seq == SUBLANE and seq_len % Q_BLOCK == 0
    assert seq_len >= win_buf and seq_len >= WINDOW + Q_BLOCK and win_buf == WINDOW
    assert n_p % n_s == 0 and conv_dim % LANE == 0 and n_experts <= LANE
    assert past_len + dec_seq <= 2 * 32 * SEL_BLOCK and seq_len <= 32 * SEL_BLOCK

    off_gate = ATTN_WIDTH + N_KV_COMP * KV_WIDTH + 2 * KV_WIDTH
    off_conv = off_gate + N_GATE
    w_p = jnp.concatenate([w_in[:, :off_gate], w_in[:, off_conv:], w_in[:, off_gate:off_conv],
                           jnp.zeros((d, LANE - N_GATE), w_in.dtype)], axis=1).astype(BF16)
    n_cb = w_p.shape[1] // LANE
    x_all = jnp.concatenate([x_prompt.reshape(n_p, d), x_sample.reshape(n_s, d)], axis=0)
    n_tok = n_p + n_s
    proj = _proj(x_all, w_p, _pick(n_tok, 512), _pick(w_p.shape[1], 1152, LANE))

    kv_lo, kv_hi = ATTN_WIDTH, ATTN_WIDTH + N_KV_COMP * KV_WIDTH
    kv_shape = (N_KV_COMP, N_KV_HEADS, HEAD_DIM)
    kv_prompt = proj[:n_p, kv_lo:kv_hi].reshape((nb, seq_len) + kv_shape)
    kv_sample = proj[n_p:, kv_lo:kv_hi].reshape((db, dec_seq) + kv_shape)
    win_shape = (2, N_KV_HEADS, HEAD_DIM)
    win_new_p = proj[:n_p, kv_hi:off_gate].reshape(nb, seq_len, 2 * KV_WIDTH)
    win_prompt = win_new_p[:, seq_len - win_buf:].reshape((nb, win_buf) + win_shape)
    win_new_s = proj[n_p:, kv_hi:off_gate].reshape(db, dec_seq, 2 * KV_WIDTH)
    win_sample = jnp.concatenate([state_win[:, dec_seq:], win_new_s], axis=1)
    win_sample = win_sample.reshape((db, win_buf) + win_shape)

    w1k = jnp.concatenate(jnp.split(cmp_k_w1, CMP_RATIO, axis=0), axis=1).astype(BF16)
    w1v = jnp.concatenate(jnp.split(cmp_v_w1, CMP_RATIO, axis=0), axis=1).astype(BF16)
    bk = _pe_bias(cmp_k_pe, cmp_k_w1.astype(BF16))
    bv = _pe_bias(cmp_v_pe, cmp_v_w1.astype(BF16))
    w2k, w2v = cmp_k_w2.astype(BF16), cmp_v_w2.astype(BF16)
    pp = seq_len // LANE
    prompt_ids = jnp.arange(nb * pp, dtype=I32).reshape(nb, pp)
    proj_pages = proj.reshape(n_tok // LANE, LANE, n_cb * LANE)
    kc_p, vc_p = _compress(prompt_ids, proj_pages, kv_lo // (2 * KV_WIDTH), w1k, w1v, bk, bv, w2k, w2v)
    kc_s, vc_s = _compress(page_table, cache, 0, w1k, w1v, bk, bv, w2k, w2v)

    o_p = _prompt_attention(proj, kc_p, vc_p, nb, seq_len, n_cb)
    o_s = _sample_attention(page_table, cache, proj, kc_s, vc_s, state_win, n_p, dec_seq, n_cb)
    o_attn = jnp.concatenate([o_p, o_s], axis=0)

    y_p, u_p = _short_conv(proj, conv_w, 0, nb, seq_len, seq_len)
    z = jnp.zeros((db, dec_seq - 2, conv_dim), F32)
    s1 = jnp.concatenate([state_conv[:, 1:2], z, z[:, :1]], axis=1).reshape(n_s, conv_dim)
    s2 = jnp.concatenate([state_conv, z], axis=1).reshape(n_s, conv_dim)
    y_s, u_s = _short_conv(proj, conv_w, n_p // n_s, 1, n_s, dec_seq, state=(s1, s2))
    o_conv = jnp.concatenate([y_p, y_s], axis=0)
    conv_prompt = u_p.reshape(nb, SUBLANE, conv_dim)[:, SUBLANE - (CONV_WIDTH - 1):]
    conv_sample = u_s.reshape(db, dec_seq, conv_dim)[:, dec_seq - (CONV_WIDTH - 1):]

    w_r = jnp.pad(w_router, ((0, 0), (0, LANE - n_experts)))
    b_r = jnp.pad(b_router, (0, LANE - n_experts)).reshape(1, LANE)
    x1, x1b, e_idx, gates = _mix_ln_router(
        o_attn, o_conv, x_all, w_out[:ATTN_WIDTH].astype(BF16), w_out[ATTN_WIDTH:].astype(BF16),
        ln1_g.reshape(1, d), ln1_b.reshape(1, d), w_r, b_r, dn_alpha, n_experts)

    row_tok, pos, block_e, n_used = _route(e_idx[:, :TOP_K], n_experts)
    y_rows = _experts(block_e, n_used, x1b[row_tok], w_gu.astype(BF16),
                      b_gu.reshape(n_experts, 1, -1), w_down.astype(BF16),
                      b_down.reshape(n_experts, 1, -1))

    x2 = _combine_ln(x1, y_rows[pos.T], gates, ln2_g.reshape(1, d), ln2_b.reshape(1, d), dn_alpha)
    y_prompt = x2[:n_p].reshape(nb, seq_len, d)
    y_sample = x2[n_p:].reshape(db, dec_seq, d)
    return (y_prompt, y_sample, kv_prompt, kv_sample, win_prompt, win_sample,
            conv_prompt, conv_sample)


def kernel(x_prompt, x_sample, cache_kv, state_win, state_conv, page_table, w_in, w_out, conv_w, cmp_k_pe, cmp_k_w1, cmp_k_w2, cmp_v_pe, cmp_v_w1, cmp_v_w2, ln1_g, ln1_b, w_router, b_router, w_gu, b_gu, w_down, b_down, ln2_g, ln2_b):
    depth = w_in.shape[0]
    y_prompt, y_sample = x_prompt, x_sample
    outs = [[] for _ in range(6)]
    for layer in range(depth):
        cache = cache_kv[layer]
        cache = cache.reshape(cache.shape[0], cache.shape[1], N_KV_COMP * KV_WIDTH)
        win = state_win[layer]
        win = win.reshape(win.shape[0], win.shape[1], 2 * KV_WIDTH)
        res = _layer(y_prompt, y_sample, cache, win, state_conv[layer], page_table,
                     w_in[layer], w_out[layer], conv_w[layer],
                     cmp_k_pe[layer], cmp_k_w1[layer], cmp_k_w2[layer],
                     cmp_v_pe[layer], cmp_v_w1[layer], cmp_v_w2[layer],
                     ln1_g[layer], ln1_b[layer], w_router[layer], b_router[layer],
                     w_gu[layer], b_gu[layer], w_down[layer], b_down[layer],
                     ln2_g[layer], ln2_b[layer], depth)
        y_prompt, y_sample = res[0], res[1]
        for lst, val in zip(outs, res[2:]):
            lst.append(val)
    return (y_prompt, y_sample) + tuple(jnp.stack(o) for o in outs)
```

```python
import functools
import math

import numpy as np
import jax
import jax.numpy as jnp
from jax import lax
from jax.experimental import pallas as pl
from jax.experimental.pallas import tpu as pltpu

F32 = jnp.float32
BF16 = jnp.bfloat16
I32 = jnp.int32

HEAD_DIM = 128
N_HEADS = 8
N_KV_HEADS = 2
GQA_REP = N_HEADS // N_KV_HEADS
N_KV_COMP = 4
KV_WIDTH = N_KV_HEADS * HEAD_DIM
ATTN_WIDTH = N_HEADS * HEAD_DIM
CONV_WIDTH = 3
CMP_BLOCK = 32
CMP_STRIDE = 16
CMP_RATIO = CMP_BLOCK // CMP_STRIDE
CMP_HIDDEN = 256
SEL_BLOCK = 64
SEL_TOPK = 8
WINDOW = 512
N_BRANCH = 3
Q_BLOCK = 128
TOP_K = 4
SWIGLU_LIMIT = 7.0
SWIGLU_ALPHA = 1.702
MOE_TM = 256
LN_EPS = 1e-5
NEG = -1e30
BIG = 1e30
LANE = 128
SUBLANE = 8
VMEM_LIMIT = 56 * 1024 * 1024
SEL_CHUNK = 512

CB_KV = ATTN_WIDTH // LANE
CB_WIN = CB_KV + N_KV_COMP * N_KV_HEADS
CB_CONV = CB_WIN + 2 * N_KV_HEADS
N_GATE = N_BRANCH * N_HEADS


def _alibi_slopes(n):
    def pow2(m):
        start = 2.0 ** (-8.0 / m)
        return [start ** (i + 1) for i in range(m)]
    if math.log2(n).is_integer():
        s = pow2(n)
    else:
        c = 2 ** math.floor(math.log2(n))
        s = pow2(c) + pow2(2 * c)[0::2][:n - c]
    return [float(np.float32(v)) for v in s]


SLOPES = _alibi_slopes(N_HEADS)


def _pick(n, target, mult=SUBLANE):
    for d in range(min(n, target), 0, -1):
        if n % d == 0 and d % mult == 0:
            return d
    raise ValueError((n, target, mult))


def _cparams(sem):
    return pltpu.CompilerParams(dimension_semantics=sem, vmem_limit_bytes=VMEM_LIMIT)


def _proj_kernel(x_ref, w_ref, o_ref):
    o_ref[...] = jnp.dot(x_ref[...].astype(BF16), w_ref[...], preferred_element_type=F32)


def _proj(x, w, tm, tn):
    m, k = x.shape
    n = w.shape[1]
    return pl.pallas_call(
        _proj_kernel,
        out_shape=jax.ShapeDtypeStruct((m, n), F32),
        grid=(m // tm, n // tn),
        in_specs=[pl.BlockSpec((tm, k), lambda i, j: (i, 0)),
                  pl.BlockSpec((k, tn), lambda i, j: (0, j))],
        out_specs=pl.BlockSpec((tm, tn), lambda i, j: (i, j)),
        compiler_params=_cparams(("parallel", "arbitrary")),
    )(x, w)


def _pe_bias_kernel(pe_ref, w1_ref, o_ref):
    o_ref[...] = jnp.dot(pe_ref[...].astype(BF16), w1_ref[...], preferred_element_type=F32)


def _pe_bias(pe, w1_bf):
    flat = jnp.broadcast_to(pe.reshape(1, -1), (SUBLANE, pe.size))
    return pl.pallas_call(
        _pe_bias_kernel,
        out_shape=jax.ShapeDtypeStruct((SUBLANE, CMP_HIDDEN), F32),
    )(flat, w1_bf)


def _compress_core(raw, weights, n_chunk):
    outs = []
    for kv, (w1_ref, b_ref, w2_ref) in enumerate(weights):
        acc = jnp.zeros((N_KV_HEADS * n_chunk, CMP_RATIO * CMP_HIDDEN), F32)
        for sp in range(CMP_STRIDE // 2):
            per_g = []
            for g in range(N_KV_HEADS):
                halves = [raw[kv * N_KV_HEADS + g, pl.ds(s, n_chunk, stride=CMP_STRIDE), :]
                          for s in (2 * sp, 2 * sp + 1)]
                per_g.append(jnp.concatenate(halves, axis=1))
            lhs = jnp.concatenate(per_g, axis=0).astype(BF16)
            acc = acc + jnp.dot(lhs, w1_ref[pl.ds(sp * 2 * HEAD_DIM, 2 * HEAD_DIM), :],
                                preferred_element_type=F32)
        per_kv = []
        for g in range(N_KV_HEADS):
            part0 = acc[g * n_chunk:(g + 1) * n_chunk, :CMP_HIDDEN]
            part1 = acc[g * n_chunk:(g + 1) * n_chunk, CMP_HIDDEN:]
            part1 = jnp.concatenate([part1[1:], part1[:1]], axis=0)
            pre = (b_ref[0:1, :] + part0) + part1
            per_kv.append(jnp.dot(jax.nn.gelu(pre).astype(BF16), w2_ref[...],
                                  preferred_element_type=F32))
        outs.append(per_kv)
    return outs


def _compress_kernel(ids_ref, *refs, n_pages):
    pages = refs[:n_pages]
    w1k_ref, w1v_ref, bk_ref, bv_ref, w2k_ref, w2v_ref, kc_ref, vc_ref, raw = refs[n_pages:]
    n_chunk = n_pages * (LANE // CMP_STRIDE)
    for cb in range(2 * N_KV_HEADS):
        for p in range(n_pages):
            raw[cb, p * LANE:(p + 1) * LANE, :] = pages[p][0, :, cb * HEAD_DIM:(cb + 1) * HEAD_DIM]
    kc, vc = _compress_core(raw, ((w1k_ref, bk_ref, w2k_ref), (w1v_ref, bv_ref, w2v_ref)), n_chunk)
    for g in range(N_KV_HEADS):
        kc_ref[0, g] = kc[g]
        vc_ref[0, g] = vc[g]


def _compress(page_ids, src, col_block, w1k, w1v, bk, bv, w2k, w2v):
    nb, n_pages = page_ids.shape
    n_chunk = n_pages * (LANE // CMP_STRIDE)
    width = 2 * KV_WIDTH

    def page_map(p):
        return lambda b, ids: (ids[b, p], 0, col_block)

    full = lambda shape: pl.BlockSpec(shape, lambda b, ids: (0,) * len(shape))
    in_specs = [pl.BlockSpec((1, LANE, width), page_map(p)) for p in range(n_pages)]
    in_specs += [full(w1k.shape), full(w1v.shape), full(bk.shape), full(bv.shape),
                 full(w2k.shape), full(w2v.shape)]
    out_spec = pl.BlockSpec((1, N_KV_HEADS, n_chunk, HEAD_DIM), lambda b, ids: (b, 0, 0, 0))
    out_sd = jax.ShapeDtypeStruct((nb, N_KV_HEADS, n_chunk, HEAD_DIM), F32)
    return pl.pallas_call(
        functools.partial(_compress_kernel, n_pages=n_pages),
        out_shape=(out_sd, out_sd),
        grid_spec=pltpu.PrefetchScalarGridSpec(
            num_scalar_prefetch=1, grid=(nb,), in_specs=in_specs,
            out_specs=(out_spec, out_spec),
            scratch_shapes=[pltpu.VMEM((2 * N_KV_HEADS, n_pages * LANE, HEAD_DIM), F32)]),
        compiler_params=_cparams(("parallel",)),
    )(page_ids, *([src] * n_pages), w1k, w1v, bk, bv, w2k, w2v)


def _nt_dot(a, b):
    return lax.dot_general(a, b, (((1,), (1,)), ((), ())), preferred_element_type=F32)


def _masked_softmax(s, mask):
    s = jnp.where(mask, s, NEG)
    e = jnp.where(mask, jnp.exp(s - jnp.max(s, axis=-1, keepdims=True)), 0.0)
    den = jnp.sum(e, axis=-1, keepdims=True)
    return e / jnp.where(den > 0, den, 1.0)


def _stack_heads(q, tq):
    q = q * (HEAD_DIM ** -0.5)
    return jnp.concatenate([q[:, r * HEAD_DIM:(r + 1) * HEAD_DIM] for r in range(GQA_REP)],
                           axis=0).astype(BF16)


def _row_consts(g, tq, p0):
    rows = GQA_REP * tq
    ridx = lax.broadcasted_iota(I32, (rows, 1), 0)
    slope = jnp.zeros((rows, 1), F32)
    for r in range(GQA_REP):
        slope = jnp.where(ridx // tq == r, SLOPES[g * GQA_REP + r], slope)
    t = p0 + ridx % tq
    return slope, t


def _cmp_branch(q4, kc, vc, slope, t, tq):
    n_chunk = kc.shape[0]
    s = _nt_dot(q4, kc.astype(BF16))
    c_end = lax.broadcasted_iota(I32, (1, n_chunk), 1) * CMP_STRIDE + (CMP_BLOCK - 1)
    rel = t - c_end
    s = s - slope * rel.astype(F32)
    p = _masked_softmax(s, rel >= 0)
    o_c = jnp.dot(p.astype(BF16), vc.astype(BF16), preferred_element_type=F32)
    p_sum = p[0:tq]
    for r in range(1, GQA_REP):
        p_sum = p_sum + p[r * tq:(r + 1) * tq]
    return o_c, p_sum


def _select_blocks(p_sum, t_q, n_sel):
    tq, n_chunk = p_sum.shape
    ci = lax.broadcasted_iota(I32, (n_chunk, LANE), 0) * CMP_STRIDE
    sj = lax.broadcasted_iota(I32, (n_chunk, LANE), 1) * SEL_BLOCK
    overlap = jnp.maximum(jnp.minimum(ci + CMP_BLOCK, sj + SEL_BLOCK) - jnp.maximum(ci, sj), 0)
    overlap = overlap.astype(F32).astype(BF16)
    p_hi = p_sum.astype(BF16)
    p_lo = (p_sum - p_hi.astype(F32)).astype(BF16)
    imp = (jnp.dot(p_hi, overlap, preferred_element_type=F32)
           + jnp.dot(p_lo, overlap, preferred_element_type=F32))
    lane = lax.broadcasted_iota(I32, (tq, LANE), 1)
    cur = t_q // SEL_BLOCK
    valid = (lane <= cur) & (lane < n_sel)
    forced = (lane == 0) | (lane == cur) | (lane == cur - 1)
    score = jnp.where(valid, jnp.where(forced, BIG, imp), -1.0)
    lo = jnp.zeros((tq, 1), I32)
    hi = jnp.zeros((tq, 1), I32)
    for _ in range(min(SEL_TOPK, n_sel)):
        m = jnp.max(score, axis=-1, keepdims=True)
        idx = jnp.min(jnp.where(score == m, lane, LANE), axis=-1, keepdims=True)
        ok = m >= 0.0
        bit = lax.shift_left(jnp.ones_like(idx), idx & 31)
        lo = lo | jnp.where(ok & (idx < 32), bit, 0)
        hi = hi | jnp.where(ok & (idx >= 32), bit, 0)
        score = jnp.where(lane == idx, -1.0, score)
    return lo, hi


def _sel_mask(lo4, hi4, pos, t, n_sel):
    blk = pos // SEL_BLOCK
    word = lo4 if n_sel <= 32 else jnp.where(blk < 32, lo4, hi4)
    bits = lax.shift_right_logical(jnp.broadcast_to(word, (word.shape[0], pos.shape[1])),
                                   jnp.broadcast_to(blk & 31, (word.shape[0], pos.shape[1])))
    return (pos <= t) & ((bits & 1) == 1)


def _rep_rows(x):
    return jnp.concatenate([x] * GQA_REP, axis=0)


def _gated_sum(sig, g, o_c, o_s, o_w, tq):
    outs = []
    for r in range(GQA_REP):
        h = g * GQA_REP + r
        sl = slice(r * tq, (r + 1) * tq)
        g0 = sig[:, N_BRANCH * h + 0:N_BRANCH * h + 1]
        g1 = sig[:, N_BRANCH * h + 1:N_BRANCH * h + 2]
        g2 = sig[:, N_BRANCH * h + 2:N_BRANCH * h + 3]
        outs.append(g0 * o_c[sl] + g1 * o_s[sl] + g2 * o_w[sl])
    return outs


def _prompt_attn_kernel(q_ref, gate_ref, kc_ref, vc_ref, ks_ref, vs_ref, kw_ref, vw_ref,
                        o_ref, ksb, vsb, kwb, vwb, *, tq, seq_len):
    g = pl.program_id(1)
    i = pl.program_id(2)
    n_sel = -(-seq_len // SEL_BLOCK)

    @pl.when(i == 0)
    def _():
        ksb[...] = ks_ref[...].astype(BF16)
        vsb[...] = vs_ref[...].astype(BF16)
        kwb[...] = kw_ref[...].astype(BF16)
        vwb[...] = vw_ref[...].astype(BF16)

    p0 = i * tq
    q4 = _stack_heads(q_ref[...], tq)
    rows = GQA_REP * tq
    ridx = lax.broadcasted_iota(I32, (rows, 1), 0)
    t = p0 + ridx % tq
    slope = jnp.zeros((rows, 1), F32)
    for gg in range(N_KV_HEADS):
        for r in range(GQA_REP):
            slope = jnp.where((ridx // tq == r) & (g == gg), SLOPES[gg * GQA_REP + r], slope)

    o_c, p_sum = _cmp_branch(q4, kc_ref[0, 0], vc_ref[0, 0], slope, t, tq)
    lo, hi = _select_blocks(p_sum, t[0:tq], n_sel)
    lo4, hi4 = _rep_rows(lo), _rep_rows(hi)

    def chunk(c, carry):
        m, l, acc = carry
        start = pl.multiple_of(c * SEL_CHUNK, SEL_CHUNK)
        k = ksb[pl.ds(start, SEL_CHUNK), :]
        v = vsb[pl.ds(start, SEL_CHUNK), :]
        pos = start + lax.broadcasted_iota(I32, (1, SEL_CHUNK), 1)
        s = _nt_dot(q4, k) - slope * (t - pos).astype(F32)
        mask = _sel_mask(lo4, hi4, pos, t, n_sel)
        s = jnp.where(mask, s, NEG)
        m_new = jnp.maximum(m, jnp.max(s, axis=-1, keepdims=True))
        a = jnp.exp(m - m_new)
        e = jnp.where(mask, jnp.exp(s - m_new), 0.0)
        l = a * l + jnp.sum(e, axis=-1, keepdims=True)
        acc = a * acc + jnp.dot(e.astype(BF16), v, preferred_element_type=F32)
        return m_new, l, acc

    n_chunks = (p0 + tq - 1) // SEL_CHUNK + 1
    m0 = jnp.full((rows, 1), NEG, F32)
    l0 = jnp.zeros((rows, 1), F32)
    a0 = jnp.zeros((rows, HEAD_DIM), F32)
    _, l, acc = lax.fori_loop(0, n_chunks, chunk, (m0, l0, a0))
    o_s = acc / jnp.where(l > 0, l, 1.0)

    n_win = WINDOW + tq
    start = pl.multiple_of(jnp.maximum(p0 - WINDOW, 0), tq)
    kw = kwb[pl.ds(start, n_win), :]
    vw = vwb[pl.ds(start, n_win), :]
    kpos = start + lax.broadcasted_iota(I32, (1, n_win), 1)
    rel = t - kpos
    s = _nt_dot(q4, kw) - slope * rel.astype(F32)
    p = _masked_softmax(s, (rel >= 0) & (rel < WINDOW))
    o_w = jnp.dot(p.astype(BF16), vw, preferred_element_type=F32)

    sig = jax.nn.sigmoid(gate_ref[...])
    for gg in range(N_KV_HEADS):
        @pl.when(g == gg)
        def _(gg=gg):
            outs = _gated_sum(sig, gg, o_c, o_s, o_w, tq)
            for r in range(GQA_REP):
                o_ref[:, r * HEAD_DIM:(r + 1) * HEAD_DIM] = outs[r]


def _prompt_attention(proj, kc, vc, n_batch, seq_len, n_cols_blocks):
    tq = Q_BLOCK
    nqb = seq_len // tq
    gate_cb = n_cols_blocks - 1
    qw = GQA_REP * HEAD_DIM
    kv_spec = lambda cb: pl.BlockSpec((seq_len, HEAD_DIM), lambda b, g, i: (b, cb + g))
    c_spec = pl.BlockSpec((1, 1) + kc.shape[2:], lambda b, g, i: (b, g, 0, 0))
    return pl.pallas_call(
        functools.partial(_prompt_attn_kernel, tq=tq, seq_len=seq_len),
        out_shape=jax.ShapeDtypeStruct((n_batch * seq_len, ATTN_WIDTH), F32),
        grid=(n_batch, N_KV_HEADS, nqb),
        in_specs=[pl.BlockSpec((tq, qw), lambda b, g, i: (b * nqb + i, g)),
                  pl.BlockSpec((tq, LANE), lambda b, g, i: (b * nqb + i, gate_cb)),
                  c_spec, c_spec,
                  kv_spec(CB_KV + 2 * N_KV_HEADS), kv_spec(CB_KV + 3 * N_KV_HEADS),
                  kv_spec(CB_WIN), kv_spec(CB_WIN + N_KV_HEADS)],
        out_specs=pl.BlockSpec((tq, qw), lambda b, g, i: (b * nqb + i, g)),
        scratch_shapes=[pltpu.VMEM((seq_len, HEAD_DIM), BF16)] * 4,
        compiler_params=_cparams(("parallel", "parallel", "arbitrary")),
    )(proj, proj, kc, vc, proj, proj, proj, proj)


def _sample_attn_kernel(pt_ref, *refs, n_pages, tq, past_len, win_buf):
    pages = refs[:n_pages]
    (q_ref, kvn_ref, winn_ref, gate_ref, win_ref, w1k_ref, w1v_ref, bk_ref, bv_ref, w2k_ref,
     w2v_ref, o_ref, kvo_ref, wino_ref, raw) = refs[n_pages:]
    seq_len = past_len + tq
    n_sel = -(-seq_len // SEL_BLOCK)
    n_planes = N_KV_COMP * N_KV_HEADS
    n_wplanes = 2 * N_KV_HEADS
    page = pages[0].shape[1] // n_planes
    n_chunk = past_len // CMP_STRIDE
    zpad = jnp.zeros((LANE - tq, HEAD_DIM), BF16)

    for c in range(n_planes):
        kvo_ref[pl.ds(c, tq, stride=n_planes), :] = kvn_ref[:, c * HEAD_DIM:(c + 1) * HEAD_DIM]
    keep = (win_buf - tq) * n_wplanes
    wino_ref[0, 0:keep, :] = win_ref[0, tq * n_wplanes:win_buf * n_wplanes, :]
    for c in range(n_wplanes):
        wino_ref[0, pl.ds(keep + c, tq, stride=n_wplanes), :] = winn_ref[:, c * HEAD_DIM:(c + 1) * HEAD_DIM]

    for cb in range(2 * N_KV_HEADS):
        for p in range(n_pages):
            raw[cb, p * page:(p + 1) * page, :] = pages[p][0, pl.ds(cb, page, stride=n_planes), :]
    kc, vc = _compress_core(raw, ((w1k_ref, bk_ref, w2k_ref), (w1v_ref, bv_ref, w2v_ref)), n_chunk)

    sig = jax.nn.sigmoid(gate_ref[...])
    for g in range(N_KV_HEADS):
        q4 = _stack_heads(q_ref[:, g * GQA_REP * HEAD_DIM:(g + 1) * GQA_REP * HEAD_DIM], tq)
        slope, t = _row_consts(g, tq, past_len)
        o_c, p_sum = _cmp_branch(q4, kc[g], vc[g], slope, t, tq)
        lo, hi = _select_blocks(p_sum, t[0:tq], n_sel)
        lo4, hi4 = _rep_rows(lo), _rep_rows(hi)

        kpl = 2 * N_KV_HEADS + g
        vpl = 3 * N_KV_HEADS + g
        k = jnp.concatenate(
            [p[0, pl.ds(kpl, page, stride=n_planes), :].astype(BF16) for p in pages]
            + [kvn_ref[:, kpl * HEAD_DIM:(kpl + 1) * HEAD_DIM].astype(BF16), zpad], axis=0)
        v = jnp.concatenate(
            [p[0, pl.ds(vpl, page, stride=n_planes), :].astype(BF16) for p in pages]
            + [kvn_ref[:, vpl * HEAD_DIM:(vpl + 1) * HEAD_DIM].astype(BF16), zpad], axis=0)
        pos = lax.broadcasted_iota(I32, (1, k.shape[0]), 1)
        s = _nt_dot(q4, k) - slope * (t - pos).astype(F32)
        p = _masked_softmax(s, _sel_mask(lo4, hi4, pos, t, n_sel))
        o_s = jnp.dot(p.astype(BF16), v, preferred_element_type=F32)

        kw = jnp.concatenate([win_ref[0, pl.ds(g, win_buf, stride=n_wplanes), :].astype(BF16),
                              winn_ref[:, g * HEAD_DIM:(g + 1) * HEAD_DIM].astype(BF16), zpad],
                             axis=0)
        vp = N_KV_HEADS + g
        vw = jnp.concatenate([win_ref[0, pl.ds(vp, win_buf, stride=n_wplanes), :].astype(BF16),
                              winn_ref[:, vp * HEAD_DIM:(vp + 1) * HEAD_DIM].astype(BF16), zpad],
                             axis=0)
        kpos = (past_len - win_buf) + lax.broadcasted_iota(I32, (1, kw.shape[0]), 1)
        rel = t - kpos
        s = _nt_dot(q4, kw) - slope * rel.astype(F32)
        p = _masked_softmax(s, (rel >= 0) & (rel < WINDOW))
        o_w = jnp.dot(p.astype(BF16), vw, preferred_element_type=F32)

        outs = _gated_sum(sig, g, o_c, o_s, o_w, tq)
        for r in range(GQA_REP):
            h = g * GQA_REP + r
            o_ref[:, h * HEAD_DIM:(h + 1) * HEAD_DIM] = outs[r]


def _sample_attention(page_table, cache, proj, state_win, cmp_w, n_prompt_rows, tq, n_cols_blocks):
    db, n_pages = page_table.shape
    n_planes = N_KV_COMP * N_KV_HEADS
    n_wplanes = 2 * N_KV_HEADS
    page = cache.shape[1] // n_planes
    past_len = n_pages * page
    win_buf = state_win.shape[1] // n_wplanes
    r0 = n_prompt_rows // tq
    gate_cb = n_cols_blocks - 1

    def page_map(p):
        return lambda b, pt: (pt[b, p], 0, 0)

    full = lambda a: pl.BlockSpec(a.shape, lambda b, pt: (0,) * a.ndim)
    in_specs = [pl.BlockSpec((1,) + cache.shape[1:], page_map(p)) for p in range(n_pages)]
    in_specs += [
        pl.BlockSpec((tq, ATTN_WIDTH), lambda b, pt: (r0 + b, 0)),
        pl.BlockSpec((tq, N_KV_COMP * KV_WIDTH), lambda b, pt: (r0 + b, 1)),
        pl.BlockSpec((tq, 2 * KV_WIDTH), lambda b, pt: (r0 + b, CB_WIN * LANE // (2 * KV_WIDTH))),
        pl.BlockSpec((tq, LANE), lambda b, pt: (r0 + b, gate_cb)),
        pl.BlockSpec((1,) + state_win.shape[1:], lambda b, pt: (b, 0, 0)),
    ] + [full(w) for w in cmp_w]
    return pl.pallas_call(
        functools.partial(_sample_attn_kernel, n_pages=n_pages, tq=tq, past_len=past_len,
                          win_buf=win_buf),
        out_shape=(jax.ShapeDtypeStruct((db * tq, ATTN_WIDTH), F32),
                   jax.ShapeDtypeStruct((db * tq * n_planes, HEAD_DIM), F32),
                   jax.ShapeDtypeStruct(state_win.shape, F32)),
        grid_spec=pltpu.PrefetchScalarGridSpec(
            num_scalar_prefetch=1, grid=(db,), in_specs=in_specs,
            out_specs=(pl.BlockSpec((tq, ATTN_WIDTH), lambda b, pt: (b, 0)),
                       pl.BlockSpec((tq * n_planes, HEAD_DIM), lambda b, pt: (b, 0)),
                       pl.BlockSpec((1,) + state_win.shape[1:], lambda b, pt: (b, 0, 0))),
            scratch_shapes=[pltpu.VMEM((2 * N_KV_HEADS, past_len, HEAD_DIM), F32)]),
        compiler_params=_cparams(("parallel",)),
    )(page_table, *([cache] * n_pages), proj, proj, proj, proj, state_win, *cmp_w)


def _interleave_kernel(x_ref, o_ref, *, n_planes):
    rows = x_ref.shape[0]
    for c in range(n_planes):
        o_ref[pl.ds(c, rows, stride=n_planes), :] = x_ref[:, c * LANE:(c + 1) * LANE]


def _interleave(proj, n_blocks, block_rows, n_planes, row_block_of, col_block):
    return pl.pallas_call(
        functools.partial(_interleave_kernel, n_planes=n_planes),
        out_shape=jax.ShapeDtypeStruct((n_blocks * block_rows * n_planes, LANE), F32),
        grid=(n_blocks,),
        in_specs=[pl.BlockSpec((block_rows, n_planes * LANE), lambda i: (row_block_of(i), col_block))],
        out_specs=pl.BlockSpec((block_rows * n_planes, LANE), lambda i: (i, 0)),
        compiler_params=_cparams(("parallel",)),
    )(proj)


def _conv_kernel(*refs, seq_len, has_state):
    if has_state:
        b_ref, c_ref, h_ref, w_ref, s1_ref, s2_ref, y_ref, u_ref = refs
    else:
        b_ref, c_ref, h_ref, w_ref, y_ref, u_ref = refs
    u = c_ref[...] * h_ref[...]
    rows = u.shape[0]
    t = lax.broadcasted_iota(I32, (rows, 1), 0) % seq_len
    p1 = jnp.where(t >= 1, pltpu.roll(u, 1, 0), 0.0)
    p2 = jnp.where(t >= 2, pltpu.roll(u, 2, 0), 0.0)
    if has_state:
        p1 = p1 + s1_ref[...]
        p2 = p2 + s2_ref[...]
    y = w_ref[0:1, :] * p2
    y = y + w_ref[1:2, :] * p1
    y = y + w_ref[2:3, :] * u
    y_ref[...] = b_ref[...] * y
    u_ref[...] = u[rows - u_ref.shape[0]:, :]


def _short_conv(proj, conv_w, row_block0, n_blocks, block_rows, seq_len, state=None):
    conv_dim = conv_w.shape[1]
    n_ct = conv_dim // LANE
    tail = SUBLANE if state is None else block_rows
    spec = lambda cb: pl.BlockSpec((block_rows, LANE), lambda r, c: (row_block0 + r, cb + c))
    in_specs = [spec(CB_CONV), spec(CB_CONV + n_ct), spec(CB_CONV + 2 * n_ct),
                pl.BlockSpec((CONV_WIDTH, LANE), lambda r, c: (0, c))]
    args = [proj, proj, proj, conv_w]
    if state is not None:
        st_spec = pl.BlockSpec((block_rows, LANE), lambda r, c: (r, c))
        in_specs += [st_spec, st_spec]
        args += list(state)
    return pl.pallas_call(
        functools.partial(_conv_kernel, seq_len=seq_len, has_state=state is not None),
        out_shape=(jax.ShapeDtypeStruct((n_blocks * block_rows, conv_dim), F32),
                   jax.ShapeDtypeStruct((n_blocks * tail, conv_dim), F32)),
        grid=(n_blocks, n_ct),
        in_specs=in_specs,
        out_specs=(pl.BlockSpec((block_rows, LANE), lambda r, c: (r, c)),
                   pl.BlockSpec((tail, LANE), lambda r, c: (r, c))),
        compiler_params=_cparams(("parallel", "parallel")),
    )(*args)


def _layer_norm(x, g, b):
    mu = jnp.mean(x, axis=-1, keepdims=True)
    var = jnp.mean(jnp.square(x - mu), axis=-1, keepdims=True)
    return (x - mu) * lax.rsqrt(var + LN_EPS) * g + b


def _mix_kernel(oa_ref, oc_ref, x_ref, wa_ref, wc_ref, g_ref, b_ref, wr_ref, br_ref,
                x1_ref, e_ref, gt_ref, *, dn_alpha, n_experts):
    mix = jnp.dot(oa_ref[...].astype(BF16), wa_ref[...], preferred_element_type=F32)
    mix = mix + jnp.dot(oc_ref[...].astype(BF16), wc_ref[...], preferred_element_type=F32)
    x1 = _layer_norm(dn_alpha * x_ref[...] + mix, g_ref[...], b_ref[...])
    x1_ref[...] = x1
    logits = jnp.dot(x1, wr_ref[...], preferred_element_type=F32,
                     precision=lax.Precision.HIGHEST) + br_ref[...]
    tm = logits.shape[0]
    lane = lax.broadcasted_iota(I32, (tm, LANE), 1)
    score = jnp.where(lane < n_experts, logits, -jnp.inf)
    e_out = jnp.zeros((tm, LANE), I32)
    v_out = jnp.zeros((tm, LANE), F32)
    v0 = None
    den = jnp.zeros((tm, 1), F32)
    for k in range(TOP_K):
        m = jnp.max(score, axis=-1, keepdims=True)
        idx = jnp.min(jnp.where(score == m, lane, LANE), axis=-1, keepdims=True)
        if k == 0:
            v0 = m
        ex = jnp.exp(m - v0)
        den = den + ex
        e_out = jnp.where(lane == k, idx, e_out)
        v_out = jnp.where(lane == k, ex, v_out)
        score = jnp.where(lane == idx, -jnp.inf, score)
    e_ref[...] = e_out
    gt_ref[...] = v_out / den


def _mix_ln_router(o_attn, o_conv, x, w_out_a, w_out_c, ln_g, ln_b, w_r, b_r, dn_alpha, n_experts):
    n, d = x.shape
    tm = _pick(n, 256)
    row = lambda w: pl.BlockSpec((tm, w), lambda i: (i, 0))
    full = lambda a: pl.BlockSpec(a.shape, lambda i: (0, 0))
    return pl.pallas_call(
        functools.partial(_mix_kernel, dn_alpha=dn_alpha, n_experts=n_experts),
        out_shape=(jax.ShapeDtypeStruct((n, d), F32),
                   jax.ShapeDtypeStruct((n, LANE), I32), jax.ShapeDtypeStruct((n, LANE), F32)),
        grid=(n // tm,),
        in_specs=[row(o_attn.shape[1]), row(o_conv.shape[1]), row(d), full(w_out_a), full(w_out_c),
                  full(ln_g), full(ln_b), full(w_r), full(b_r)],
        out_specs=(row(d), row(LANE), row(LANE)),
        compiler_params=_cparams(("parallel",)),
    )(o_attn, o_conv, x, w_out_a, w_out_c, ln_g, ln_b, w_r, b_r)


def _expert_changed(be_ref, m):
    return (m == 0) | (be_ref[m] != be_ref[jnp.maximum(m - 1, 0)])


def _gate_up_kernel(be_ref, br_ref, x_ref, wg_ref, wu_ref, bg_ref, bu_ref, a_ref, wgb, wub):
    m = pl.program_id(1)
    rows = br_ref[m]
    half = MOE_TM // 2

    @pl.when(_expert_changed(be_ref, m) & (rows > 0))
    def _():
        wgb[...] = wg_ref[0].astype(BF16)
        wub[...] = wu_ref[0].astype(BF16)

    def act(x):
        hg = jnp.dot(x, wgb[...], preferred_element_type=F32) + bg_ref[0]
        hu = jnp.dot(x, wub[...], preferred_element_type=F32) + bu_ref[0]
        hg = jnp.minimum(hg, SWIGLU_LIMIT)
        hu = jnp.clip(hu, -SWIGLU_LIMIT, SWIGLU_LIMIT)
        return (hg * jax.nn.sigmoid(SWIGLU_ALPHA * hg) * (hu + 1.0)).astype(BF16)

    @pl.when(rows > half)
    def _():
        a_ref[...] = act(x_ref[...].astype(BF16))

    @pl.when((rows > 0) & (rows <= half))
    def _():
        a_ref[0:half, :] = act(x_ref[0:half, :].astype(BF16))
        a_ref[half:, :] = jnp.zeros((MOE_TM - half, a_ref.shape[1]), BF16)

    @pl.when(rows == 0)
    def _():
        a_ref[...] = jnp.zeros_like(a_ref)


def _down_kernel(be_ref, br_ref, a_ref, w_ref, b_ref, y_ref, wb):
    m = pl.program_id(1)
    rows = br_ref[m]
    half = MOE_TM // 2

    @pl.when(_expert_changed(be_ref, m) & (rows > 0))
    def _():
        wb[...] = w_ref[0].astype(BF16)

    @pl.when(rows > half)
    def _():
        y_ref[...] = jnp.dot(a_ref[...], wb[...], preferred_element_type=F32) + b_ref[0]

    @pl.when((rows > 0) & (rows <= half))
    def _():
        y_ref[0:half, :] = jnp.dot(a_ref[0:half, :], wb[...], preferred_element_type=F32) + b_ref[0]
        y_ref[half:, :] = jnp.zeros((MOE_TM - half, y_ref.shape[1]), F32)

    @pl.when(rows == 0)
    def _():
        y_ref[...] = jnp.zeros_like(y_ref)


def _experts(block_e, block_rows, xr, w_gu, b_gu, w_down, b_down):
    rows, d = xr.shape
    d_ff = w_down.shape[1]
    n_blocks = rows // MOE_TM
    tf = _pick(d_ff, 1024, LANE)
    nf = d_ff // tf
    a = pl.pallas_call(
        _gate_up_kernel,
        out_shape=jax.ShapeDtypeStruct((rows, d_ff), BF16),
        grid_spec=pltpu.PrefetchScalarGridSpec(
            num_scalar_prefetch=2, grid=(nf, n_blocks),
            in_specs=[pl.BlockSpec((MOE_TM, d), lambda f, m, be, br: (m, 0)),
                      pl.BlockSpec((1, d, tf), lambda f, m, be, br: (be[m], 0, f)),
                      pl.BlockSpec((1, d, tf), lambda f, m, be, br: (be[m], 0, nf + f)),
                      pl.BlockSpec((1, 1, tf), lambda f, m, be, br: (be[m], 0, f)),
                      pl.BlockSpec((1, 1, tf), lambda f, m, be, br: (be[m], 0, nf + f))],
            out_specs=pl.BlockSpec((MOE_TM, tf), lambda f, m, be, br: (m, f)),
            scratch_shapes=[pltpu.VMEM((d, tf), BF16), pltpu.VMEM((d, tf), BF16)]),
        compiler_params=_cparams(("arbitrary", "arbitrary")),
    )(block_e, block_rows, xr, w_gu, w_gu, b_gu, b_gu)
    tn = _pick(d, 1024, LANE)
    return pl.pallas_call(
        _down_kernel,
        out_shape=jax.ShapeDtypeStruct((rows, d), F32),
        grid_spec=pltpu.PrefetchScalarGridSpec(
            num_scalar_prefetch=2, grid=(d // tn, n_blocks),
            in_specs=[pl.BlockSpec((MOE_TM, d_ff), lambda j, m, be, br: (m, 0)),
                      pl.BlockSpec((1, d_ff, tn), lambda j, m, be, br: (be[m], 0, j)),
                      pl.BlockSpec((1, 1, tn), lambda j, m, be, br: (be[m], 0, j))],
            out_specs=pl.BlockSpec((MOE_TM, tn), lambda j, m, be, br: (m, j)),
            scratch_shapes=[pltpu.VMEM((d_ff, tn), BF16)]),
        compiler_params=_cparams(("arbitrary", "arbitrary")),
    )(block_e, block_rows, a, w_down, b_down)


def _combine_kernel(x1_ref, y_ref, gt_ref, g_ref, b_ref, o_ref, *, dn_alpha):
    gt = gt_ref[...]
    moe = y_ref[0] * gt[:, 0:1]
    for k in range(1, TOP_K):
        moe = moe + y_ref[k] * gt[:, k:k + 1]
    o_ref[...] = _layer_norm(dn_alpha * x1_ref[...] + moe, g_ref[...], b_ref[...])


def _combine_ln(x1, yk, gates, ln_g, ln_b, dn_alpha):
    n, d = x1.shape
    tm = _pick(n, 256)
    return pl.pallas_call(
        functools.partial(_combine_kernel, dn_alpha=dn_alpha),
        out_shape=jax.ShapeDtypeStruct((n, d), F32),
        grid=(n // tm,),
        in_specs=[pl.BlockSpec((tm, d), lambda i: (i, 0)),
                  pl.BlockSpec((TOP_K, tm, d), lambda i: (0, i, 0)),
                  pl.BlockSpec((tm, LANE), lambda i: (i, 0)),
                  pl.BlockSpec((1, d), lambda i: (0, 0)),
                  pl.BlockSpec((1, d), lambda i: (0, 0))],
        out_specs=pl.BlockSpec((tm, d), lambda i: (i, 0)),
        compiler_params=_cparams(("parallel",)),
    )(x1, yk, gates, ln_g, ln_b)


def _route(top_e, n_experts):
    n = top_e.shape[0]
    n4 = n * TOP_K
    e_flat = top_e.reshape(-1)
    entry = jnp.arange(n4, dtype=I32)
    e_s, order = lax.sort((e_flat, entry), num_keys=1, is_stable=True)
    experts = jnp.arange(n_experts, dtype=I32)
    counts = jnp.sum((e_flat[:, None] == experts[None, :]).astype(I32), axis=0)
    starts = jnp.cumsum(counts) - counts
    padded = (counts + MOE_TM - 1) // MOE_TM * MOE_TM
    pends = jnp.cumsum(padded)
    pstarts = pends - padded
    dest_s = pstarts[e_s] + (entry - starts[e_s])
    _, pos = lax.sort((order, dest_s), num_keys=1)
    n_blocks = -(-n4 // MOE_TM) + n_experts
    block0 = jnp.arange(n_blocks, dtype=I32) * MOE_TM
    block_e = jnp.minimum(jnp.sum((pends[None, :] <= block0[:, None]).astype(I32), axis=1),
                          n_experts - 1)
    block_rows = jnp.clip(counts[block_e] - (block0 - pstarts[block_e]), 0, MOE_TM)
    row = jnp.arange(n_blocks * MOE_TM, dtype=I32)
    row_e = jnp.repeat(block_e, MOE_TM)
    within = row - pstarts[row_e]
    src = jnp.minimum(starts[row_e] + within, n4 - 1)
    row_tok = jnp.where(within < counts[row_e], order[src] // TOP_K, 0)
    return row_tok, pos.reshape(n, TOP_K), block_e, block_rows.astype(I32)


def _layer(x_prompt, x_sample, cache, state_win, state_conv, page_table,
           w_in, w_out, conv_w, cmp_k_pe, cmp_k_w1, cmp_k_w2, cmp_v_pe, cmp_v_w1, cmp_v_w2,
           ln1_g, ln1_b, w_router, b_router, w_gu, b_gu, w_down, b_down, ln2_g, ln2_b, depth):
    nb, seq_len, d = x_prompt.shape
    db, dec_seq, _ = x_sample.shape
    n_planes = N_KV_COMP * N_KV_HEADS
    n_wplanes = 2 * N_KV_HEADS
    page = cache.shape[1] // n_planes
    n_pages = page_table.shape[1]
    past_len = n_pages * page
    win_buf = state_win.shape[1] // n_wplanes
    conv_dim = conv_w.shape[1]
    n_experts = w_router.shape[1]
    n_p, n_s = nb * seq_len, db * dec_seq
    dn_alpha = (2.0 * depth) ** 0.25
    assert page == LANE and dec_seq == SUBLANE and seq_len % Q_BLOCK == 0
    assert seq_len >= WINDOW + Q_BLOCK and win_buf == WINDOW and seq_len % win_buf == 0
    assert n_p % n_s == 0 and conv_dim % LANE == 0 and n_experts <= LANE
    assert past_len + dec_seq <= 2 * 32 * SEL_BLOCK and seq_len <= 32 * SEL_BLOCK

    off_gate = ATTN_WIDTH + N_KV_COMP * KV_WIDTH + 2 * KV_WIDTH
    off_conv = off_gate + N_GATE
    w_p = jnp.concatenate([w_in[:, :off_gate], w_in[:, off_conv:], w_in[:, off_gate:off_conv],
                           jnp.zeros((d, LANE - N_GATE), w_in.dtype)], axis=1).astype(BF16)
    n_cb = w_p.shape[1] // LANE
    x_all = jnp.concatenate([x_prompt.reshape(n_p, d), x_sample.reshape(n_s, d)], axis=0)
    n_tok = n_p + n_s
    proj = _proj(x_all, w_p, _pick(n_tok, 512), _pick(w_p.shape[1], 1152, LANE))

    kv_cb = ATTN_WIDTH // (n_planes * LANE)
    tb = _pick(n_p, 256)
    kv_prompt = _interleave(proj, n_p // tb, tb, n_planes, lambda i: i, kv_cb)
    win_cb = (ATTN_WIDTH + n_planes * LANE) // (n_wplanes * LANE)
    per_seq = seq_len // win_buf
    win_prompt = _interleave(proj, nb, win_buf, n_wplanes,
                             lambda i: i * per_seq + per_seq - 1, win_cb)

    w1k = jnp.concatenate(jnp.split(cmp_k_w1, CMP_RATIO, axis=0), axis=1).astype(BF16)
    w1v = jnp.concatenate(jnp.split(cmp_v_w1, CMP_RATIO, axis=0), axis=1).astype(BF16)
    bk = _pe_bias(cmp_k_pe, cmp_k_w1.astype(BF16))
    bv = _pe_bias(cmp_v_pe, cmp_v_w1.astype(BF16))
    cmp_w = (w1k, w1v, bk, bv, cmp_k_w2.astype(BF16), cmp_v_w2.astype(BF16))
    pp = seq_len // LANE
    prompt_ids = jnp.arange(nb * pp, dtype=I32).reshape(nb, pp)
    proj_pages = proj.reshape(n_tok // LANE, LANE, n_cb * LANE)
    kc_p, vc_p = _compress(prompt_ids, proj_pages, ATTN_WIDTH // (2 * KV_WIDTH), *cmp_w)

    o_p = _prompt_attention(proj, kc_p, vc_p, nb, seq_len, n_cb)
    o_s, kv_sample, win_sample = _sample_attention(page_table, cache, proj, state_win, cmp_w,
                                                   n_p, dec_seq, n_cb)
    o_attn = jnp.concatenate([o_p, o_s], axis=0)

    y_p, u_p = _short_conv(proj, conv_w, 0, nb, seq_len, seq_len)
    z = jnp.zeros((db, dec_seq - 2, conv_dim), F32)
    s1 = jnp.concatenate([state_conv[:, 1:2], z, z[:, :1]], axis=1).reshape(n_s, conv_dim)
    s2 = jnp.concatenate([state_conv, z], axis=1).reshape(n_s, conv_dim)
    y_s, u_s = _short_conv(proj, conv_w, n_p // n_s, 1, n_s, dec_seq, state=(s1, s2))
    o_conv = jnp.concatenate([y_p, y_s], axis=0)
    conv_prompt = u_p.reshape(nb, SUBLANE, conv_dim)[:, SUBLANE - (CONV_WIDTH - 1):]
    conv_sample = u_s.reshape(db, dec_seq, conv_dim)[:, dec_seq - (CONV_WIDTH - 1):]

    w_r = jnp.pad(w_router, ((0, 0), (0, LANE - n_experts)))
    b_r = jnp.pad(b_router, (0, LANE - n_experts)).reshape(1, LANE)
    x1, e_idx, gates = _mix_ln_router(
        o_attn, o_conv, x_all, w_out[:ATTN_WIDTH].astype(BF16), w_out[ATTN_WIDTH:].astype(BF16),
        ln1_g.reshape(1, d), ln1_b.reshape(1, d), w_r, b_r, dn_alpha, n_experts)

    row_tok, pos, block_e, block_rows = _route(e_idx[:, :TOP_K], n_experts)
    y_rows = _experts(block_e, block_rows, x1[row_tok], w_gu, b_gu.reshape(n_experts, 1, -1),
                      w_down, b_down.reshape(n_experts, 1, -1))

    x2 = _combine_ln(x1, y_rows[pos.T], gates, ln2_g.reshape(1, d), ln2_b.reshape(1, d), dn_alpha)
    y_prompt = x2[:n_p].reshape(nb, seq_len, d)
    y_sample = x2[n_p:].reshape(db, dec_seq, d)
    kv_shape = (N_KV_COMP, N_KV_HEADS, HEAD_DIM)
    win_shape = (win_buf, 2, N_KV_HEADS, HEAD_DIM)
    return (y_prompt, y_sample,
            kv_prompt.reshape((nb, seq_len) + kv_shape), kv_sample.reshape((db, dec_seq) + kv_shape),
            win_prompt.reshape((nb,) + win_shape), win_sample.reshape((db,) + win_shape),
            conv_prompt, conv_sample)


def kernel(x_prompt, x_sample, cache_kv, state_win, state_conv, page_table, w_in, w_out, conv_w, cmp_k_pe, cmp_k_w1, cmp_k_w2, cmp_v_pe, cmp_v_w1, cmp_v_w2, ln1_g, ln1_b, w_router, b_router, w_gu, b_gu, w_down, b_down, ln2_g, ln2_b):
    depth, n_pool, page = cache_kv.shape[:3]
    db, win_buf = state_win.shape[1:3]
    cache_rows = cache_kv.reshape(depth * n_pool, page * N_KV_COMP * N_KV_HEADS, HEAD_DIM)
    win_rows = state_win.reshape(depth, db, win_buf * 2 * N_KV_HEADS, HEAD_DIM)
    y_prompt, y_sample = x_prompt, x_sample
    outs = [[] for _ in range(6)]
    for layer in range(depth):
        res = _layer(y_prompt, y_sample, cache_rows, win_rows[layer], state_conv[layer],
                     page_table + layer * n_pool,
                     w_in[layer], w_out[layer], conv_w[layer],
                     cmp_k_pe[layer], cmp_k_w1[layer], cmp_k_w2[layer],
                     cmp_v_pe[layer], cmp_v_w1[layer], cmp_v_w2[layer],
                     ln1_g[layer], ln1_b[layer], w_router[layer], b_router[layer],
                     w_gu[layer], b_gu[layer], w_down[layer], b_down[layer],
                     ln2_g[layer], ln2_b[layer], depth)
        y_prompt, y_sample = res[0], res[1]
        for lst, val in zip(outs, res[2:]):
            lst.append(val)
    return (y_prompt, y_sample) + tuple(jnp.stack(o) for o in outs)
```

```python
import functools
import math

import numpy as np
import jax
import jax.numpy as jnp
from jax import lax
from jax.experimental import pallas as pl
from jax.experimental.pallas import tpu as pltpu

F32 = jnp.float32
BF16 = jnp.bfloat16
I32 = jnp.int32

HEAD_DIM = 128
N_HEADS = 8
N_KV_HEADS = 2
GQA_REP = N_HEADS // N_KV_HEADS
N_KV_COMP = 4
KV_WIDTH = N_KV_HEADS * HEAD_DIM
ATTN_WIDTH = N_HEADS * HEAD_DIM
CONV_WIDTH = 3
CMP_BLOCK = 32
CMP_STRIDE = 16
CMP_RATIO = CMP_BLOCK // CMP_STRIDE
CMP_HIDDEN = 256
SEL_BLOCK = 64
SEL_TOPK = 8
WINDOW = 512
N_BRANCH = 3
Q_BLOCK = 128
TOP_K = 4
SWIGLU_LIMIT = 7.0
SWIGLU_ALPHA = 1.702
MOE_TM = 256
LN_EPS = 1e-5
NEG = -1e30
BIG = 1e30
LANE = 128
SUBLANE = 8
VMEM_LIMIT = 56 * 1024 * 1024
SEL_CHUNK = 512
MASK_M = -2.0 ** 100
AUX_HI = 64
AUX_LO = 65

CB_KV = ATTN_WIDTH // LANE
CB_WIN = CB_KV + N_KV_COMP * N_KV_HEADS
CB_CONV = CB_WIN + 2 * N_KV_HEADS
N_GATE = N_BRANCH * N_HEADS


def _alibi_slopes(n):
    def pow2(m):
        start = 2.0 ** (-8.0 / m)
        return [start ** (i + 1) for i in range(m)]
    if math.log2(n).is_integer():
        s = pow2(n)
    else:
        c = 2 ** math.floor(math.log2(n))
        s = pow2(c) + pow2(2 * c)[0::2][:n - c]
    return [float(np.float32(v)) for v in s]


SLOPES = _alibi_slopes(N_HEADS)


def _pick(n, target, mult=SUBLANE):
    for d in range(min(n, target), 0, -1):
        if n % d == 0 and d % mult == 0:
            return d
    raise ValueError((n, target, mult))


def _cparams(sem):
    return pltpu.CompilerParams(dimension_semantics=sem, vmem_limit_bytes=VMEM_LIMIT)


def _proj_kernel(x_ref, w_ref, o_ref):
    o_ref[...] = jnp.dot(x_ref[...].astype(BF16), w_ref[...], preferred_element_type=F32)


def _proj(x, w, tm, tn):
    m, k = x.shape
    n = w.shape[1]
    return pl.pallas_call(
        _proj_kernel,
        out_shape=jax.ShapeDtypeStruct((m, n), F32),
        grid=(m // tm, n // tn),
        in_specs=[pl.BlockSpec((tm, k), lambda i, j: (i, 0)),
                  pl.BlockSpec((k, tn), lambda i, j: (0, j))],
        out_specs=pl.BlockSpec((tm, tn), lambda i, j: (i, j)),
        compiler_params=_cparams(("parallel", "arbitrary")),
    )(x, w)


def _pe_bias_kernel(pe_ref, w1_ref, o_ref):
    o_ref[...] = jnp.dot(pe_ref[...].astype(BF16), w1_ref[...], preferred_element_type=F32)


def _pe_bias(pe, w1_bf):
    flat = jnp.broadcast_to(pe.reshape(1, -1), (SUBLANE, pe.size))
    return pl.pallas_call(
        _pe_bias_kernel,
        out_shape=jax.ShapeDtypeStruct((SUBLANE, CMP_HIDDEN), F32),
    )(flat, w1_bf)


def _compress_core(raw, weights, n_chunk):
    outs = []
    for kv, (w1_ref, b_ref, w2_ref) in enumerate(weights):
        acc = jnp.zeros((N_KV_HEADS * n_chunk, CMP_RATIO * CMP_HIDDEN), F32)
        for sp in range(CMP_STRIDE // 2):
            per_g = []
            for g in range(N_KV_HEADS):
                halves = [raw[kv * N_KV_HEADS + g, pl.ds(s, n_chunk, stride=CMP_STRIDE), :]
                          for s in (2 * sp, 2 * sp + 1)]
                per_g.append(jnp.concatenate(halves, axis=1))
            lhs = jnp.concatenate(per_g, axis=0).astype(BF16)
            acc = acc + jnp.dot(lhs, w1_ref[pl.ds(sp * 2 * HEAD_DIM, 2 * HEAD_DIM), :],
                                preferred_element_type=F32)
        per_kv = []
        for g in range(N_KV_HEADS):
            part0 = acc[g * n_chunk:(g + 1) * n_chunk, :CMP_HIDDEN]
            part1 = acc[g * n_chunk:(g + 1) * n_chunk, CMP_HIDDEN:]
            part1 = jnp.concatenate([part1[1:], part1[:1]], axis=0)
            pre = (b_ref[0:1, :] + part0) + part1
            per_kv.append(jnp.dot(jax.nn.gelu(pre).astype(BF16), w2_ref[...],
                                  preferred_element_type=F32))
        outs.append(per_kv)
    return outs


def _compress_kernel(ids_ref, *refs, n_pages):
    pages = refs[:n_pages]
    w1k_ref, w1v_ref, bk_ref, bv_ref, w2k_ref, w2v_ref, kc_ref, vc_ref, raw = refs[n_pages:]
    n_chunk = n_pages * (LANE // CMP_STRIDE)
    for cb in range(2 * N_KV_HEADS):
        for p in range(n_pages):
            raw[cb, p * LANE:(p + 1) * LANE, :] = pages[p][0, :, cb * HEAD_DIM:(cb + 1) * HEAD_DIM]
    kc, vc = _compress_core(raw, ((w1k_ref, bk_ref, w2k_ref), (w1v_ref, bv_ref, w2v_ref)), n_chunk)
    for g in range(N_KV_HEADS):
        kc_ref[0, g] = kc[g]
        vc_ref[0, g] = vc[g]


def _compress(page_ids, src, col_block, w1k, w1v, bk, bv, w2k, w2v):
    nb, n_pages = page_ids.shape
    n_chunk = n_pages * (LANE // CMP_STRIDE)
    width = 2 * KV_WIDTH

    def page_map(p):
        return lambda b, ids: (ids[b, p], 0, col_block)

    full = lambda shape: pl.BlockSpec(shape, lambda b, ids: (0,) * len(shape))
    in_specs = [pl.BlockSpec((1, LANE, width), page_map(p)) for p in range(n_pages)]
    in_specs += [full(w1k.shape), full(w1v.shape), full(bk.shape), full(bv.shape),
                 full(w2k.shape), full(w2v.shape)]
    out_spec = pl.BlockSpec((1, N_KV_HEADS, n_chunk, HEAD_DIM), lambda b, ids: (b, 0, 0, 0))
    out_sd = jax.ShapeDtypeStruct((nb, N_KV_HEADS, n_chunk, HEAD_DIM), F32)
    return pl.pallas_call(
        functools.partial(_compress_kernel, n_pages=n_pages),
        out_shape=(out_sd, out_sd),
        grid_spec=pltpu.PrefetchScalarGridSpec(
            num_scalar_prefetch=1, grid=(nb,), in_specs=in_specs,
            out_specs=(out_spec, out_spec),
            scratch_shapes=[pltpu.VMEM((2 * N_KV_HEADS, n_pages * LANE, HEAD_DIM), F32)]),
        compiler_params=_cparams(("parallel",)),
    )(page_ids, *([src] * n_pages), w1k, w1v, bk, bv, w2k, w2v)


def _nt_dot(a, b):
    return lax.dot_general(a, b, (((1,), (1,)), ((), ())), preferred_element_type=F32)


def _masked_softmax(s, mask):
    s = jnp.where(mask, s, NEG)
    e = jnp.where(mask, jnp.exp(s - jnp.max(s, axis=-1, keepdims=True)), 0.0)
    den = jnp.sum(e, axis=-1, keepdims=True)
    return e / jnp.where(den > 0, den, 1.0)


def _stack_heads(q, tq):
    q = q * (HEAD_DIM ** -0.5)
    return jnp.concatenate([q[:, r * HEAD_DIM:(r + 1) * HEAD_DIM] for r in range(GQA_REP)],
                           axis=0).astype(BF16)


def _row_consts(g, tq, p0):
    rows = GQA_REP * tq
    ridx = lax.broadcasted_iota(I32, (rows, 1), 0)
    slope = jnp.zeros((rows, 1), F32)
    for r in range(GQA_REP):
        slope = jnp.where(ridx // tq == r, SLOPES[g * GQA_REP + r], slope)
    t = p0 + ridx % tq
    return slope, t


def _cmp_branch(q4, kc, vc, slope, t, tq):
    n_chunk = kc.shape[0]
    s = _nt_dot(q4, kc.astype(BF16))
    c_end = lax.broadcasted_iota(I32, (1, n_chunk), 1) * CMP_STRIDE + (CMP_BLOCK - 1)
    rel = t - c_end
    s = s - slope * rel.astype(F32)
    p = _masked_softmax(s, rel >= 0)
    o_c = jnp.dot(p.astype(BF16), vc.astype(BF16), preferred_element_type=F32)
    p_sum = p[0:tq]
    for r in range(1, GQA_REP):
        p_sum = p_sum + p[r * tq:(r + 1) * tq]
    return o_c, p_sum


def _pos_aux(pos):
    pos = np.asarray(pos).reshape(-1, 1)
    lane = np.arange(LANE).reshape(1, -1)
    blk = pos // SEL_BLOCK
    aux = np.where(lane == blk, MASK_M, 0.0)
    aux = np.where(lane == AUX_HI, blk * SEL_BLOCK, aux)
    aux = np.where(lane == AUX_LO, pos % SEL_BLOCK, aux)
    assert blk.max() < AUX_HI
    return jnp.asarray(aux, F32).astype(BF16)


def _query_aux(slope, notsel4):
    lane = lax.broadcasted_iota(I32, (slope.shape[0], LANE), 1)
    base = jnp.zeros((slope.shape[0], LANE), F32) if notsel4 is None else notsel4
    return jnp.where((lane == AUX_HI) | (lane == AUX_LO), slope, base).astype(BF16)


def _not_selected(p_sum, t_row, n_sel):
    tq, n_chunk = p_sum.shape
    nj = -(-n_sel // SUBLANE) * SUBLANE
    cj = lax.broadcasted_iota(I32, (nj, n_chunk), 0) * SEL_BLOCK
    cn = lax.broadcasted_iota(I32, (nj, n_chunk), 1) * CMP_STRIDE
    overlap = jnp.maximum(jnp.minimum(cn + CMP_BLOCK, cj + SEL_BLOCK) - jnp.maximum(cn, cj), 0)
    overlap = overlap.astype(F32).astype(BF16)
    p_hi = p_sum.astype(BF16)
    p_lo = (p_sum - p_hi.astype(F32)).astype(BF16)
    imp = _nt_dot(overlap, p_hi) + _nt_dot(overlap, p_lo)
    row = lax.broadcasted_iota(I32, (nj, tq), 0)
    cur = t_row // SEL_BLOCK
    valid = (row <= cur) & (row < n_sel)
    forced = (row == 0) | (row == cur) | (row == cur - 1)
    score = jnp.where(valid, jnp.where(forced, BIG, imp), -1.0)
    notsel = jnp.ones((nj, tq), F32)
    for _ in range(min(SEL_TOPK, n_sel)):
        m = jnp.max(score, axis=0, keepdims=True)
        idx = jnp.min(jnp.where(score == m, row, nj), axis=0, keepdims=True)
        hit = row == idx
        notsel = jnp.where(hit & (m >= 0.0), 0.0, notsel)
        score = jnp.where(hit, -1.0, score)
    notsel = jnp.concatenate([notsel, jnp.ones((LANE - nj, tq), F32)], axis=0)
    if tq < LANE:
        notsel = jnp.concatenate([notsel, jnp.ones((LANE, LANE - tq), F32)], axis=1)
    return notsel.T


def _rep_rows(x):
    return jnp.concatenate([x] * GQA_REP, axis=0)


def _softmax_rows(s):
    e = jnp.exp(s - jnp.max(s, axis=-1, keepdims=True))
    return e / jnp.sum(e, axis=-1, keepdims=True)


def _gated_sum(sig, g, o_c, o_s, o_w, tq):
    outs = []
    for r in range(GQA_REP):
        h = g * GQA_REP + r
        sl = slice(r * tq, (r + 1) * tq)
        g0 = sig[:, N_BRANCH * h + 0:N_BRANCH * h + 1]
        g1 = sig[:, N_BRANCH * h + 1:N_BRANCH * h + 2]
        g2 = sig[:, N_BRANCH * h + 2:N_BRANCH * h + 3]
        outs.append(g0 * o_c[sl] + g1 * o_s[sl] + g2 * o_w[sl])
    return outs


def _prompt_attn_kernel(q_ref, gate_ref, kc_ref, vc_ref, ks_ref, vs_ref, kw_ref, vw_ref, aux_ref,
                        o_ref, ksb, vsb, kwb, vwb, *, tq, seq_len):
    g = pl.program_id(1)
    i = pl.program_id(2)
    n_sel = -(-seq_len // SEL_BLOCK)

    @pl.when(i == 0)
    def _():
        ksb[:, 0:HEAD_DIM] = ks_ref[...].astype(BF16)
        ksb[:, HEAD_DIM:] = aux_ref[...]
        kwb[:, 0:HEAD_DIM] = kw_ref[...].astype(BF16)
        kwb[:, HEAD_DIM:] = aux_ref[...]
        vsb[...] = vs_ref[...].astype(BF16)
        vwb[...] = vw_ref[...].astype(BF16)

    p0 = i * tq
    q4 = _stack_heads(q_ref[...], tq)
    rows = GQA_REP * tq
    ridx = lax.broadcasted_iota(I32, (rows, 1), 0)
    t = p0 + ridx % tq
    slope = jnp.zeros((rows, 1), F32)
    for gg in range(N_KV_HEADS):
        for r in range(GQA_REP):
            slope = jnp.where((ridx // tq == r) & (g == gg), SLOPES[gg * GQA_REP + r], slope)

    o_c, p_sum = _cmp_branch(q4, kc_ref[0, 0], vc_ref[0, 0], slope, t, tq)
    t_row = p0 + lax.broadcasted_iota(I32, (1, tq), 1)
    notsel4 = _rep_rows(_not_selected(p_sum, t_row, n_sel))
    q_sel = jnp.concatenate([q4, _query_aux(slope, notsel4)], axis=1)
    q_win = jnp.concatenate([q4, _query_aux(slope, None)], axis=1)

    def chunk(c, carry, causal):
        m, l, acc = carry
        start = pl.multiple_of(c * SEL_CHUNK, SEL_CHUNK)
        s = _nt_dot(q_sel, ksb[pl.ds(start, SEL_CHUNK), :])
        if causal:
            pos = start + lax.broadcasted_iota(I32, (1, SEL_CHUNK), 1)
            s = jnp.where(pos <= t, s, NEG)
        m_new = jnp.maximum(m, jnp.max(s, axis=-1, keepdims=True))
        a = jnp.exp(m - m_new)
        e = jnp.exp(s - m_new)
        l = a * l + jnp.sum(e, axis=-1, keepdims=True)
        acc = a * acc + jnp.dot(e.astype(BF16), vsb[pl.ds(start, SEL_CHUNK), :],
                                preferred_element_type=F32)
        return m_new, l, acc

    last = (p0 + tq - 1) // SEL_CHUNK
    init = (jnp.full((rows, 1), NEG, F32), jnp.zeros((rows, 1), F32),
            jnp.zeros((rows, HEAD_DIM), F32))
    carry = lax.fori_loop(0, last, lambda c, cr: chunk(c, cr, False), init)
    _, l, acc = chunk(last, carry, True)
    o_s = acc / l

    n_win = WINDOW + tq
    start = pl.multiple_of(jnp.maximum(p0 - WINDOW, 0), tq)
    rel = ((p0 - start) + lax.broadcasted_iota(I32, (tq, n_win), 0)
           - lax.broadcasted_iota(I32, (tq, n_win), 1))
    band = jnp.where((rel >= 0) & (rel < WINDOW), 0.0, NEG)
    s = _nt_dot(q_win, kwb[pl.ds(start, n_win), :]) + _rep_rows(band)
    o_w = jnp.dot(_softmax_rows(s).astype(BF16), vwb[pl.ds(start, n_win), :],
                  preferred_element_type=F32)

    sig = jax.nn.sigmoid(gate_ref[...])
    for gg in range(N_KV_HEADS):
        @pl.when(g == gg)
        def _(gg=gg):
            outs = _gated_sum(sig, gg, o_c, o_s, o_w, tq)
            for r in range(GQA_REP):
                o_ref[:, r * HEAD_DIM:(r + 1) * HEAD_DIM] = outs[r]


def _prompt_attention(proj, kc, vc, n_batch, seq_len, n_cols_blocks):
    tq = Q_BLOCK
    nqb = seq_len // tq
    gate_cb = n_cols_blocks - 1
    qw = GQA_REP * HEAD_DIM
    aux = _pos_aux(np.arange(seq_len))
    kv_spec = lambda cb: pl.BlockSpec((seq_len, HEAD_DIM), lambda b, g, i: (b, cb + g))
    c_spec = pl.BlockSpec((1, 1) + kc.shape[2:], lambda b, g, i: (b, g, 0, 0))
    return pl.pallas_call(
        functools.partial(_prompt_attn_kernel, tq=tq, seq_len=seq_len),
        out_shape=jax.ShapeDtypeStruct((n_batch * seq_len, ATTN_WIDTH), F32),
        grid=(n_batch, N_KV_HEADS, nqb),
        in_specs=[pl.BlockSpec((tq, qw), lambda b, g, i: (b * nqb + i, g)),
                  pl.BlockSpec((tq, LANE), lambda b, g, i: (b * nqb + i, gate_cb)),
                  c_spec, c_spec,
                  kv_spec(CB_KV + 2 * N_KV_HEADS), kv_spec(CB_KV + 3 * N_KV_HEADS),
                  kv_spec(CB_WIN), kv_spec(CB_WIN + N_KV_HEADS),
                  pl.BlockSpec(aux.shape, lambda b, g, i: (0, 0))],
        out_specs=pl.BlockSpec((tq, qw), lambda b, g, i: (b * nqb + i, g)),
        scratch_shapes=[pltpu.VMEM((seq_len, 2 * HEAD_DIM), BF16), pltpu.VMEM((seq_len, HEAD_DIM), BF16),
                        pltpu.VMEM((seq_len, 2 * HEAD_DIM), BF16), pltpu.VMEM((seq_len, HEAD_DIM), BF16)],
        compiler_params=_cparams(("parallel", "parallel", "arbitrary")),
    )(proj, proj, kc, vc, proj, proj, proj, proj, aux)


def _sample_attn_kernel(pt_ref, *refs, n_pages, tq, past_len, win_buf):
    pages = refs[:n_pages]
    (q_ref, kvn_ref, winn_ref, gate_ref, win_ref, w1k_ref, w1v_ref, bk_ref, bv_ref, w2k_ref,
     w2v_ref, saux_ref, waux_ref, o_ref, kvo_ref, wino_ref, raw) = refs[n_pages:]
    seq_len = past_len + tq
    n_sel = -(-seq_len // SEL_BLOCK)
    n_planes = N_KV_COMP * N_KV_HEADS
    n_wplanes = 2 * N_KV_HEADS
    page = pages[0].shape[1] // n_planes
    n_chunk = past_len // CMP_STRIDE
    zpad = jnp.zeros((LANE - tq, HEAD_DIM), BF16)

    for c in range(n_planes):
        kvo_ref[pl.ds(c, tq, stride=n_planes), :] = kvn_ref[:, c * HEAD_DIM:(c + 1) * HEAD_DIM]
    keep = (win_buf - tq) * n_wplanes
    wino_ref[0, 0:keep, :] = win_ref[0, tq * n_wplanes:win_buf * n_wplanes, :]
    for c in range(n_wplanes):
        wino_ref[0, pl.ds(keep + c, tq, stride=n_wplanes), :] = winn_ref[:, c * HEAD_DIM:(c + 1) * HEAD_DIM]

    for cb in range(2 * N_KV_HEADS):
        for p in range(n_pages):
            raw[cb, p * page:(p + 1) * page, :] = pages[p][0, pl.ds(cb, page, stride=n_planes), :]
    kc, vc = _compress_core(raw, ((w1k_ref, bk_ref, w2k_ref), (w1v_ref, bv_ref, w2v_ref)), n_chunk)

    sig = jax.nn.sigmoid(gate_ref[...])
    q4s, consts, o_cs, p_sums = [], [], [], []
    for g in range(N_KV_HEADS):
        q4 = _stack_heads(q_ref[:, g * GQA_REP * HEAD_DIM:(g + 1) * GQA_REP * HEAD_DIM], tq)
        slope, t = _row_consts(g, tq, past_len)
        o_c, p_sum = _cmp_branch(q4, kc[g], vc[g], slope, t, tq)
        q4s.append(q4); consts.append((slope, t)); o_cs.append(o_c); p_sums.append(p_sum)
    t_row = past_len + lax.broadcasted_iota(I32, (1, N_KV_HEADS * tq), 1) % tq
    notsel = _not_selected(jnp.concatenate(p_sums, axis=0), t_row, n_sel)

    for g in range(N_KV_HEADS):
        q4, (slope, t), o_c = q4s[g], consts[g], o_cs[g]
        q_sel = jnp.concatenate([q4, _query_aux(slope, _rep_rows(notsel[g * tq:(g + 1) * tq]))], axis=1)
        q_win = jnp.concatenate([q4, _query_aux(slope, None)], axis=1)

        kpl = 2 * N_KV_HEADS + g
        vpl = 3 * N_KV_HEADS + g
        k = jnp.concatenate(
            [p[0, pl.ds(kpl, page, stride=n_planes), :].astype(BF16) for p in pages]
            + [kvn_ref[:, kpl * HEAD_DIM:(kpl + 1) * HEAD_DIM].astype(BF16), zpad], axis=0)
        v = jnp.concatenate(
            [p[0, pl.ds(vpl, page, stride=n_planes), :].astype(BF16) for p in pages]
            + [kvn_ref[:, vpl * HEAD_DIM:(vpl + 1) * HEAD_DIM].astype(BF16), zpad], axis=0)
        pos = lax.broadcasted_iota(I32, (1, k.shape[0]), 1)
        s = _nt_dot(q_sel, jnp.concatenate([k, saux_ref[...]], axis=1))
        p = _softmax_rows(jnp.where(pos <= t, s, NEG))
        o_s = jnp.dot(p.astype(BF16), v, preferred_element_type=F32)

        kw = jnp.concatenate([win_ref[0, pl.ds(g, win_buf, stride=n_wplanes), :].astype(BF16),
                              winn_ref[:, g * HEAD_DIM:(g + 1) * HEAD_DIM].astype(BF16), zpad],
                             axis=0)
        vp = N_KV_HEADS + g
        vw = jnp.concatenate([win_ref[0, pl.ds(vp, win_buf, stride=n_wplanes), :].astype(BF16),
                              winn_ref[:, vp * HEAD_DIM:(vp + 1) * HEAD_DIM].astype(BF16), zpad],
                             axis=0)
        kpos = (past_len - win_buf) + lax.broadcasted_iota(I32, (1, kw.shape[0]), 1)
        rel = t - kpos
        s = _nt_dot(q_win, jnp.concatenate([kw, waux_ref[...]], axis=1))
        p = _softmax_rows(jnp.where((rel >= 0) & (rel < WINDOW), s, NEG))
        o_w = jnp.dot(p.astype(BF16), vw, preferred_element_type=F32)

        outs = _gated_sum(sig, g, o_c, o_s, o_w, tq)
        for r in range(GQA_REP):
            h = g * GQA_REP + r
            o_ref[:, h * HEAD_DIM:(h + 1) * HEAD_DIM] = outs[r]


def _sample_attention(page_table, cache, proj, state_win, cmp_w, n_prompt_rows, tq, n_cols_blocks):
    db, n_pages = page_table.shape
    n_planes = N_KV_COMP * N_KV_HEADS
    n_wplanes = 2 * N_KV_HEADS
    page = cache.shape[1] // n_planes
    past_len = n_pages * page
    win_buf = state_win.shape[1] // n_wplanes
    r0 = n_prompt_rows // tq
    gate_cb = n_cols_blocks - 1
    sel_aux = _pos_aux(np.arange(past_len + LANE))
    win_aux = _pos_aux(past_len - win_buf + np.arange(win_buf + LANE))

    def page_map(p):
        return lambda b, pt: (pt[b, p], 0, 0)

    full = lambda a: pl.BlockSpec(a.shape, lambda b, pt: (0,) * a.ndim)
    in_specs = [pl.BlockSpec((1,) + cache.shape[1:], page_map(p)) for p in range(n_pages)]
    in_specs += [
        pl.BlockSpec((tq, ATTN_WIDTH), lambda b, pt: (r0 + b, 0)),
        pl.BlockSpec((tq, N_KV_COMP * KV_WIDTH), lambda b, pt: (r0 + b, 1)),
        pl.BlockSpec((tq, 2 * KV_WIDTH), lambda b, pt: (r0 + b, CB_WIN * LANE // (2 * KV_WIDTH))),
        pl.BlockSpec((tq, LANE), lambda b, pt: (r0 + b, gate_cb)),
        pl.BlockSpec((1,) + state_win.shape[1:], lambda b, pt: (b, 0, 0)),
    ] + [full(w) for w in cmp_w] + [full(sel_aux), full(win_aux)]
    return pl.pallas_call(
        functools.partial(_sample_attn_kernel, n_pages=n_pages, tq=tq, past_len=past_len,
                          win_buf=win_buf),
        out_shape=(jax.ShapeDtypeStruct((db * tq, ATTN_WIDTH), F32),
                   jax.ShapeDtypeStruct((db * tq * n_planes, HEAD_DIM), F32),
                   jax.ShapeDtypeStruct(state_win.shape, F32)),
        grid_spec=pltpu.PrefetchScalarGridSpec(
            num_scalar_prefetch=1, grid=(db,), in_specs=in_specs,
            out_specs=(pl.BlockSpec((tq, ATTN_WIDTH), lambda b, pt: (b, 0)),
                       pl.BlockSpec((tq * n_planes, HEAD_DIM), lambda b, pt: (b, 0)),
                       pl.BlockSpec((1,) + state_win.shape[1:], lambda b, pt: (b, 0, 0))),
            scratch_shapes=[pltpu.VMEM((2 * N_KV_HEADS, past_len, HEAD_DIM), F32)]),
        compiler_params=_cparams(("parallel",)),
    )(page_table, *([cache] * n_pages), proj, proj, proj, proj, state_win, *cmp_w, sel_aux, win_aux)


def _interleave_kernel(x_ref, o_ref, *, n_planes):
    rows = x_ref.shape[0]
    for c in range(n_planes):
        o_ref[pl.ds(c, rows, stride=n_planes), :] = x_ref[:, c * LANE:(c + 1) * LANE]


def _interleave(proj, n_blocks, block_rows, n_planes, row_block_of, col_block):
    return pl.pallas_call(
        functools.partial(_interleave_kernel, n_planes=n_planes),
        out_shape=jax.ShapeDtypeStruct((n_blocks * block_rows * n_planes, LANE), F32),
        grid=(n_blocks,),
        in_specs=[pl.BlockSpec((block_rows, n_planes * LANE), lambda i: (row_block_of(i), col_block))],
        out_specs=pl.BlockSpec((block_rows * n_planes, LANE), lambda i: (i, 0)),
        compiler_params=_cparams(("parallel",)),
    )(proj)


def _conv_kernel(*refs, seq_len, has_state):
    if has_state:
        b_ref, c_ref, h_ref, w_ref, s1_ref, s2_ref, y_ref, u_ref = refs
    else:
        b_ref, c_ref, h_ref, w_ref, y_ref, u_ref = refs
    u = c_ref[...] * h_ref[...]
    rows = u.shape[0]
    t = lax.broadcasted_iota(I32, (rows, 1), 0) % seq_len
    p1 = jnp.where(t >= 1, pltpu.roll(u, 1, 0), 0.0)
    p2 = jnp.where(t >= 2, pltpu.roll(u, 2, 0), 0.0)
    if has_state:
        p1 = p1 + s1_ref[...]
        p2 = p2 + s2_ref[...]
    y = w_ref[0:1, :] * p2
    y = y + w_ref[1:2, :] * p1
    y = y + w_ref[2:3, :] * u
    y_ref[...] = b_ref[...] * y
    u_ref[...] = u[rows - u_ref.shape[0]:, :]


def _short_conv(proj, conv_w, row_block0, n_blocks, block_rows, seq_len, state=None):
    conv_dim = conv_w.shape[1]
    n_ct = conv_dim // LANE
    tail = SUBLANE if state is None else block_rows
    spec = lambda cb: pl.BlockSpec((block_rows, LANE), lambda r, c: (row_block0 + r, cb + c))
    in_specs = [spec(CB_CONV), spec(CB_CONV + n_ct), spec(CB_CONV + 2 * n_ct),
                pl.BlockSpec((CONV_WIDTH, LANE), lambda r, c: (0, c))]
    args = [proj, proj, proj, conv_w]
    if state is not None:
        st_spec = pl.BlockSpec((block_rows, LANE), lambda r, c: (r, c))
        in_specs += [st_spec, st_spec]
        args += list(state)
    return pl.pallas_call(
        functools.partial(_conv_kernel, seq_len=seq_len, has_state=state is not None),
        out_shape=(jax.ShapeDtypeStruct((n_blocks * block_rows, conv_dim), F32),
                   jax.ShapeDtypeStruct((n_blocks * tail, conv_dim), F32)),
        grid=(n_blocks, n_ct),
        in_specs=in_specs,
        out_specs=(pl.BlockSpec((block_rows, LANE), lambda r, c: (r, c)),
                   pl.BlockSpec((tail, LANE), lambda r, c: (r, c))),
        compiler_params=_cparams(("parallel", "parallel")),
    )(*args)


def _layer_norm(x, g, b):
    mu = jnp.mean(x, axis=-1, keepdims=True)
    var = jnp.mean(jnp.square(x - mu), axis=-1, keepdims=True)
    return (x - mu) * lax.rsqrt(var + LN_EPS) * g + b


def _mix_kernel(oa_ref, oc_ref, x_ref, wa_ref, wc_ref, g_ref, b_ref, wr_ref, br_ref,
                x1_ref, e_ref, gt_ref, *, dn_alpha, n_experts):
    mix = jnp.dot(oa_ref[...].astype(BF16), wa_ref[...], preferred_element_type=F32)
    mix = mix + jnp.dot(oc_ref[...].astype(BF16), wc_ref[...], preferred_element_type=F32)
    x1 = _layer_norm(dn_alpha * x_ref[...] + mix, g_ref[...], b_ref[...])
    x1_ref[...] = x1
    x_hi = x1.astype(BF16)
    x_lo = (x1 - x_hi.astype(F32)).astype(BF16)
    w_r = wr_ref[...]
    w_hi = w_r.astype(BF16)
    w_lo = (w_r - w_hi.astype(F32)).astype(BF16)
    logits = (jnp.dot(x_hi, w_hi, preferred_element_type=F32)
              + jnp.dot(x_lo, w_hi, preferred_element_type=F32)
              + jnp.dot(x_hi, w_lo, preferred_element_type=F32)) + br_ref[...]
    tm = logits.shape[0]
    lane = lax.broadcasted_iota(I32, (tm, LANE), 1)
    score = jnp.where(lane < n_experts, logits, -jnp.inf)
    e_out = jnp.zeros((tm, LANE), I32)
    v_out = jnp.zeros((tm, LANE), F32)
    v0 = None
    den = jnp.zeros((tm, 1), F32)
    for k in range(TOP_K):
        m = jnp.max(score, axis=-1, keepdims=True)
        idx = jnp.min(jnp.where(score == m, lane, LANE), axis=-1, keepdims=True)
        if k == 0:
            v0 = m
        ex = jnp.exp(m - v0)
        den = den + ex
        e_out = jnp.where(lane == k, idx, e_out)
        v_out = jnp.where(lane == k, ex, v_out)
        score = jnp.where(lane == idx, -jnp.inf, score)
    e_ref[...] = e_out
    gt_ref[...] = v_out / den


def _mix_ln_router(o_attn, o_conv, x, w_out_a, w_out_c, ln_g, ln_b, w_r, b_r, dn_alpha, n_experts):
    n, d = x.shape
    tm = _pick(n, 256)
    row = lambda w: pl.BlockSpec((tm, w), lambda i: (i, 0))
    full = lambda a: pl.BlockSpec(a.shape, lambda i: (0, 0))
    return pl.pallas_call(
        functools.partial(_mix_kernel, dn_alpha=dn_alpha, n_experts=n_experts),
        out_shape=(jax.ShapeDtypeStruct((n, d), F32),
                   jax.ShapeDtypeStruct((n, LANE), I32), jax.ShapeDtypeStruct((n, LANE), F32)),
        grid=(n // tm,),
        in_specs=[row(o_attn.shape[1]), row(o_conv.shape[1]), row(d), full(w_out_a), full(w_out_c),
                  full(ln_g), full(ln_b), full(w_r), full(b_r)],
        out_specs=(row(d), row(LANE), row(LANE)),
        compiler_params=_cparams(("parallel",)),
    )(o_attn, o_conv, x, w_out_a, w_out_c, ln_g, ln_b, w_r, b_r)


def _expert_changed(be_ref, m):
    return (m == 0) | (be_ref[m] != be_ref[jnp.maximum(m - 1, 0)])


def _gate_up_kernel(be_ref, br_ref, x_ref, wg_ref, wu_ref, bg_ref, bu_ref, a_ref, wgb, wub):
    m = pl.program_id(1)
    rows = br_ref[m]
    half = MOE_TM // 2

    @pl.when(_expert_changed(be_ref, m) & (rows > 0))
    def _():
        wgb[...] = wg_ref[0].astype(BF16)
        wub[...] = wu_ref[0].astype(BF16)

    def act(x):
        hg = jnp.dot(x, wgb[...], preferred_element_type=F32) + bg_ref[0]
        hu = jnp.dot(x, wub[...], preferred_element_type=F32) + bu_ref[0]
        hg = jnp.minimum(hg, SWIGLU_LIMIT)
        hu = jnp.clip(hu, -SWIGLU_LIMIT, SWIGLU_LIMIT)
        return (hg * jax.nn.sigmoid(SWIGLU_ALPHA * hg) * (hu + 1.0)).astype(BF16)

    @pl.when(rows > half)
    def _():
        a_ref[...] = act(x_ref[...].astype(BF16))

    @pl.when((rows > 0) & (rows <= half))
    def _():
        a_ref[0:half, :] = act(x_ref[0:half, :].astype(BF16))
        a_ref[half:, :] = jnp.zeros((MOE_TM - half, a_ref.shape[1]), BF16)

    @pl.when(rows == 0)
    def _():
        a_ref[...] = jnp.zeros_like(a_ref)


def _down_kernel(be_ref, br_ref, a_ref, w_ref, b_ref, y_ref, wb):
    m = pl.program_id(1)
    rows = br_ref[m]
    half = MOE_TM // 2

    @pl.when(_expert_changed(be_ref, m) & (rows > 0))
    def _():
        wb[...] = w_ref[0].astype(BF16)

    @pl.when(rows > half)
    def _():
        y_ref[...] = jnp.dot(a_ref[...], wb[...], preferred_element_type=F32) + b_ref[0]

    @pl.when((rows > 0) & (rows <= half))
    def _():
        y_ref[0:half, :] = jnp.dot(a_ref[0:half, :], wb[...], preferred_element_type=F32) + b_ref[0]
        y_ref[half:, :] = jnp.zeros((MOE_TM - half, y_ref.shape[1]), F32)

    @pl.when(rows == 0)
    def _():
        y_ref[...] = jnp.zeros_like(y_ref)


def _experts(block_e, block_rows, xr, w_gu, b_gu, w_down, b_down):
    rows, d = xr.shape
    d_ff = w_down.shape[1]
    n_blocks = rows // MOE_TM
    tf = _pick(d_ff, 1024, LANE)
    nf = d_ff // tf
    a = pl.pallas_call(
        _gate_up_kernel,
        out_shape=jax.ShapeDtypeStruct((rows, d_ff), BF16),
        grid_spec=pltpu.PrefetchScalarGridSpec(
            num_scalar_prefetch=2, grid=(nf, n_blocks),
            in_specs=[pl.BlockSpec((MOE_TM, d), lambda f, m, be, br: (m, 0)),
                      pl.BlockSpec((1, d, tf), lambda f, m, be, br: (be[m], 0, f)),
                      pl.BlockSpec((1, d, tf), lambda f, m, be, br: (be[m], 0, nf + f)),
                      pl.BlockSpec((1, 1, tf), lambda f, m, be, br: (be[m], 0, f)),
                      pl.BlockSpec((1, 1, tf), lambda f, m, be, br: (be[m], 0, nf + f))],
            out_specs=pl.BlockSpec((MOE_TM, tf), lambda f, m, be, br: (m, f)),
            scratch_shapes=[pltpu.VMEM((d, tf), BF16), pltpu.VMEM((d, tf), BF16)]),
        compiler_params=_cparams(("arbitrary", "arbitrary")),
    )(block_e, block_rows, xr, w_gu, w_gu, b_gu, b_gu)
    tn = _pick(d, 1024, LANE)
    return pl.pallas_call(
        _down_kernel,
        out_shape=jax.ShapeDtypeStruct((rows, d), F32),
        grid_spec=pltpu.PrefetchScalarGridSpec(
            num_scalar_prefetch=2, grid=(d // tn, n_blocks),
            in_specs=[pl.BlockSpec((MOE_TM, d_ff), lambda j, m, be, br: (m, 0)),
                      pl.BlockSpec((1, d_ff, tn), lambda j, m, be, br: (be[m], 0, j)),
                      pl.BlockSpec((1, 1, tn), lambda j, m, be, br: (be[m], 0, j))],
            out_specs=pl.BlockSpec((MOE_TM, tn), lambda j, m, be, br: (m, j)),
            scratch_shapes=[pltpu.VMEM((d_ff, tn), BF16)]),
        compiler_params=_cparams(("arbitrary", "arbitrary")),
    )(block_e, block_rows, a, w_down, b_down)


def _combine_kernel(x1_ref, y_ref, gt_ref, g_ref, b_ref, o_ref, *, dn_alpha):
    gt = gt_ref[...]
    moe = y_ref[0] * gt[:, 0:1]
    for k in range(1, TOP_K):
        moe = moe + y_ref[k] * gt[:, k:k + 1]
    o_ref[...] = _layer_norm(dn_alpha * x1_ref[...] + moe, g_ref[...], b_ref[...])


def _combine_ln(x1, yk, gates, ln_g, ln_b, dn_alpha):
    n, d = x1.shape
    tm = _pick(n, 256)
    return pl.pallas_call(
        functools.partial(_combine_kernel, dn_alpha=dn_alpha),
        out_shape=jax.ShapeDtypeStruct((n, d), F32),
        grid=(n // tm,),
        in_specs=[pl.BlockSpec((tm, d), lambda i: (i, 0)),
                  pl.BlockSpec((TOP_K, tm, d), lambda i: (0, i, 0)),
                  pl.BlockSpec((tm, LANE), lambda i: (i, 0)),
                  pl.BlockSpec((1, d), lambda i: (0, 0)),
                  pl.BlockSpec((1, d), lambda i: (0, 0))],
        out_specs=pl.BlockSpec((tm, d), lambda i: (i, 0)),
        compiler_params=_cparams(("parallel",)),
    )(x1, yk, gates, ln_g, ln_b)


def _route(top_e, n_experts):
    n = top_e.shape[0]
    n4 = n * TOP_K
    e_flat = top_e.reshape(-1)
    entry = jnp.arange(n4, dtype=I32)
    e_s, order = lax.sort((e_flat, entry), num_keys=1, is_stable=True)
    experts = jnp.arange(n_experts, dtype=I32)
    counts = jnp.sum((e_flat[:, None] == experts[None, :]).astype(I32), axis=0)
    starts = jnp.cumsum(counts) - counts
    padded = (counts + MOE_TM - 1) // MOE_TM * MOE_TM
    pends = jnp.cumsum(padded)
    pstarts = pends - padded
    dest_s = pstarts[e_s] + (entry - starts[e_s])
    _, pos = lax.sort((order, dest_s), num_keys=1)
    n_blocks = -(-n4 // MOE_TM) + n_experts
    block0 = jnp.arange(n_blocks, dtype=I32) * MOE_TM
    block_e = jnp.minimum(jnp.sum((pends[None, :] <= block0[:, None]).astype(I32), axis=1),
                          n_experts - 1)
    block_rows = jnp.clip(counts[block_e] - (block0 - pstarts[block_e]), 0, MOE_TM)
    row = jnp.arange(n_blocks * MOE_TM, dtype=I32)
    row_e = jnp.repeat(block_e, MOE_TM)
    within = row - pstarts[row_e]
    src = jnp.minimum(starts[row_e] + within, n4 - 1)
    row_tok = jnp.where(within < counts[row_e], order[src] // TOP_K, 0)
    return row_tok, pos.reshape(n, TOP_K), block_e, block_rows.astype(I32)


def _layer(x_prompt, x_sample, cache, state_win, state_conv, page_table,
           w_in, w_out, conv_w, cmp_k_pe, cmp_k_w1, cmp_k_w2, cmp_v_pe, cmp_v_w1, cmp_v_w2,
           ln1_g, ln1_b, w_router, b_router, w_gu, b_gu, w_down, b_down, ln2_g, ln2_b, depth):
    nb, seq_len, d = x_prompt.shape
    db, dec_seq, _ = x_sample.shape
    n_planes = N_KV_COMP * N_KV_HEADS
    n_wplanes = 2 * N_KV_HEADS
    page = cache.shape[1] // n_planes
    n_pages = page_table.shape[1]
    past_len = n_pages * page
    win_buf = state_win.shape[1] // n_wplanes
    conv_dim = conv_w.shape[1]
    n_experts = w_router.shape[1]
    n_p, n_s = nb * seq_len, db * dec_seq
    dn_alpha = (2.0 * depth) ** 0.25
    assert page == LANE and dec_seq == SUBLANE and seq_len % Q_BLOCK == 0
    assert seq_len >= WINDOW + Q_BLOCK and win_buf == WINDOW and seq_len % win_buf == 0
    assert n_p % n_s == 0 and conv_dim % LANE == 0 and n_experts <= LANE
    assert past_len + dec_seq <= 2 * 32 * SEL_BLOCK and seq_len <= 32 * SEL_BLOCK

    off_gate = ATTN_WIDTH + N_KV_COMP * KV_WIDTH + 2 * KV_WIDTH
    off_conv = off_gate + N_GATE
    w_p = jnp.concatenate([w_in[:, :off_gate], w_in[:, off_conv:], w_in[:, off_gate:off_conv],
                           jnp.zeros((d, LANE - N_GATE), w_in.dtype)], axis=1).astype(BF16)
    n_cb = w_p.shape[1] // LANE
    x_all = jnp.concatenate([x_prompt.reshape(n_p, d), x_sample.reshape(n_s, d)], axis=0)
    n_tok = n_p + n_s
    proj = _proj(x_all, w_p, _pick(n_tok, 512), _pick(w_p.shape[1], 1152, LANE))

    kv_cb = ATTN_WIDTH // (n_planes * LANE)
    tb = _pick(n_p, 256)
    kv_prompt = _interleave(proj, n_p // tb, tb, n_planes, lambda i: i, kv_cb)
    win_cb = (ATTN_WIDTH + n_planes * LANE) // (n_wplanes * LANE)
    per_seq = seq_len // win_buf
    win_prompt = _interleave(proj, nb, win_buf, n_wplanes,
                             lambda i: i * per_seq + per_seq - 1, win_cb)

    w1k = jnp.concatenate(jnp.split(cmp_k_w1, CMP_RATIO, axis=0), axis=1).astype(BF16)
    w1v = jnp.concatenate(jnp.split(cmp_v_w1, CMP_RATIO, axis=0), axis=1).astype(BF16)
    bk = _pe_bias(cmp_k_pe, cmp_k_w1.astype(BF16))
    bv = _pe_bias(cmp_v_pe, cmp_v_w1.astype(BF16))
    cmp_w = (w1k, w1v, bk, bv, cmp_k_w2.astype(BF16), cmp_v_w2.astype(BF16))
    pp = seq_len // LANE
    prompt_ids = jnp.arange(nb * pp, dtype=I32).reshape(nb, pp)
    proj_pages = proj.reshape(n_tok // LANE, LANE, n_cb * LANE)
    kc_p, vc_p = _compress(prompt_ids, proj_pages, ATTN_WIDTH // (2 * KV_WIDTH), *cmp_w)

    o_p = _prompt_attention(proj, kc_p, vc_p, nb, seq_len, n_cb)
    o_s, kv_sample, win_sample = _sample_attention(page_table, cache, proj, state_win, cmp_w,
                                                   n_p, dec_seq, n_cb)
    o_attn = jnp.concatenate([o_p, o_s], axis=0)

    y_p, u_p = _short_conv(proj, conv_w, 0, nb, seq_len, seq_len)
    z = jnp.zeros((db, dec_seq - 2, conv_dim), F32)
    s1 = jnp.concatenate([state_conv[:, 1:2], z, z[:, :1]], axis=1).reshape(n_s, conv_dim)
    s2 = jnp.concatenate([state_conv, z], axis=1).reshape(n_s, conv_dim)
    y_s, u_s = _short_conv(proj, conv_w, n_p // n_s, 1, n_s, dec_seq, state=(s1, s2))
    o_conv = jnp.concatenate([y_p, y_s], axis=0)
    conv_prompt = u_p.reshape(nb, SUBLANE, conv_dim)[:, SUBLANE - (CONV_WIDTH - 1):]
    conv_sample = u_s.reshape(db, dec_seq, conv_dim)[:, dec_seq - (CONV_WIDTH - 1):]

    w_r = jnp.pad(w_router, ((0, 0), (0, LANE - n_experts)))
    b_r = jnp.pad(b_router, (0, LANE - n_experts)).reshape(1, LANE)
    x1, e_idx, gates = _mix_ln_router(
        o_attn, o_conv, x_all, w_out[:ATTN_WIDTH].astype(BF16), w_out[ATTN_WIDTH:].astype(BF16),
        ln1_g.reshape(1, d), ln1_b.reshape(1, d), w_r, b_r, dn_alpha, n_experts)

    row_tok, pos, block_e, block_rows = _route(e_idx[:, :TOP_K], n_experts)
    y_rows = _experts(block_e, block_rows, x1[row_tok], w_gu, b_gu.reshape(n_experts, 1, -1),
                      w_down, b_down.reshape(n_experts, 1, -1))

    x2 = _combine_ln(x1, y_rows[pos.T], gates, ln2_g.reshape(1, d), ln2_b.reshape(1, d), dn_alpha)
    y_prompt = x2[:n_p].reshape(nb, seq_len, d)
    y_sample = x2[n_p:].reshape(db, dec_seq, d)
    kv_shape = (N_KV_COMP, N_KV_HEADS, HEAD_DIM)
    win_shape = (win_buf, 2, N_KV_HEADS, HEAD_DIM)
    return (y_prompt, y_sample,
            kv_prompt.reshape((nb, seq_len) + kv_shape), kv_sample.reshape((db, dec_seq) + kv_shape),
            win_prompt.reshape((nb,) + win_shape), win_sample.reshape((db,) + win_shape),
            conv_prompt, conv_sample)


def kernel(x_prompt, x_sample, cache_kv, state_win, state_conv, page_table, w_in, w_out, conv_w, cmp_k_pe, cmp_k_w1, cmp_k_w2, cmp_v_pe, cmp_v_w1, cmp_v_w2, ln1_g, ln1_b, w_router, b_router, w_gu, b_gu, w_down, b_down, ln2_g, ln2_b):
    depth, n_pool, page = cache_kv.shape[:3]
    db, win_buf = state_win.shape[1:3]
    cache_rows = cache_kv.reshape(depth * n_pool, page * N_KV_COMP * N_KV_HEADS, HEAD_DIM)
    win_rows = state_win.reshape(depth, db, win_buf * 2 * N_KV_HEADS, HEAD_DIM)
    y_prompt, y_sample = x_prompt, x_sample
    outs = [[] for _ in range(6)]
    for layer in range(depth):
        res = _layer(y_prompt, y_sample, cache_rows, win_rows[layer], state_conv[layer],
                     page_table + layer * n_pool,
                     w_in[layer], w_out[layer], conv_w[layer],
                     cmp_k_pe[layer], cmp_k_w1[layer], cmp_k_w2[layer],
                     cmp_v_pe[layer], cmp_v_w1[layer], cmp_v_w2[layer],
                     ln1_g[layer], ln1_b[layer], w_router[layer], b_router[layer],
                     w_gu[layer], b_gu[layer], w_down[layer], b_down[layer],
                     ln2_g[layer], ln2_b[layer], depth)
        y_prompt, y_sample = res[0], res[1]
        for lst, val in zip(outs, res[2:]):
            lst.append(val)
    return (y_prompt, y_sample) + tuple(jnp.stack(o) for o in outs)
```

```python
import functools
import math

import numpy as np
import jax
import jax.numpy as jnp
from jax import lax
from jax.experimental import pallas as pl
from jax.experimental.pallas import tpu as pltpu

F32 = jnp.float32
BF16 = jnp.bfloat16
I32 = jnp.int32

HEAD_DIM = 128
N_HEADS = 8
N_KV_HEADS = 2
GQA_REP = N_HEADS // N_KV_HEADS
N_KV_COMP = 4
KV_WIDTH = N_KV_HEADS * HEAD_DIM
ATTN_WIDTH = N_HEADS * HEAD_DIM
CONV_WIDTH = 3
CMP_BLOCK = 32
CMP_STRIDE = 16
CMP_RATIO = CMP_BLOCK // CMP_STRIDE
CMP_HIDDEN = 256
SEL_BLOCK = 64
SEL_TOPK = 8
WINDOW = 512
N_BRANCH = 3
Q_BLOCK = 128
TOP_K = 4
SWIGLU_LIMIT = 7.0
SWIGLU_ALPHA = 1.702
MOE_TM = 256
LN_EPS = 1e-5
NEG = -1e30
BIG = 1e30
LANE = 128
SUBLANE = 8
VMEM_LIMIT = 56 * 1024 * 1024
SEL_CHUNK = 512
MASK_M = -2.0 ** 100
AUX_HI = 64
AUX_LO = 65

CB_KV = ATTN_WIDTH // LANE
CB_WIN = CB_KV + N_KV_COMP * N_KV_HEADS
CB_CONV = CB_WIN + 2 * N_KV_HEADS
N_GATE = N_BRANCH * N_HEADS


def _alibi_slopes(n):
    def pow2(m):
        start = 2.0 ** (-8.0 / m)
        return [start ** (i + 1) for i in range(m)]
    if math.log2(n).is_integer():
        s = pow2(n)
    else:
        c = 2 ** math.floor(math.log2(n))
        s = pow2(c) + pow2(2 * c)[0::2][:n - c]
    return [float(np.float32(v)) for v in s]


SLOPES = _alibi_slopes(N_HEADS)


def _pick(n, target, mult=SUBLANE):
    for d in range(min(n, target), 0, -1):
        if n % d == 0 and d % mult == 0:
            return d
    raise ValueError((n, target, mult))


def _cparams(sem):
    return pltpu.CompilerParams(dimension_semantics=sem, vmem_limit_bytes=VMEM_LIMIT)


def _proj_kernel(x_ref, w_ref, o_ref):
    o_ref[...] = jnp.dot(x_ref[...].astype(BF16), w_ref[...], preferred_element_type=F32)


def _proj(x, w, tm, tn):
    m, k = x.shape
    n = w.shape[1]
    return pl.pallas_call(
        _proj_kernel,
        out_shape=jax.ShapeDtypeStruct((m, n), F32),
        grid=(m // tm, n // tn),
        in_specs=[pl.BlockSpec((tm, k), lambda i, j: (i, 0)),
                  pl.BlockSpec((k, tn), lambda i, j: (0, j))],
        out_specs=pl.BlockSpec((tm, tn), lambda i, j: (i, j)),
        compiler_params=_cparams(("parallel", "arbitrary")),
    )(x, w)


def _pe_bias_kernel(pe_ref, w1_ref, o_ref):
    o_ref[...] = jnp.dot(pe_ref[...].astype(BF16), w1_ref[...], preferred_element_type=F32)


def _pe_bias(pe, w1_bf):
    flat = jnp.broadcast_to(pe.reshape(1, -1), (SUBLANE, pe.size))
    return pl.pallas_call(
        _pe_bias_kernel,
        out_shape=jax.ShapeDtypeStruct((SUBLANE, CMP_HIDDEN), F32),
    )(flat, w1_bf)


def _compress_core(raw, weights, n_chunk):
    outs = []
    for kv, (w1_ref, b_ref, w2_ref) in enumerate(weights):
        acc = jnp.zeros((N_KV_HEADS * n_chunk, CMP_RATIO * CMP_HIDDEN), F32)
        for sp in range(CMP_STRIDE // 2):
            per_g = []
            for g in range(N_KV_HEADS):
                halves = [raw[kv * N_KV_HEADS + g, pl.ds(s, n_chunk, stride=CMP_STRIDE), :]
                          for s in (2 * sp, 2 * sp + 1)]
                per_g.append(jnp.concatenate(halves, axis=1))
            lhs = jnp.concatenate(per_g, axis=0).astype(BF16)
            acc = acc + jnp.dot(lhs, w1_ref[pl.ds(sp * 2 * HEAD_DIM, 2 * HEAD_DIM), :],
                                preferred_element_type=F32)
        per_kv = []
        for g in range(N_KV_HEADS):
            part0 = acc[g * n_chunk:(g + 1) * n_chunk, :CMP_HIDDEN]
            part1 = acc[g * n_chunk:(g + 1) * n_chunk, CMP_HIDDEN:]
            part1 = jnp.concatenate([part1[1:], part1[:1]], axis=0)
            pre = (b_ref[0:1, :] + part0) + part1
            per_kv.append(jnp.dot(jax.nn.gelu(pre).astype(BF16), w2_ref[...],
                                  preferred_element_type=F32))
        outs.append(per_kv)
    return outs


def _compress_kernel(ids_ref, *refs, n_pages):
    pages = refs[:n_pages]
    w1k_ref, w1v_ref, bk_ref, bv_ref, w2k_ref, w2v_ref, kc_ref, vc_ref, raw = refs[n_pages:]
    n_chunk = n_pages * (LANE // CMP_STRIDE)
    for cb in range(2 * N_KV_HEADS):
        for p in range(n_pages):
            raw[cb, p * LANE:(p + 1) * LANE, :] = pages[p][0, :, cb * HEAD_DIM:(cb + 1) * HEAD_DIM]
    kc, vc = _compress_core(raw, ((w1k_ref, bk_ref, w2k_ref), (w1v_ref, bv_ref, w2v_ref)), n_chunk)
    for g in range(N_KV_HEADS):
        kc_ref[0, g] = kc[g]
        vc_ref[0, g] = vc[g]


def _compress(page_ids, src, col_block, w1k, w1v, bk, bv, w2k, w2v):
    nb, n_pages = page_ids.shape
    n_chunk = n_pages * (LANE // CMP_STRIDE)
    width = 2 * KV_WIDTH

    def page_map(p):
        return lambda b, ids: (ids[b, p], 0, col_block)

    full = lambda shape: pl.BlockSpec(shape, lambda b, ids: (0,) * len(shape))
    in_specs = [pl.BlockSpec((1, LANE, width), page_map(p)) for p in range(n_pages)]
    in_specs += [full(w1k.shape), full(w1v.shape), full(bk.shape), full(bv.shape),
                 full(w2k.shape), full(w2v.shape)]
    out_spec = pl.BlockSpec((1, N_KV_HEADS, n_chunk, HEAD_DIM), lambda b, ids: (b, 0, 0, 0))
    out_sd = jax.ShapeDtypeStruct((nb, N_KV_HEADS, n_chunk, HEAD_DIM), F32)
    return pl.pallas_call(
        functools.partial(_compress_kernel, n_pages=n_pages),
        out_shape=(out_sd, out_sd),
        grid_spec=pltpu.PrefetchScalarGridSpec(
            num_scalar_prefetch=1, grid=(nb,), in_specs=in_specs,
            out_specs=(out_spec, out_spec),
            scratch_shapes=[pltpu.VMEM((2 * N_KV_HEADS, n_pages * LANE, HEAD_DIM), F32)]),
        compiler_params=_cparams(("parallel",)),
    )(page_ids, *([src] * n_pages), w1k, w1v, bk, bv, w2k, w2v)


def _nt_dot(a, b):
    return lax.dot_general(a, b, (((1,), (1,)), ((), ())), preferred_element_type=F32)


def _masked_softmax(s, mask):
    s = jnp.where(mask, s, NEG)
    e = jnp.where(mask, jnp.exp(s - jnp.max(s, axis=-1, keepdims=True)), 0.0)
    den = jnp.sum(e, axis=-1, keepdims=True)
    return e / jnp.where(den > 0, den, 1.0)


def _stack_heads(q, tq):
    q = q * (HEAD_DIM ** -0.5)
    return jnp.concatenate([q[:, r * HEAD_DIM:(r + 1) * HEAD_DIM] for r in range(GQA_REP)],
                           axis=0).astype(BF16)


def _row_consts(g, tq, p0):
    rows = GQA_REP * tq
    ridx = lax.broadcasted_iota(I32, (rows, 1), 0)
    slope = jnp.zeros((rows, 1), F32)
    for r in range(GQA_REP):
        slope = jnp.where(ridx // tq == r, SLOPES[g * GQA_REP + r], slope)
    t = p0 + ridx % tq
    return slope, t


def _cmp_branch(q4, kc, vc, slope, t, tq):
    n_chunk = kc.shape[0]
    s = _nt_dot(q4, kc.astype(BF16))
    c_end = lax.broadcasted_iota(I32, (1, n_chunk), 1) * CMP_STRIDE + (CMP_BLOCK - 1)
    rel = t - c_end
    s = s - slope * rel.astype(F32)
    p = _masked_softmax(s, rel >= 0)
    o_c = jnp.dot(p.astype(BF16), vc.astype(BF16), preferred_element_type=F32)
    p_sum = p[0:tq]
    for r in range(1, GQA_REP):
        p_sum = p_sum + p[r * tq:(r + 1) * tq]
    return o_c, p_sum


def _pos_aux(pos):
    pos = np.asarray(pos).reshape(-1, 1)
    lane = np.arange(LANE).reshape(1, -1)
    blk = pos // SEL_BLOCK
    aux = np.where(lane == blk, MASK_M, 0.0)
    aux = np.where(lane == AUX_HI, blk * SEL_BLOCK, aux)
    aux = np.where(lane == AUX_LO, pos % SEL_BLOCK, aux)
    assert blk.max() < AUX_HI
    return jnp.asarray(aux, F32).astype(BF16)


def _query_aux(slope, notsel4):
    lane = lax.broadcasted_iota(I32, (slope.shape[0], LANE), 1)
    base = jnp.zeros((slope.shape[0], LANE), F32) if notsel4 is None else notsel4
    return jnp.where((lane == AUX_HI) | (lane == AUX_LO), slope, base).astype(BF16)


def _not_selected(p_sum, t_row, n_sel):
    tq, n_chunk = p_sum.shape
    nj = -(-n_sel // SUBLANE) * SUBLANE
    cj = lax.broadcasted_iota(I32, (nj, n_chunk), 0) * SEL_BLOCK
    cn = lax.broadcasted_iota(I32, (nj, n_chunk), 1) * CMP_STRIDE
    overlap = jnp.maximum(jnp.minimum(cn + CMP_BLOCK, cj + SEL_BLOCK) - jnp.maximum(cn, cj), 0)
    overlap = overlap.astype(F32).astype(BF16)
    p_hi = p_sum.astype(BF16)
    p_lo = (p_sum - p_hi.astype(F32)).astype(BF16)
    imp = _nt_dot(overlap, p_hi) + _nt_dot(overlap, p_lo)
    row = lax.broadcasted_iota(I32, (nj, tq), 0)
    cur = t_row // SEL_BLOCK
    valid = (row <= cur) & (row < n_sel)
    forced = (row == 0) | (row == cur) | (row == cur - 1)
    score = jnp.where(valid, jnp.where(forced, BIG, imp), -1.0)
    notsel = jnp.ones((nj, tq), F32)
    for _ in range(min(SEL_TOPK, n_sel)):
        m = jnp.max(score, axis=0, keepdims=True)
        idx = jnp.min(jnp.where(score == m, row, nj), axis=0, keepdims=True)
        hit = row == idx
        notsel = jnp.where(hit & (m >= 0.0), 0.0, notsel)
        score = jnp.where(hit, -1.0, score)
    notsel = jnp.concatenate([notsel, jnp.ones((LANE - nj, tq), F32)], axis=0)
    if tq < LANE:
        notsel = jnp.concatenate([notsel, jnp.ones((LANE, LANE - tq), F32)], axis=1)
    return notsel.T


def _rep_rows(x):
    return jnp.concatenate([x] * GQA_REP, axis=0)


def _softmax_rows(s):
    e = jnp.exp(s - jnp.max(s, axis=-1, keepdims=True))
    return e / jnp.sum(e, axis=-1, keepdims=True)


def _gated_sum(sig, g, o_c, o_s, o_w, tq):
    outs = []
    for r in range(GQA_REP):
        h = g * GQA_REP + r
        sl = slice(r * tq, (r + 1) * tq)
        g0 = sig[:, N_BRANCH * h + 0:N_BRANCH * h + 1]
        g1 = sig[:, N_BRANCH * h + 1:N_BRANCH * h + 2]
        g2 = sig[:, N_BRANCH * h + 2:N_BRANCH * h + 3]
        outs.append(g0 * o_c[sl] + g1 * o_s[sl] + g2 * o_w[sl])
    return outs


def _prompt_attn_kernel(q_ref, gate_ref, kc_ref, vc_ref, ks_ref, vs_ref, kw_ref, vw_ref, aux_ref,
                        o_ref, ksb, vsb, kwb, vwb, *, tq, seq_len):
    g = pl.program_id(1)
    i = pl.program_id(2)
    n_sel = -(-seq_len // SEL_BLOCK)

    @pl.when(i == 0)
    def _():
        ksb[:, 0:HEAD_DIM] = ks_ref[...].astype(BF16)
        ksb[:, HEAD_DIM:] = aux_ref[...]
        kwb[:, 0:HEAD_DIM] = kw_ref[...].astype(BF16)
        kwb[:, HEAD_DIM:] = aux_ref[...]
        vsb[...] = vs_ref[...].astype(BF16)
        vwb[...] = vw_ref[...].astype(BF16)

    p0 = i * tq
    q4 = _stack_heads(q_ref[...], tq)
    rows = GQA_REP * tq
    ridx = lax.broadcasted_iota(I32, (rows, 1), 0)
    t = p0 + ridx % tq
    slope = jnp.zeros((rows, 1), F32)
    for gg in range(N_KV_HEADS):
        for r in range(GQA_REP):
            slope = jnp.where((ridx // tq == r) & (g == gg), SLOPES[gg * GQA_REP + r], slope)

    o_c, p_sum = _cmp_branch(q4, kc_ref[0, 0], vc_ref[0, 0], slope, t, tq)
    t_row = p0 + lax.broadcasted_iota(I32, (1, tq), 1)
    notsel4 = _rep_rows(_not_selected(p_sum, t_row, n_sel))
    q_sel = jnp.concatenate([q4, _query_aux(slope, notsel4)], axis=1)
    q_win = jnp.concatenate([q4, _query_aux(slope, None)], axis=1)

    def chunk(c, carry, causal):
        m, l, acc = carry
        start = pl.multiple_of(c * SEL_CHUNK, SEL_CHUNK)
        s = _nt_dot(q_sel, ksb[pl.ds(start, SEL_CHUNK), :])
        if causal:
            pos = start + lax.broadcasted_iota(I32, (1, SEL_CHUNK), 1)
            s = jnp.where(pos <= t, s, NEG)
        m_new = jnp.maximum(m, jnp.max(s, axis=-1, keepdims=True))
        a = jnp.exp(m - m_new)
        e = jnp.exp(s - m_new)
        l = a * l + jnp.sum(e, axis=-1, keepdims=True)
        acc = a * acc + jnp.dot(e.astype(BF16), vsb[pl.ds(start, SEL_CHUNK), :],
                                preferred_element_type=F32)
        return m_new, l, acc

    last = (p0 + tq - 1) // SEL_CHUNK
    init = (jnp.full((rows, 1), NEG, F32), jnp.zeros((rows, 1), F32),
            jnp.zeros((rows, HEAD_DIM), F32))
    carry = lax.fori_loop(0, last, lambda c, cr: chunk(c, cr, False), init)
    _, l, acc = chunk(last, carry, True)
    o_s = acc / l

    n_win = WINDOW + tq
    start = pl.multiple_of(jnp.maximum(p0 - WINDOW, 0), tq)
    rel = ((p0 - start) + lax.broadcasted_iota(I32, (tq, n_win), 0)
           - lax.broadcasted_iota(I32, (tq, n_win), 1))
    band = jnp.where((rel >= 0) & (rel < WINDOW), 0.0, NEG)
    s = _nt_dot(q_win, kwb[pl.ds(start, n_win), :]) + _rep_rows(band)
    o_w = jnp.dot(_softmax_rows(s).astype(BF16), vwb[pl.ds(start, n_win), :],
                  preferred_element_type=F32)

    sig = jax.nn.sigmoid(gate_ref[...])
    for gg in range(N_KV_HEADS):
        @pl.when(g == gg)
        def _(gg=gg):
            outs = _gated_sum(sig, gg, o_c, o_s, o_w, tq)
            for r in range(GQA_REP):
                o_ref[:, r * HEAD_DIM:(r + 1) * HEAD_DIM] = outs[r]


def _prompt_attention(proj, kc, vc, n_batch, seq_len, n_cols_blocks):
    tq = Q_BLOCK
    nqb = seq_len // tq
    gate_cb = n_cols_blocks - 1
    qw = GQA_REP * HEAD_DIM
    aux = _pos_aux(np.arange(seq_len))
    kv_spec = lambda cb: pl.BlockSpec((seq_len, HEAD_DIM), lambda b, g, i: (b, cb + g))
    c_spec = pl.BlockSpec((1, 1) + kc.shape[2:], lambda b, g, i: (b, g, 0, 0))
    return pl.pallas_call(
        functools.partial(_prompt_attn_kernel, tq=tq, seq_len=seq_len),
        out_shape=jax.ShapeDtypeStruct((n_batch * seq_len, ATTN_WIDTH), F32),
        grid=(n_batch, N_KV_HEADS, nqb),
        in_specs=[pl.BlockSpec((tq, qw), lambda b, g, i: (b * nqb + i, g)),
                  pl.BlockSpec((tq, LANE), lambda b, g, i: (b * nqb + i, gate_cb)),
                  c_spec, c_spec,
                  kv_spec(CB_KV + 2 * N_KV_HEADS), kv_spec(CB_KV + 3 * N_KV_HEADS),
                  kv_spec(CB_WIN), kv_spec(CB_WIN + N_KV_HEADS),
                  pl.BlockSpec(aux.shape, lambda b, g, i: (0, 0))],
        out_specs=pl.BlockSpec((tq, qw), lambda b, g, i: (b * nqb + i, g)),
        scratch_shapes=[pltpu.VMEM((seq_len, 2 * HEAD_DIM), BF16), pltpu.VMEM((seq_len, HEAD_DIM), BF16),
                        pltpu.VMEM((seq_len, 2 * HEAD_DIM), BF16), pltpu.VMEM((seq_len, HEAD_DIM), BF16)],
        compiler_params=_cparams(("parallel", "parallel", "arbitrary")),
    )(proj, proj, kc, vc, proj, proj, proj, proj, aux)


def _sample_attn_kernel(pt_ref, *refs, n_pages, tq, past_len, win_buf):
    pages = refs[:n_pages]
    (q_ref, kvn_ref, winn_ref, gate_ref, win_ref, w1k_ref, w1v_ref, bk_ref, bv_ref, w2k_ref,
     w2v_ref, saux_ref, waux_ref, o_ref, kvo_ref, wino_ref, raw) = refs[n_pages:]
    seq_len = past_len + tq
    n_sel = -(-seq_len // SEL_BLOCK)
    n_planes = N_KV_COMP * N_KV_HEADS
    n_wplanes = 2 * N_KV_HEADS
    page = pages[0].shape[1] // n_planes
    n_chunk = past_len // CMP_STRIDE
    zpad = jnp.zeros((LANE - tq, HEAD_DIM), BF16)

    for c in range(n_planes):
        kvo_ref[pl.ds(c, tq, stride=n_planes), :] = kvn_ref[:, c * HEAD_DIM:(c + 1) * HEAD_DIM]
    keep = (win_buf - tq) * n_wplanes
    wino_ref[0, 0:keep, :] = win_ref[0, tq * n_wplanes:win_buf * n_wplanes, :]
    for c in range(n_wplanes):
        wino_ref[0, pl.ds(keep + c, tq, stride=n_wplanes), :] = winn_ref[:, c * HEAD_DIM:(c + 1) * HEAD_DIM]

    for cb in range(2 * N_KV_HEADS):
        for p in range(n_pages):
            raw[cb, p * page:(p + 1) * page, :] = pages[p][0, pl.ds(cb, page, stride=n_planes), :]
    kc, vc = _compress_core(raw, ((w1k_ref, bk_ref, w2k_ref), (w1v_ref, bv_ref, w2v_ref)), n_chunk)

    sig = jax.nn.sigmoid(gate_ref[...])
    q4s, consts, o_cs, p_sums = [], [], [], []
    for g in range(N_KV_HEADS):
        q4 = _stack_heads(q_ref[:, g * GQA_REP * HEAD_DIM:(g + 1) * GQA_REP * HEAD_DIM], tq)
        slope, t = _row_consts(g, tq, past_len)
        o_c, p_sum = _cmp_branch(q4, kc[g], vc[g], slope, t, tq)
        q4s.append(q4); consts.append((slope, t)); o_cs.append(o_c); p_sums.append(p_sum)
    t_row = past_len + lax.broadcasted_iota(I32, (1, N_KV_HEADS * tq), 1) % tq
    notsel = _not_selected(jnp.concatenate(p_sums, axis=0), t_row, n_sel)

    for g in range(N_KV_HEADS):
        q4, (slope, t), o_c = q4s[g], consts[g], o_cs[g]
        q_sel = jnp.concatenate([q4, _query_aux(slope, _rep_rows(notsel[g * tq:(g + 1) * tq]))], axis=1)
        q_win = jnp.concatenate([q4, _query_aux(slope, None)], axis=1)

        kpl = 2 * N_KV_HEADS + g
        vpl = 3 * N_KV_HEADS + g
        k = jnp.concatenate(
            [p[0, pl.ds(kpl, page, stride=n_planes), :].astype(BF16) for p in pages]
            + [kvn_ref[:, kpl * HEAD_DIM:(kpl + 1) * HEAD_DIM].astype(BF16), zpad], axis=0)
        v = jnp.concatenate(
            [p[0, pl.ds(vpl, page, stride=n_planes), :].astype(BF16) for p in pages]
            + [kvn_ref[:, vpl * HEAD_DIM:(vpl + 1) * HEAD_DIM].astype(BF16), zpad], axis=0)
        pos = lax.broadcasted_iota(I32, (1, k.shape[0]), 1)
        s = _nt_dot(q_sel, jnp.concatenate([k, saux_ref[...]], axis=1))
        p = _softmax_rows(jnp.where(pos <= t, s, NEG))
        o_s = jnp.dot(p.astype(BF16), v, preferred_element_type=F32)

        kw = jnp.concatenate([win_ref[0, pl.ds(g, win_buf, stride=n_wplanes), :].astype(BF16),
                              winn_ref[:, g * HEAD_DIM:(g + 1) * HEAD_DIM].astype(BF16), zpad],
                             axis=0)
        vp = N_KV_HEADS + g
        vw = jnp.concatenate([win_ref[0, pl.ds(vp, win_buf, stride=n_wplanes), :].astype(BF16),
                              winn_ref[:, vp * HEAD_DIM:(vp + 1) * HEAD_DIM].astype(BF16), zpad],
                             axis=0)
        kpos = (past_len - win_buf) + lax.broadcasted_iota(I32, (1, kw.shape[0]), 1)
        rel = t - kpos
        s = _nt_dot(q_win, jnp.concatenate([kw, waux_ref[...]], axis=1))
        p = _softmax_rows(jnp.where((rel >= 0) & (rel < WINDOW), s, NEG))
        o_w = jnp.dot(p.astype(BF16), vw, preferred_element_type=F32)

        outs = _gated_sum(sig, g, o_c, o_s, o_w, tq)
        for r in range(GQA_REP):
            h = g * GQA_REP + r
            o_ref[:, h * HEAD_DIM:(h + 1) * HEAD_DIM] = outs[r]


def _sample_attention(page_table, cache, proj, state_win, cmp_w, n_prompt_rows, tq, n_cols_blocks):
    db, n_pages = page_table.shape
    n_planes = N_KV_COMP * N_KV_HEADS
    n_wplanes = 2 * N_KV_HEADS
    page = cache.shape[1] // n_planes
    past_len = n_pages * page
    win_buf = state_win.shape[1] // n_wplanes
    r0 = n_prompt_rows // tq
    gate_cb = n_cols_blocks - 1
    sel_aux = _pos_aux(np.arange(past_len + LANE))
    win_aux = _pos_aux(past_len - win_buf + np.arange(win_buf + LANE))

    def page_map(p):
        return lambda b, pt: (pt[b, p], 0, 0)

    full = lambda a: pl.BlockSpec(a.shape, lambda b, pt: (0,) * a.ndim)
    in_specs = [pl.BlockSpec((1,) + cache.shape[1:], page_map(p)) for p in range(n_pages)]
    in_specs += [
        pl.BlockSpec((tq, ATTN_WIDTH), lambda b, pt: (r0 + b, 0)),
        pl.BlockSpec((tq, N_KV_COMP * KV_WIDTH), lambda b, pt: (r0 + b, 1)),
        pl.BlockSpec((tq, 2 * KV_WIDTH), lambda b, pt: (r0 + b, CB_WIN * LANE // (2 * KV_WIDTH))),
        pl.BlockSpec((tq, LANE), lambda b, pt: (r0 + b, gate_cb)),
        pl.BlockSpec((1,) + state_win.shape[1:], lambda b, pt: (b, 0, 0)),
    ] + [full(w) for w in cmp_w] + [full(sel_aux), full(win_aux)]
    return pl.pallas_call(
        functools.partial(_sample_attn_kernel, n_pages=n_pages, tq=tq, past_len=past_len,
                          win_buf=win_buf),
        out_shape=(jax.ShapeDtypeStruct((db * tq, ATTN_WIDTH), F32),
                   jax.ShapeDtypeStruct((db * tq * n_planes, HEAD_DIM), F32),
                   jax.ShapeDtypeStruct(state_win.shape, F32)),
        grid_spec=pltpu.PrefetchScalarGridSpec(
            num_scalar_prefetch=1, grid=(db,), in_specs=in_specs,
            out_specs=(pl.BlockSpec((tq, ATTN_WIDTH), lambda b, pt: (b, 0)),
                       pl.BlockSpec((tq * n_planes, HEAD_DIM), lambda b, pt: (b, 0)),
                       pl.BlockSpec((1,) + state_win.shape[1:], lambda b, pt: (b, 0, 0))),
            scratch_shapes=[pltpu.VMEM((2 * N_KV_HEADS, past_len, HEAD_DIM), F32)]),
        compiler_params=_cparams(("parallel",)),
    )(page_table, *([cache] * n_pages), proj, proj, proj, proj, state_win, *cmp_w, sel_aux, win_aux)


def _interleave_kernel(x_ref, o_ref, *, n_planes):
    rows = x_ref.shape[0]
    for c in range(n_planes):
        o_ref[pl.ds(c, rows, stride=n_planes), :] = x_ref[:, c * LANE:(c + 1) * LANE]


def _interleave(proj, n_blocks, block_rows, n_planes, row_block_of, col_block):
    return pl.pallas_call(
        functools.partial(_interleave_kernel, n_planes=n_planes),
        out_shape=jax.ShapeDtypeStruct((n_blocks * block_rows * n_planes, LANE), F32),
        grid=(n_blocks,),
        in_specs=[pl.BlockSpec((block_rows, n_planes * LANE), lambda i: (row_block_of(i), col_block))],
        out_specs=pl.BlockSpec((block_rows * n_planes, LANE), lambda i: (i, 0)),
        compiler_params=_cparams(("parallel",)),
    )(proj)


def _conv_kernel(*refs, seq_len, has_state):
    if has_state:
        b_ref, c_ref, h_ref, w_ref, s1_ref, s2_ref, y_ref, u_ref = refs
    else:
        b_ref, c_ref, h_ref, w_ref, y_ref, u_ref = refs
    u = c_ref[...] * h_ref[...]
    rows = u.shape[0]
    t = lax.broadcasted_iota(I32, (rows, 1), 0) % seq_len
    p1 = jnp.where(t >= 1, pltpu.roll(u, 1, 0), 0.0)
    p2 = jnp.where(t >= 2, pltpu.roll(u, 2, 0), 0.0)
    if has_state:
        p1 = p1 + s1_ref[...]
        p2 = p2 + s2_ref[...]
    y = w_ref[0:1, :] * p2
    y = y + w_ref[1:2, :] * p1
    y = y + w_ref[2:3, :] * u
    y_ref[...] = b_ref[...] * y
    u_ref[...] = u[rows - u_ref.shape[0]:, :]


def _short_conv(proj, conv_w, row_block0, n_blocks, block_rows, seq_len, state=None):
    conv_dim = conv_w.shape[1]
    n_ct = conv_dim // LANE
    tail = SUBLANE if state is None else block_rows
    spec = lambda cb: pl.BlockSpec((block_rows, LANE), lambda r, c: (row_block0 + r, cb + c))
    in_specs = [spec(CB_CONV), spec(CB_CONV + n_ct), spec(CB_CONV + 2 * n_ct),
                pl.BlockSpec((CONV_WIDTH, LANE), lambda r, c: (0, c))]
    args = [proj, proj, proj, conv_w]
    if state is not None:
        st_spec = pl.BlockSpec((block_rows, LANE), lambda r, c: (r, c))
        in_specs += [st_spec, st_spec]
        args += list(state)
    return pl.pallas_call(
        functools.partial(_conv_kernel, seq_len=seq_len, has_state=state is not None),
        out_shape=(jax.ShapeDtypeStruct((n_blocks * block_rows, conv_dim), F32),
                   jax.ShapeDtypeStruct((n_blocks * tail, conv_dim), F32)),
        grid=(n_blocks, n_ct),
        in_specs=in_specs,
        out_specs=(pl.BlockSpec((block_rows, LANE), lambda r, c: (r, c)),
                   pl.BlockSpec((tail, LANE), lambda r, c: (r, c))),
        compiler_params=_cparams(("parallel", "parallel")),
    )(*args)


def _layer_norm(x, g, b):
    mu = jnp.mean(x, axis=-1, keepdims=True)
    var = jnp.mean(jnp.square(x - mu), axis=-1, keepdims=True)
    return (x - mu) * lax.rsqrt(var + LN_EPS) * g + b


def _mix_kernel(oa_ref, oc_ref, x_ref, wa_ref, wc_ref, g_ref, b_ref, wr_ref, br_ref,
                x1_ref, e_ref, gt_ref, *, dn_alpha, n_experts):
    mix = jnp.dot(oa_ref[...].astype(BF16), wa_ref[...], preferred_element_type=F32)
    mix = mix + jnp.dot(oc_ref[...].astype(BF16), wc_ref[...], preferred_element_type=F32)
    x1 = _layer_norm(dn_alpha * x_ref[...] + mix, g_ref[...], b_ref[...])
    x1_ref[...] = x1
    x_hi = x1.astype(BF16)
    x_lo = (x1 - x_hi.astype(F32)).astype(BF16)
    w_r = wr_ref[...]
    w_hi = w_r.astype(BF16)
    w_lo = (w_r - w_hi.astype(F32)).astype(BF16)
    logits = (jnp.dot(x_hi, w_hi, preferred_element_type=F32)
              + jnp.dot(x_lo, w_hi, preferred_element_type=F32)
              + jnp.dot(x_hi, w_lo, preferred_element_type=F32)) + br_ref[...]
    tm = logits.shape[0]
    lane = lax.broadcasted_iota(I32, (tm, LANE), 1)
    score = jnp.where(lane < n_experts, logits, -jnp.inf)
    e_out = jnp.zeros((tm, LANE), I32)
    v_out = jnp.zeros((tm, LANE), F32)
    v0 = None
    den = jnp.zeros((tm, 1), F32)
    for k in range(TOP_K):
        m = jnp.max(score, axis=-1, keepdims=True)
        idx = jnp.min(jnp.where(score == m, lane, LANE), axis=-1, keepdims=True)
        if k == 0:
            v0 = m
        ex = jnp.exp(m - v0)
        den = den + ex
        e_out = jnp.where(lane == k, idx, e_out)
        v_out = jnp.where(lane == k, ex, v_out)
        score = jnp.where(lane == idx, -jnp.inf, score)
    e_ref[...] = e_out
    gt_ref[...] = v_out / den


def _mix_ln_router(o_attn, o_conv, x, w_out_a, w_out_c, ln_g, ln_b, w_r, b_r, dn_alpha, n_experts):
    n, d = x.shape
    tm = _pick(n, 256)
    row = lambda w: pl.BlockSpec((tm, w), lambda i: (i, 0))
    full = lambda a: pl.BlockSpec(a.shape, lambda i: (0, 0))
    return pl.pallas_call(
        functools.partial(_mix_kernel, dn_alpha=dn_alpha, n_experts=n_experts),
        out_shape=(jax.ShapeDtypeStruct((n, d), F32),
                   jax.ShapeDtypeStruct((n, LANE), I32), jax.ShapeDtypeStruct((n, LANE), F32)),
        grid=(n // tm,),
        in_specs=[row(o_attn.shape[1]), row(o_conv.shape[1]), row(d), full(w_out_a), full(w_out_c),
                  full(ln_g), full(ln_b), full(w_r), full(b_r)],
        out_specs=(row(d), row(LANE), row(LANE)),
        compiler_params=_cparams(("parallel",)),
    )(o_attn, o_conv, x, w_out_a, w_out_c, ln_g, ln_b, w_r, b_r)


def _weight_pipeline(f, m, nf, chg_ref, nxt_ref, wrap_ref, cnt_ref, copies, cast):
    @pl.when((f == 0) & (m == 0))
    def _():
        cnt_ref[0] = 0
        for c in copies(nxt_ref[0], 0, 0):
            c.start()

    @pl.when(chg_ref[m] == 1)
    def _():
        slot = cnt_ref[0] % 2
        f_next = f + wrap_ref[m]

        @pl.when(f_next < nf)
        def _():
            for c in copies(nxt_ref[m + 1], f_next, 1 - slot):
                c.start()

        for c in copies(0, 0, slot):
            c.wait()
        cast(slot)
        cnt_ref[0] = cnt_ref[0] + 1


def _gate_up_kernel(be_ref, br_ref, chg_ref, nxt_ref, wrap_ref, x_ref, w_hbm, bg_ref, bu_ref, a_ref,
                    wbuf, wgb, wub, sem, cnt_ref, *, nf, tf):
    f = pl.program_id(0)
    m = pl.program_id(1)
    rows = br_ref[m]
    half = MOE_TM // 2

    def copies(e, fi, slot):
        return [pltpu.make_async_copy(
            w_hbm.at[e, :, pl.ds(pl.multiple_of((part * nf + fi) * tf, tf), tf)],
            wbuf.at[slot, part], sem.at[slot, part]) for part in range(2)]

    def cast(slot):
        wgb[...] = wbuf[slot, 0].astype(BF16)
        wub[...] = wbuf[slot, 1].astype(BF16)

    _weight_pipeline(f, m, nf, chg_ref, nxt_ref, wrap_ref, cnt_ref, copies, cast)

    def act(x):
        hg = jnp.dot(x, wgb[...], preferred_element_type=F32) + bg_ref[0]
        hu = jnp.dot(x, wub[...], preferred_element_type=F32) + bu_ref[0]
        hg = jnp.minimum(hg, SWIGLU_LIMIT)
        hu = jnp.clip(hu, -SWIGLU_LIMIT, SWIGLU_LIMIT)
        return (hg * jax.nn.sigmoid(SWIGLU_ALPHA * hg) * (hu + 1.0)).astype(BF16)

    @pl.when(rows > half)
    def _():
        a_ref[...] = act(x_ref[...].astype(BF16))

    @pl.when((rows > 0) & (rows <= half))
    def _():
        a_ref[0:half, :] = act(x_ref[0:half, :].astype(BF16))
        a_ref[half:, :] = jnp.zeros((MOE_TM - half, a_ref.shape[1]), BF16)

    @pl.when(rows == 0)
    def _():
        a_ref[...] = jnp.zeros_like(a_ref)


def _down_kernel(be_ref, br_ref, chg_ref, nxt_ref, wrap_ref, a_ref, w_hbm, b_ref, y_ref,
                 wbuf, wb, sem, cnt_ref, *, nj, tn):
    j = pl.program_id(0)
    m = pl.program_id(1)
    rows = br_ref[m]
    half = MOE_TM // 2

    def copies(e, ji, slot):
        return [pltpu.make_async_copy(
            w_hbm.at[e, :, pl.ds(pl.multiple_of(ji * tn, tn), tn)], wbuf.at[slot], sem.at[slot])]

    def cast(slot):
        wb[...] = wbuf[slot].astype(BF16)

    _weight_pipeline(j, m, nj, chg_ref, nxt_ref, wrap_ref, cnt_ref, copies, cast)

    @pl.when(rows > half)
    def _():
        y_ref[...] = jnp.dot(a_ref[...], wb[...], preferred_element_type=F32) + b_ref[0]

    @pl.when((rows > 0) & (rows <= half))
    def _():
        y_ref[0:half, :] = jnp.dot(a_ref[0:half, :], wb[...], preferred_element_type=F32) + b_ref[0]
        y_ref[half:, :] = jnp.zeros((MOE_TM - half, y_ref.shape[1]), F32)

    @pl.when(rows == 0)
    def _():
        y_ref[...] = jnp.zeros_like(y_ref)


def _experts(plan, xr, w_gu, b_gu, w_down, b_down):
    block_e, block_rows, chg, nxt, wrap = plan
    rows, d = xr.shape
    d_ff = w_down.shape[1]
    n_blocks = rows // MOE_TM
    tf = _pick(d_ff, 1024, LANE)
    nf = d_ff // tf
    a = pl.pallas_call(
        functools.partial(_gate_up_kernel, nf=nf, tf=tf),
        out_shape=jax.ShapeDtypeStruct((rows, d_ff), BF16),
        grid_spec=pltpu.PrefetchScalarGridSpec(
            num_scalar_prefetch=5, grid=(nf, n_blocks),
            in_specs=[pl.BlockSpec((MOE_TM, d), lambda f, m, be, *_: (m, 0)),
                      pl.BlockSpec(memory_space=pl.ANY),
                      pl.BlockSpec((1, 1, tf), lambda f, m, be, *_: (be[m], 0, f)),
                      pl.BlockSpec((1, 1, tf), lambda f, m, be, *_: (be[m], 0, nf + f))],
            out_specs=pl.BlockSpec((MOE_TM, tf), lambda f, m, be, *_: (m, f)),
            scratch_shapes=[pltpu.VMEM((2, 2, d, tf), F32), pltpu.VMEM((d, tf), BF16),
                            pltpu.VMEM((d, tf), BF16), pltpu.SemaphoreType.DMA((2, 2)),
                            pltpu.SMEM((1,), I32)]),
        compiler_params=_cparams(("arbitrary", "arbitrary")),
    )(block_e, block_rows, chg, nxt, wrap, xr, w_gu, b_gu, b_gu)
    tn = _pick(d, 1024, LANE)
    nj = d // tn
    return pl.pallas_call(
        functools.partial(_down_kernel, nj=nj, tn=tn),
        out_shape=jax.ShapeDtypeStruct((rows, d), F32),
        grid_spec=pltpu.PrefetchScalarGridSpec(
            num_scalar_prefetch=5, grid=(nj, n_blocks),
            in_specs=[pl.BlockSpec((MOE_TM, d_ff), lambda j, m, be, *_: (m, 0)),
                      pl.BlockSpec(memory_space=pl.ANY),
                      pl.BlockSpec((1, 1, tn), lambda j, m, be, *_: (be[m], 0, j))],
            out_specs=pl.BlockSpec((MOE_TM, tn), lambda j, m, be, *_: (m, j)),
            scratch_shapes=[pltpu.VMEM((2, d_ff, tn), F32), pltpu.VMEM((d_ff, tn), BF16),
                            pltpu.SemaphoreType.DMA((2,)), pltpu.SMEM((1,), I32)]),
        compiler_params=_cparams(("arbitrary", "arbitrary")),
    )(block_e, block_rows, chg, nxt, wrap, a, w_down, b_down)


def _combine_kernel(x1_ref, y_ref, gt_ref, g_ref, b_ref, o_ref, *, dn_alpha):
    gt = gt_ref[...]
    moe = y_ref[0] * gt[:, 0:1]
    for k in range(1, TOP_K):
        moe = moe + y_ref[k] * gt[:, k:k + 1]
    o_ref[...] = _layer_norm(dn_alpha * x1_ref[...] + moe, g_ref[...], b_ref[...])


def _combine_ln(x1, yk, gates, ln_g, ln_b, dn_alpha):
    n, d = x1.shape
    tm = _pick(n, 256)
    return pl.pallas_call(
        functools.partial(_combine_kernel, dn_alpha=dn_alpha),
        out_shape=jax.ShapeDtypeStruct((n, d), F32),
        grid=(n // tm,),
        in_specs=[pl.BlockSpec((tm, d), lambda i: (i, 0)),
                  pl.BlockSpec((TOP_K, tm, d), lambda i: (0, i, 0)),
                  pl.BlockSpec((tm, LANE), lambda i: (i, 0)),
                  pl.BlockSpec((1, d), lambda i: (0, 0)),
                  pl.BlockSpec((1, d), lambda i: (0, 0))],
        out_specs=pl.BlockSpec((tm, d), lambda i: (i, 0)),
        compiler_params=_cparams(("parallel",)),
    )(x1, yk, gates, ln_g, ln_b)


def _route(top_e, n_experts):
    n = top_e.shape[0]
    n4 = n * TOP_K
    e_flat = top_e.reshape(-1)
    entry = jnp.arange(n4, dtype=I32)
    e_s, order = lax.sort((e_flat, entry), num_keys=1, is_stable=True)
    experts = jnp.arange(n_experts, dtype=I32)
    counts = jnp.sum((e_flat[:, None] == experts[None, :]).astype(I32), axis=0)
    starts = jnp.cumsum(counts) - counts
    padded = (counts + MOE_TM - 1) // MOE_TM * MOE_TM
    pends = jnp.cumsum(padded)
    pstarts = pends - padded
    dest_s = pstarts[e_s] + (entry - starts[e_s])
    _, pos = lax.sort((order, dest_s), num_keys=1)
    n_blocks = -(-n4 // MOE_TM) + n_experts
    block0 = jnp.arange(n_blocks, dtype=I32) * MOE_TM
    block_e = jnp.minimum(jnp.sum((pends[None, :] <= block0[:, None]).astype(I32), axis=1),
                          n_experts - 1)
    block_rows = jnp.clip(counts[block_e] - (block0 - pstarts[block_e]), 0, MOE_TM)
    row = jnp.arange(n_blocks * MOE_TM, dtype=I32)
    row_e = jnp.repeat(block_e, MOE_TM)
    within = row - pstarts[row_e]
    src = jnp.minimum(starts[row_e] + within, n4 - 1)
    row_tok = jnp.where(within < counts[row_e], order[src] // TOP_K, 0)
    blk = jnp.arange(n_blocks, dtype=I32)
    prev_e = jnp.concatenate([block_e[:1] - 1, block_e[:-1]])
    chg = (block_rows > 0) & (block_e != prev_e)
    later = jnp.where(chg, blk, n_blocks)
    nxt_idx = lax.cummin(jnp.concatenate([later[1:], later[:1] * 0 + n_blocks]), reverse=True)
    wrap = nxt_idx >= n_blocks
    nxt_e = block_e[jnp.where(wrap, 0, nxt_idx)]
    nxt = jnp.concatenate([block_e[:1], nxt_e])
    plan = (block_e, block_rows.astype(I32), chg.astype(I32), nxt.astype(I32), wrap.astype(I32))
    return row_tok, pos.reshape(n, TOP_K), plan


def _layer(x_prompt, x_sample, cache, state_win, state_conv, page_table,
           w_in, w_out, conv_w, cmp_k_pe, cmp_k_w1, cmp_k_w2, cmp_v_pe, cmp_v_w1, cmp_v_w2,
           ln1_g, ln1_b, w_router, b_router, w_gu, b_gu, w_down, b_down, ln2_g, ln2_b, depth):
    nb, seq_len, d = x_prompt.shape
    db, dec_seq, _ = x_sample.shape
    n_planes = N_KV_COMP * N_KV_HEADS
    n_wplanes = 2 * N_KV_HEADS
    page = cache.shape[1] // n_planes
    n_pages = page_table.shape[1]
    past_len = n_pages * page
    win_buf = state_win.shape[1] // n_wplanes
    conv_dim = conv_w.shape[1]
    n_experts = w_router.shape[1]
    n_p, n_s = nb * seq_len, db * dec_seq
    dn_alpha = (2.0 * depth) ** 0.25
    assert page == LANE and dec_seq == SUBLANE and seq_len % Q_BLOCK == 0
    assert seq_len >= WINDOW + Q_BLOCK and win_buf == WINDOW and seq_len % win_buf == 0
    assert n_p % n_s == 0 and conv_dim % LANE == 0 and n_experts <= LANE
    assert past_len + dec_seq <= 2 * 32 * SEL_BLOCK and seq_len <= 32 * SEL_BLOCK

    off_gate = ATTN_WIDTH + N_KV_COMP * KV_WIDTH + 2 * KV_WIDTH
    off_conv = off_gate + N_GATE
    w_p = jnp.concatenate([w_in[:, :off_gate], w_in[:, off_conv:], w_in[:, off_gate:off_conv],
                           jnp.zeros((d, LANE - N_GATE), w_in.dtype)], axis=1).astype(BF16)
    n_cb = w_p.shape[1] // LANE
    x_all = jnp.concatenate([x_prompt.reshape(n_p, d), x_sample.reshape(n_s, d)], axis=0)
    n_tok = n_p + n_s
    proj = _proj(x_all, w_p, _pick(n_tok, 512), _pick(w_p.shape[1], 1152, LANE))

    kv_cb = ATTN_WIDTH // (n_planes * LANE)
    tb = _pick(n_p, 256)
    kv_prompt = _interleave(proj, n_p // tb, tb, n_planes, lambda i: i, kv_cb)
    win_cb = (ATTN_WIDTH + n_planes * LANE) // (n_wplanes * LANE)
    per_seq = seq_len // win_buf
    win_prompt = _interleave(proj, nb, win_buf, n_wplanes,
                             lambda i: i * per_seq + per_seq - 1, win_cb)

    w1k = jnp.concatenate(jnp.split(cmp_k_w1, CMP_RATIO, axis=0), axis=1).astype(BF16)
    w1v = jnp.concatenate(jnp.split(cmp_v_w1, CMP_RATIO, axis=0), axis=1).astype(BF16)
    bk = _pe_bias(cmp_k_pe, cmp_k_w1.astype(BF16))
    bv = _pe_bias(cmp_v_pe, cmp_v_w1.astype(BF16))
    cmp_w = (w1k, w1v, bk, bv, cmp_k_w2.astype(BF16), cmp_v_w2.astype(BF16))
    pp = seq_len // LANE
    prompt_ids = jnp.arange(nb * pp, dtype=I32).reshape(nb, pp)
    proj_pages = proj.reshape(n_tok // LANE, LANE, n_cb * LANE)
    kc_p, vc_p = _compress(prompt_ids, proj_pages, ATTN_WIDTH // (2 * KV_WIDTH), *cmp_w)

    o_p = _prompt_attention(proj, kc_p, vc_p, nb, seq_len, n_cb)
    o_s, kv_sample, win_sample = _sample_attention(page_table, cache, proj, state_win, cmp_w,
                                                   n_p, dec_seq, n_cb)
    o_attn = jnp.concatenate([o_p, o_s], axis=0)

    y_p, u_p = _short_conv(proj, conv_w, 0, nb, seq_len, seq_len)
    z = jnp.zeros((db, dec_seq - 2, conv_dim), F32)
    s1 = jnp.concatenate([state_conv[:, 1:2], z, z[:, :1]], axis=1).reshape(n_s, conv_dim)
    s2 = jnp.concatenate([state_conv, z], axis=1).reshape(n_s, conv_dim)
    y_s, u_s = _short_conv(proj, conv_w, n_p // n_s, 1, n_s, dec_seq, state=(s1, s2))
    o_conv = jnp.concatenate([y_p, y_s], axis=0)
    conv_prompt = u_p.reshape(nb, SUBLANE, conv_dim)[:, SUBLANE - (CONV_WIDTH - 1):]
    conv_sample = u_s.reshape(db, dec_seq, conv_dim)[:, dec_seq - (CONV_WIDTH - 1):]

    w_r = jnp.pad(w_router, ((0, 0), (0, LANE - n_experts)))
    b_r = jnp.pad(b_router, (0, LANE - n_experts)).reshape(1, LANE)
    x1, e_idx, gates = _mix_ln_router(
        o_attn, o_conv, x_all, w_out[:ATTN_WIDTH].astype(BF16), w_out[ATTN_WIDTH:].astype(BF16),
        ln1_g.reshape(1, d), ln1_b.reshape(1, d), w_r, b_r, dn_alpha, n_experts)

    row_tok, pos, plan = _route(e_idx[:, :TOP_K], n_experts)
    y_rows = _experts(plan, x1[row_tok], w_gu, b_gu.reshape(n_experts, 1, -1),
                      w_down, b_down.reshape(n_experts, 1, -1))

    x2 = _combine_ln(x1, y_rows[pos.T], gates, ln2_g.reshape(1, d), ln2_b.reshape(1, d), dn_alpha)
    y_prompt = x2[:n_p].reshape(nb, seq_len, d)
    y_sample = x2[n_p:].reshape(db, dec_seq, d)
    kv_shape = (N_KV_COMP, N_KV_HEADS, HEAD_DIM)
    win_shape = (win_buf, 2, N_KV_HEADS, HEAD_DIM)
    return (y_prompt, y_sample,
            kv_prompt.reshape((nb, seq_len) + kv_shape), kv_sample.reshape((db, dec_seq) + kv_shape),
            win_prompt.reshape((nb,) + win_shape), win_sample.reshape((db,) + win_shape),
            conv_prompt, conv_sample)


def kernel(x_prompt, x_sample, cache_kv, state_win, state_conv, page_table, w_in, w_out, conv_w, cmp_k_pe, cmp_k_w1, cmp_k_w2, cmp_v_pe, cmp_v_w1, cmp_v_w2, ln1_g, ln1_b, w_router, b_router, w_gu, b_gu, w_down, b_down, ln2_g, ln2_b):
    depth, n_pool, page = cache_kv.shape[:3]
    db, win_buf = state_win.shape[1:3]
    cache_rows = cache_kv.reshape(depth * n_pool, page * N_KV_COMP * N_KV_HEADS, HEAD_DIM)
    win_rows = state_win.reshape(depth, db, win_buf * 2 * N_KV_HEADS, HEAD_DIM)
    y_prompt, y_sample = x_prompt, x_sample
    outs = [[] for _ in range(6)]
    for layer in range(depth):
        res = _layer(y_prompt, y_sample, cache_rows, win_rows[layer], state_conv[layer],
                     page_table + layer * n_pool,
                     w_in[layer], w_out[layer], conv_w[layer],
                     cmp_k_pe[layer], cmp_k_w1[layer], cmp_k_w2[layer],
                     cmp_v_pe[layer], cmp_v_w1[layer], cmp_v_w2[layer],
                     ln1_g[layer], ln1_b[layer], w_router[layer], b_router[layer],
                     w_gu[layer], b_gu[layer], w_down[layer], b_down[layer],
                     ln2_g[layer], ln2_b[layer], depth)
        y_prompt, y_sample = res[0], res[1]
        for lst, val in zip(outs, res[2:]):
            lst.append(val)
    return (y_prompt, y_sample) + tuple(jnp.stack(o) for o in outs)
```

```python
import functools
import math

import numpy as np
import jax
import jax.numpy as jnp
from jax import lax
from jax.experimental import pallas as pl
from jax.experimental.pallas import tpu as pltpu

F32 = jnp.float32
BF16 = jnp.bfloat16
I32 = jnp.int32

HEAD_DIM = 128
N_HEADS = 8
N_KV_HEADS = 2
GQA_REP = N_HEADS // N_KV_HEADS
N_KV_COMP = 4
KV_WIDTH = N_KV_HEADS * HEAD_DIM
ATTN_WIDTH = N_HEADS * HEAD_DIM
CONV_WIDTH = 3
CMP_BLOCK = 32
CMP_STRIDE = 16
CMP_RATIO = CMP_BLOCK // CMP_STRIDE
CMP_HIDDEN = 256
SEL_BLOCK = 64
SEL_TOPK = 8
WINDOW = 512
N_BRANCH = 3
Q_BLOCK = 128
TOP_K = 4
SWIGLU_LIMIT = 7.0
SWIGLU_ALPHA = 1.702
MOE_TM = 256
LN_EPS = 1e-5
NEG = -1e30
BIG = 1e30
LANE = 128
SUBLANE = 8
VMEM_LIMIT = 56 * 1024 * 1024
SEL_CHUNK = 512
MASK_M = -2.0 ** 100
AUX_HI = 64
AUX_LO = 65

CB_KV = ATTN_WIDTH // LANE
CB_WIN = CB_KV + N_KV_COMP * N_KV_HEADS
CB_CONV = CB_WIN + 2 * N_KV_HEADS
N_GATE = N_BRANCH * N_HEADS


def _alibi_slopes(n):
    def pow2(m):
        start = 2.0 ** (-8.0 / m)
        return [start ** (i + 1) for i in range(m)]
    if math.log2(n).is_integer():
        s = pow2(n)
    else:
        c = 2 ** math.floor(math.log2(n))
        s = pow2(c) + pow2(2 * c)[0::2][:n - c]
    return [float(np.float32(v)) for v in s]


SLOPES = _alibi_slopes(N_HEADS)


def _pick(n, target, mult=SUBLANE):
    for d in range(min(n, target), 0, -1):
        if n % d == 0 and d % mult == 0:
            return d
    raise ValueError((n, target, mult))


def _cparams(sem):
    return pltpu.CompilerParams(dimension_semantics=sem, vmem_limit_bytes=VMEM_LIMIT)


def _pair_specs(tm, width, n_first_blocks):
    return [pl.BlockSpec((tm, width), lambda i, *_: (jnp.minimum(i, n_first_blocks - 1), 0)),
            pl.BlockSpec((tm, width), lambda i, *_: (jnp.maximum(i - n_first_blocks, 0), 0))]


def _pair_load(first_ref, second_ref, n_first_blocks):
    return jnp.where(pl.program_id(0) < n_first_blocks, first_ref[...], second_ref[...])


def _proj_kernel(xp_ref, xs_ref, w_ref, o_ref, *, n_first):
    x = _pair_load(xp_ref, xs_ref, n_first)
    o_ref[...] = jnp.dot(x.astype(BF16), w_ref[...], preferred_element_type=F32)


def _proj(xp, xs, w, tm, tn):
    k = xp.shape[1]
    n = w.shape[1]
    n_first = xp.shape[0] // tm
    m = xp.shape[0] + xs.shape[0]
    return pl.pallas_call(
        functools.partial(_proj_kernel, n_first=n_first),
        out_shape=jax.ShapeDtypeStruct((m, n), F32),
        grid=(m // tm, n // tn),
        in_specs=_pair_specs(tm, k, n_first) + [pl.BlockSpec((k, tn), lambda i, j: (0, j))],
        out_specs=pl.BlockSpec((tm, tn), lambda i, j: (i, j)),
        compiler_params=_cparams(("parallel", "arbitrary")),
    )(xp, xs, w)


def _pe_bias_kernel(pe_ref, w1_ref, o_ref):
    o_ref[...] = jnp.dot(pe_ref[...].astype(BF16), w1_ref[...], preferred_element_type=F32)


def _pe_bias(pe, w1_bf):
    flat = jnp.broadcast_to(pe.reshape(1, -1), (SUBLANE, pe.size))
    return pl.pallas_call(
        _pe_bias_kernel,
        out_shape=jax.ShapeDtypeStruct((SUBLANE, CMP_HIDDEN), F32),
    )(flat, w1_bf)


S_PITCH = 136


def _scatter_rows(raw, plane, base, x):
    per = SUBLANE
    for j in range(x.shape[0] // per):
        c, s0 = divmod(j * per, CMP_STRIDE)
        raw[plane, pl.ds(s0 * S_PITCH + base + c, per, stride=S_PITCH), :] = x[j * per:(j + 1) * per]


def _compress_core(raw, weights, n_chunk):
    assert n_chunk <= S_PITCH
    outs = []
    for kv, (w1_ref, b_ref, w2_ref) in enumerate(weights):
        acc = jnp.zeros((N_KV_HEADS * n_chunk, CMP_RATIO * CMP_HIDDEN), F32)
        for sp in range(CMP_STRIDE // 2):
            per_g = []
            for g in range(N_KV_HEADS):
                halves = [raw[kv * N_KV_HEADS + g, pl.ds(s * S_PITCH, n_chunk), :]
                          for s in (2 * sp, 2 * sp + 1)]
                per_g.append(jnp.concatenate(halves, axis=1))
            lhs = jnp.concatenate(per_g, axis=0).astype(BF16)
            acc = acc + jnp.dot(lhs, w1_ref[pl.ds(sp * 2 * HEAD_DIM, 2 * HEAD_DIM), :],
                                preferred_element_type=F32)
        per_kv = []
        for g in range(N_KV_HEADS):
            part0 = acc[g * n_chunk:(g + 1) * n_chunk, :CMP_HIDDEN]
            part1 = acc[g * n_chunk:(g + 1) * n_chunk, CMP_HIDDEN:]
            part1 = jnp.concatenate([part1[1:], part1[:1]], axis=0)
            pre = (b_ref[0:1, :] + part0) + part1
            per_kv.append(jnp.dot(jax.nn.gelu(pre).astype(BF16), w2_ref[...],
                                  preferred_element_type=F32))
        outs.append(per_kv)
    return outs


def _compress_kernel(ids_ref, *refs, n_pages):
    pages = refs[:n_pages]
    w1k_ref, w1v_ref, bk_ref, bv_ref, w2k_ref, w2v_ref, kc_ref, vc_ref, raw = refs[n_pages:]
    n_chunk = n_pages * (LANE // CMP_STRIDE)
    for cb in range(2 * N_KV_HEADS):
        for p in range(n_pages):
            _scatter_rows(raw, cb, p * (LANE // CMP_STRIDE),
                          pages[p][0, :, cb * HEAD_DIM:(cb + 1) * HEAD_DIM])
    kc, vc = _compress_core(raw, ((w1k_ref, bk_ref, w2k_ref), (w1v_ref, bv_ref, w2v_ref)), n_chunk)
    for g in range(N_KV_HEADS):
        kc_ref[0, g] = kc[g]
        vc_ref[0, g] = vc[g]


def _compress(page_ids, src, col_block, w1k, w1v, bk, bv, w2k, w2v):
    nb, n_pages = page_ids.shape
    n_chunk = n_pages * (LANE // CMP_STRIDE)
    width = 2 * KV_WIDTH

    def page_map(p):
        return lambda b, ids: (ids[b, p], 0, col_block)

    full = lambda shape: pl.BlockSpec(shape, lambda b, ids: (0,) * len(shape))
    in_specs = [pl.BlockSpec((1, LANE, width), page_map(p)) for p in range(n_pages)]
    in_specs += [full(w1k.shape), full(w1v.shape), full(bk.shape), full(bv.shape),
                 full(w2k.shape), full(w2v.shape)]
    out_spec = pl.BlockSpec((1, N_KV_HEADS, n_chunk, HEAD_DIM), lambda b, ids: (b, 0, 0, 0))
    out_sd = jax.ShapeDtypeStruct((nb, N_KV_HEADS, n_chunk, HEAD_DIM), F32)
    return pl.pallas_call(
        functools.partial(_compress_kernel, n_pages=n_pages),
        out_shape=(out_sd, out_sd),
        grid_spec=pltpu.PrefetchScalarGridSpec(
            num_scalar_prefetch=1, grid=(nb,), in_specs=in_specs,
            out_specs=(out_spec, out_spec),
            scratch_shapes=[pltpu.VMEM((2 * N_KV_HEADS, CMP_STRIDE * S_PITCH, HEAD_DIM), F32)]),
        compiler_params=_cparams(("parallel",)),
    )(page_ids, *([src] * n_pages), w1k, w1v, bk, bv, w2k, w2v)


def _nt_dot(a, b):
    return lax.dot_general(a, b, (((1,), (1,)), ((), ())), preferred_element_type=F32)


def _masked_softmax(s, mask):
    s = jnp.where(mask, s, NEG)
    e = jnp.where(mask, jnp.exp(s - jnp.max(s, axis=-1, keepdims=True)), 0.0)
    den = jnp.sum(e, axis=-1, keepdims=True)
    return e / jnp.where(den > 0, den, 1.0)


def _stack_heads(q, tq):
    q = q * (HEAD_DIM ** -0.5)
    return jnp.concatenate([q[:, r * HEAD_DIM:(r + 1) * HEAD_DIM] for r in range(GQA_REP)],
                           axis=0).astype(BF16)


def _row_consts(g, tq, p0):
    rows = GQA_REP * tq
    ridx = lax.broadcasted_iota(I32, (rows, 1), 0)
    slope = jnp.zeros((rows, 1), F32)
    for r in range(GQA_REP):
        slope = jnp.where(ridx // tq == r, SLOPES[g * GQA_REP + r], slope)
    t = p0 + ridx % tq
    return slope, t


def _cmp_branch(q4, kc, vc, slope, t, tq):
    n_chunk = kc.shape[0]
    s = _nt_dot(q4, kc.astype(BF16))
    c_end = lax.broadcasted_iota(I32, (1, n_chunk), 1) * CMP_STRIDE + (CMP_BLOCK - 1)
    rel = t - c_end
    s = s - slope * rel.astype(F32)
    p = _masked_softmax(s, rel >= 0)
    o_c = jnp.dot(p.astype(BF16), vc.astype(BF16), preferred_element_type=F32)
    p_sum = p[0:tq]
    for r in range(1, GQA_REP):
        p_sum = p_sum + p[r * tq:(r + 1) * tq]
    return o_c, p_sum


def _pos_aux(pos):
    pos = np.asarray(pos).reshape(-1, 1)
    lane = np.arange(LANE).reshape(1, -1)
    blk = pos // SEL_BLOCK
    aux = np.where(lane == blk, MASK_M, 0.0)
    aux = np.where(lane == AUX_HI, blk * SEL_BLOCK, aux)
    aux = np.where(lane == AUX_LO, pos % SEL_BLOCK, aux)
    assert blk.max() < AUX_HI
    return jnp.asarray(aux, F32).astype(BF16)


def _query_aux(slope, notsel4):
    lane = lax.broadcasted_iota(I32, (slope.shape[0], LANE), 1)
    base = jnp.zeros((slope.shape[0], LANE), F32) if notsel4 is None else notsel4
    return jnp.where((lane == AUX_HI) | (lane == AUX_LO), slope, base).astype(BF16)


def _not_selected(p_sum, t_row, n_sel):
    tq, n_chunk = p_sum.shape
    nj = -(-n_sel // SUBLANE) * SUBLANE
    cj = lax.broadcasted_iota(I32, (nj, n_chunk), 0) * SEL_BLOCK
    cn = lax.broadcasted_iota(I32, (nj, n_chunk), 1) * CMP_STRIDE
    overlap = jnp.maximum(jnp.minimum(cn + CMP_BLOCK, cj + SEL_BLOCK) - jnp.maximum(cn, cj), 0)
    overlap = overlap.astype(F32).astype(BF16)
    p_hi = p_sum.astype(BF16)
    p_lo = (p_sum - p_hi.astype(F32)).astype(BF16)
    imp = _nt_dot(overlap, p_hi) + _nt_dot(overlap, p_lo)
    row = lax.broadcasted_iota(I32, (nj, tq), 0)
    cur = t_row // SEL_BLOCK
    valid = (row <= cur) & (row < n_sel)
    forced = (row == 0) | (row == cur) | (row == cur - 1)
    score = jnp.where(valid, jnp.where(forced, BIG, imp), -1.0)
    notsel = jnp.ones((nj, tq), F32)
    for _ in range(min(SEL_TOPK, n_sel)):
        m = jnp.max(score, axis=0, keepdims=True)
        idx = jnp.min(jnp.where(score == m, row, nj), axis=0, keepdims=True)
        hit = row == idx
        notsel = jnp.where(hit & (m >= 0.0), 0.0, notsel)
        score = jnp.where(hit, -1.0, score)
    notsel = jnp.concatenate([notsel, jnp.ones((LANE - nj, tq), F32)], axis=0)
    if tq < LANE:
        notsel = jnp.concatenate([notsel, jnp.ones((LANE, LANE - tq), F32)], axis=1)
    return notsel.T


def _rep_rows(x):
    return jnp.concatenate([x] * GQA_REP, axis=0)


def _softmax_rows(s):
    e = jnp.exp(s - jnp.max(s, axis=-1, keepdims=True))
    return e / jnp.sum(e, axis=-1, keepdims=True)


def _gated_sum(sig, g, o_c, o_s, o_w, tq):
    outs = []
    for r in range(GQA_REP):
        h = g * GQA_REP + r
        sl = slice(r * tq, (r + 1) * tq)
        g0 = sig[:, N_BRANCH * h + 0:N_BRANCH * h + 1]
        g1 = sig[:, N_BRANCH * h + 1:N_BRANCH * h + 2]
        g2 = sig[:, N_BRANCH * h + 2:N_BRANCH * h + 3]
        outs.append(g0 * o_c[sl] + g1 * o_s[sl] + g2 * o_w[sl])
    return outs


def _prompt_attn_kernel(q_ref, gate_ref, kc_ref, vc_ref, ks_ref, vs_ref, kw_ref, vw_ref, aux_ref,
                        o_ref, ksb, vsb, kwb, vwb, *, tq, seq_len):
    g = pl.program_id(1)
    i = pl.program_id(2)
    n_sel = -(-seq_len // SEL_BLOCK)

    @pl.when(i == 0)
    def _():
        ksb[:, 0:HEAD_DIM] = ks_ref[...].astype(BF16)
        ksb[:, HEAD_DIM:] = aux_ref[...]
        kwb[:, 0:HEAD_DIM] = kw_ref[...].astype(BF16)
        kwb[:, HEAD_DIM:] = aux_ref[...]
        vsb[...] = vs_ref[...].astype(BF16)
        vwb[...] = vw_ref[...].astype(BF16)

    p0 = i * tq
    q4 = _stack_heads(q_ref[...], tq)
    rows = GQA_REP * tq
    ridx = lax.broadcasted_iota(I32, (rows, 1), 0)
    t = p0 + ridx % tq
    slope = jnp.zeros((rows, 1), F32)
    for gg in range(N_KV_HEADS):
        for r in range(GQA_REP):
            slope = jnp.where((ridx // tq == r) & (g == gg), SLOPES[gg * GQA_REP + r], slope)

    o_c, p_sum = _cmp_branch(q4, kc_ref[0, 0], vc_ref[0, 0], slope, t, tq)
    t_row = p0 + lax.broadcasted_iota(I32, (1, tq), 1)
    notsel4 = _rep_rows(_not_selected(p_sum, t_row, n_sel))
    q_sel = jnp.concatenate([q4, _query_aux(slope, notsel4)], axis=1)
    q_win = jnp.concatenate([q4, _query_aux(slope, None)], axis=1)

    def chunk(c, carry, causal):
        m, l, acc = carry
        start = pl.multiple_of(c * SEL_CHUNK, SEL_CHUNK)
        s = _nt_dot(q_sel, ksb[pl.ds(start, SEL_CHUNK), :])
        if causal:
            pos = start + lax.broadcasted_iota(I32, (1, SEL_CHUNK), 1)
            s = jnp.where(pos <= t, s, NEG)
        m_new = jnp.maximum(m, jnp.max(s, axis=-1, keepdims=True))
        a = jnp.exp(m - m_new)
        e = jnp.exp(s - m_new)
        l = a * l + jnp.sum(e, axis=-1, keepdims=True)
        acc = a * acc + jnp.dot(e.astype(BF16), vsb[pl.ds(start, SEL_CHUNK), :],
                                preferred_element_type=F32)
        return m_new, l, acc

    last = (p0 + tq - 1) // SEL_CHUNK
    init = (jnp.full((rows, 1), NEG, F32), jnp.zeros((rows, 1), F32),
            jnp.zeros((rows, HEAD_DIM), F32))
    carry = lax.fori_loop(0, last, lambda c, cr: chunk(c, cr, False), init)
    _, l, acc = chunk(last, carry, True)
    o_s = acc / l

    n_win = WINDOW + tq
    start = pl.multiple_of(jnp.maximum(p0 - WINDOW, 0), tq)
    rel = ((p0 - start) + lax.broadcasted_iota(I32, (tq, n_win), 0)
           - lax.broadcasted_iota(I32, (tq, n_win), 1))
    band = jnp.where((rel >= 0) & (rel < WINDOW), 0.0, NEG)
    s = _nt_dot(q_win, kwb[pl.ds(start, n_win), :]) + _rep_rows(band)
    o_w = jnp.dot(_softmax_rows(s).astype(BF16), vwb[pl.ds(start, n_win), :],
                  preferred_element_type=F32)

    sig = jax.nn.sigmoid(gate_ref[...])
    for gg in range(N_KV_HEADS):
        @pl.when(g == gg)
        def _(gg=gg):
            outs = _gated_sum(sig, gg, o_c, o_s, o_w, tq)
            for r in range(GQA_REP):
                o_ref[:, r * HEAD_DIM:(r + 1) * HEAD_DIM] = outs[r]


def _prompt_attention(proj, kc, vc, n_batch, seq_len, n_cols_blocks):
    tq = Q_BLOCK
    nqb = seq_len // tq
    gate_cb = n_cols_blocks - 1
    qw = GQA_REP * HEAD_DIM
    aux = _pos_aux(np.arange(seq_len))
    kv_spec = lambda cb: pl.BlockSpec((seq_len, HEAD_DIM), lambda b, g, i: (b, cb + g))
    c_spec = pl.BlockSpec((1, 1) + kc.shape[2:], lambda b, g, i: (b, g, 0, 0))
    return pl.pallas_call(
        functools.partial(_prompt_attn_kernel, tq=tq, seq_len=seq_len),
        out_shape=jax.ShapeDtypeStruct((n_batch * seq_len, ATTN_WIDTH), F32),
        grid=(n_batch, N_KV_HEADS, nqb),
        in_specs=[pl.BlockSpec((tq, qw), lambda b, g, i: (b * nqb + i, g)),
                  pl.BlockSpec((tq, LANE), lambda b, g, i: (b * nqb + i, gate_cb)),
                  c_spec, c_spec,
                  kv_spec(CB_KV + 2 * N_KV_HEADS), kv_spec(CB_KV + 3 * N_KV_HEADS),
                  kv_spec(CB_WIN), kv_spec(CB_WIN + N_KV_HEADS),
                  pl.BlockSpec(aux.shape, lambda b, g, i: (0, 0))],
        out_specs=pl.BlockSpec((tq, qw), lambda b, g, i: (b * nqb + i, g)),
        scratch_shapes=[pltpu.VMEM((seq_len, 2 * HEAD_DIM), BF16), pltpu.VMEM((seq_len, HEAD_DIM), BF16),
                        pltpu.VMEM((seq_len, 2 * HEAD_DIM), BF16), pltpu.VMEM((seq_len, HEAD_DIM), BF16)],
        compiler_params=_cparams(("parallel", "parallel", "arbitrary")),
    )(proj, proj, kc, vc, proj, proj, proj, proj, aux)


def _sample_attn_kernel(pt_ref, *refs, n_pages, tq, past_len, win_buf):
    pages = refs[:n_pages]
    (q_ref, kvn_ref, winn_ref, gate_ref, win_ref, w1k_ref, w1v_ref, bk_ref, bv_ref, w2k_ref,
     w2v_ref, saux_ref, waux_ref, o_ref, kvo_ref, wino_ref, raw) = refs[n_pages:]
    seq_len = past_len + tq
    n_sel = -(-seq_len // SEL_BLOCK)
    n_planes = N_KV_COMP * N_KV_HEADS
    n_wplanes = 2 * N_KV_HEADS
    page = pages[0].shape[1] // n_planes
    n_chunk = past_len // CMP_STRIDE
    zpad = jnp.zeros((LANE - tq, HEAD_DIM), BF16)

    for c in range(n_planes):
        kvo_ref[pl.ds(c, tq, stride=n_planes), :] = kvn_ref[:, c * HEAD_DIM:(c + 1) * HEAD_DIM]
    keep = (win_buf - tq) * n_wplanes
    wino_ref[0, 0:keep, :] = win_ref[0, tq * n_wplanes:win_buf * n_wplanes, :]
    for c in range(n_wplanes):
        wino_ref[0, pl.ds(keep + c, tq, stride=n_wplanes), :] = winn_ref[:, c * HEAD_DIM:(c + 1) * HEAD_DIM]

    for cb in range(2 * N_KV_HEADS):
        for p in range(n_pages):
            _scatter_rows(raw, cb, p * (page // CMP_STRIDE),
                          pages[p][0, pl.ds(cb, page, stride=n_planes), :])
    kc, vc = _compress_core(raw, ((w1k_ref, bk_ref, w2k_ref), (w1v_ref, bv_ref, w2v_ref)), n_chunk)

    sig = jax.nn.sigmoid(gate_ref[...])
    q4s, consts, o_cs, p_sums = [], [], [], []
    for g in range(N_KV_HEADS):
        q4 = _stack_heads(q_ref[:, g * GQA_REP * HEAD_DIM:(g + 1) * GQA_REP * HEAD_DIM], tq)
        slope, t = _row_consts(g, tq, past_len)
        o_c, p_sum = _cmp_branch(q4, kc[g], vc[g], slope, t, tq)
        q4s.append(q4); consts.append((slope, t)); o_cs.append(o_c); p_sums.append(p_sum)
    t_row = past_len + lax.broadcasted_iota(I32, (1, N_KV_HEADS * tq), 1) % tq
    notsel = _not_selected(jnp.concatenate(p_sums, axis=0), t_row, n_sel)

    for g in range(N_KV_HEADS):
        q4, (slope, t), o_c = q4s[g], consts[g], o_cs[g]
        q_sel = jnp.concatenate([q4, _query_aux(slope, _rep_rows(notsel[g * tq:(g + 1) * tq]))], axis=1)
        q_win = jnp.concatenate([q4, _query_aux(slope, None)], axis=1)

        kpl = 2 * N_KV_HEADS + g
        vpl = 3 * N_KV_HEADS + g
        k = jnp.concatenate(
            [p[0, pl.ds(kpl, page, stride=n_planes), :].astype(BF16) for p in pages]
            + [kvn_ref[:, kpl * HEAD_DIM:(kpl + 1) * HEAD_DIM].astype(BF16), zpad], axis=0)
        v = jnp.concatenate(
            [p[0, pl.ds(vpl, page, stride=n_planes), :].astype(BF16) for p in pages]
            + [kvn_ref[:, vpl * HEAD_DIM:(vpl + 1) * HEAD_DIM].astype(BF16), zpad], axis=0)
        pos = lax.broadcasted_iota(I32, (1, k.shape[0]), 1)
        s = _nt_dot(q_sel, jnp.concatenate([k, saux_ref[...]], axis=1))
        p = _softmax_rows(jnp.where(pos <= t, s, NEG))
        o_s = jnp.dot(p.astype(BF16), v, preferred_element_type=F32)

        kw = jnp.concatenate([win_ref[0, pl.ds(g, win_buf, stride=n_wplanes), :].astype(BF16),
                              winn_ref[:, g * HEAD_DIM:(g + 1) * HEAD_DIM].astype(BF16), zpad],
                             axis=0)
        vp = N_KV_HEADS + g
        vw = jnp.concatenate([win_ref[0, pl.ds(vp, win_buf, stride=n_wplanes), :].astype(BF16),
                              winn_ref[:, vp * HEAD_DIM:(vp + 1) * HEAD_DIM].astype(BF16), zpad],
                             axis=0)
        kpos = (past_len - win_buf) + lax.broadcasted_iota(I32, (1, kw.shape[0]), 1)
        rel = t - kpos
        s = _nt_dot(q_win, jnp.concatenate([kw, waux_ref[...]], axis=1))
        p = _softmax_rows(jnp.where((rel >= 0) & (rel < WINDOW), s, NEG))
        o_w = jnp.dot(p.astype(BF16), vw, preferred_element_type=F32)

        outs = _gated_sum(sig, g, o_c, o_s, o_w, tq)
        for r in range(GQA_REP):
            h = g * GQA_REP + r
            o_ref[:, h * HEAD_DIM:(h + 1) * HEAD_DIM] = outs[r]


def _sample_attention(page_table, cache, proj, state_win, cmp_w, n_prompt_rows, tq, n_cols_blocks):
    db, n_pages = page_table.shape
    n_planes = N_KV_COMP * N_KV_HEADS
    n_wplanes = 2 * N_KV_HEADS
    page = cache.shape[1] // n_planes
    past_len = n_pages * page
    win_buf = state_win.shape[1] // n_wplanes
    r0 = n_prompt_rows // tq
    gate_cb = n_cols_blocks - 1
    sel_aux = _pos_aux(np.arange(past_len + LANE))
    win_aux = _pos_aux(past_len - win_buf + np.arange(win_buf + LANE))

    def page_map(p):
        return lambda b, pt: (pt[b, p], 0, 0)

    full = lambda a: pl.BlockSpec(a.shape, lambda b, pt: (0,) * a.ndim)
    in_specs = [pl.BlockSpec((1,) + cache.shape[1:], page_map(p)) for p in range(n_pages)]
    in_specs += [
        pl.BlockSpec((tq, ATTN_WIDTH), lambda b, pt: (r0 + b, 0)),
        pl.BlockSpec((tq, N_KV_COMP * KV_WIDTH), lambda b, pt: (r0 + b, 1)),
        pl.BlockSpec((tq, 2 * KV_WIDTH), lambda b, pt: (r0 + b, CB_WIN * LANE // (2 * KV_WIDTH))),
        pl.BlockSpec((tq, LANE), lambda b, pt: (r0 + b, gate_cb)),
        pl.BlockSpec((1,) + state_win.shape[1:], lambda b, pt: (b, 0, 0)),
    ] + [full(w) for w in cmp_w] + [full(sel_aux), full(win_aux)]
    return pl.pallas_call(
        functools.partial(_sample_attn_kernel, n_pages=n_pages, tq=tq, past_len=past_len,
                          win_buf=win_buf),
        out_shape=(jax.ShapeDtypeStruct((db * tq, ATTN_WIDTH), F32),
                   jax.ShapeDtypeStruct((db * tq * n_planes, HEAD_DIM), F32),
                   jax.ShapeDtypeStruct(state_win.shape, F32)),
        grid_spec=pltpu.PrefetchScalarGridSpec(
            num_scalar_prefetch=1, grid=(db,), in_specs=in_specs,
            out_specs=(pl.BlockSpec((tq, ATTN_WIDTH), lambda b, pt: (b, 0)),
                       pl.BlockSpec((tq * n_planes, HEAD_DIM), lambda b, pt: (b, 0)),
                       pl.BlockSpec((1,) + state_win.shape[1:], lambda b, pt: (b, 0, 0))),
            scratch_shapes=[pltpu.VMEM((2 * N_KV_HEADS, CMP_STRIDE * S_PITCH, HEAD_DIM), F32)]),
        compiler_params=_cparams(("parallel",)),
    )(page_table, *([cache] * n_pages), proj, proj, proj, proj, state_win, *cmp_w, sel_aux, win_aux)


def _interleave_kernel(x_ref, o_ref, *, n_planes):
    rows = x_ref.shape[0]
    for c in range(n_planes):
        o_ref[pl.ds(c, rows, stride=n_planes), :] = x_ref[:, c * LANE:(c + 1) * LANE]


def _interleave(proj, n_blocks, block_rows, n_planes, row_block_of, col_block):
    return pl.pallas_call(
        functools.partial(_interleave_kernel, n_planes=n_planes),
        out_shape=jax.ShapeDtypeStruct((n_blocks * block_rows * n_planes, LANE), F32),
        grid=(n_blocks,),
        in_specs=[pl.BlockSpec((block_rows, n_planes * LANE), lambda i: (row_block_of(i), col_block))],
        out_specs=pl.BlockSpec((block_rows * n_planes, LANE), lambda i: (i, 0)),
        compiler_params=_cparams(("parallel",)),
    )(proj)


def _conv_kernel(*refs, seq_len, has_state):
    if has_state:
        b_ref, c_ref, h_ref, w_ref, s1_ref, s2_ref, y_ref, u_ref = refs
    else:
        b_ref, c_ref, h_ref, w_ref, y_ref, u_ref = refs
    u = c_ref[...] * h_ref[...]
    rows = u.shape[0]
    t = lax.broadcasted_iota(I32, (rows, 1), 0) % seq_len
    p1 = jnp.where(t >= 1, pltpu.roll(u, 1, 0), 0.0)
    p2 = jnp.where(t >= 2, pltpu.roll(u, 2, 0), 0.0)
    if has_state:
        p1 = p1 + s1_ref[...]
        p2 = p2 + s2_ref[...]
    y = w_ref[0:1, :] * p2
    y = y + w_ref[1:2, :] * p1
    y = y + w_ref[2:3, :] * u
    y_ref[...] = b_ref[...] * y
    u_ref[...] = u[rows - u_ref.shape[0]:, :]


def _short_conv(proj, conv_w, row_block0, n_blocks, block_rows, seq_len, state=None):
    conv_dim = conv_w.shape[1]
    n_ct = conv_dim // LANE
    tail = SUBLANE if state is None else block_rows
    spec = lambda cb: pl.BlockSpec((block_rows, LANE), lambda r, c: (row_block0 + r, cb + c))
    in_specs = [spec(CB_CONV), spec(CB_CONV + n_ct), spec(CB_CONV + 2 * n_ct),
                pl.BlockSpec((CONV_WIDTH, LANE), lambda r, c: (0, c))]
    args = [proj, proj, proj, conv_w]
    if state is not None:
        st_spec = pl.BlockSpec((block_rows, LANE), lambda r, c: (r, c))
        in_specs += [st_spec, st_spec]
        args += list(state)
    return pl.pallas_call(
        functools.partial(_conv_kernel, seq_len=seq_len, has_state=state is not None),
        out_shape=(jax.ShapeDtypeStruct((n_blocks * block_rows, conv_dim), F32),
                   jax.ShapeDtypeStruct((n_blocks * tail, conv_dim), F32)),
        grid=(n_blocks, n_ct),
        in_specs=in_specs,
        out_specs=(pl.BlockSpec((block_rows, LANE), lambda r, c: (r, c)),
                   pl.BlockSpec((tail, LANE), lambda r, c: (r, c))),
        compiler_params=_cparams(("parallel", "parallel")),
    )(*args)


def _layer_norm(x, g, b):
    mu = jnp.mean(x, axis=-1, keepdims=True)
    var = jnp.mean(jnp.square(x - mu), axis=-1, keepdims=True)
    return (x - mu) * lax.rsqrt(var + LN_EPS) * g + b


def _mix_kernel(oa_p, oa_s, oc_p, oc_s, x_p, x_s, wa_ref, wc_ref, g_ref, b_ref, wr_ref, br_ref,
                x1_ref, e_ref, gt_ref, *, dn_alpha, n_experts, n_first):
    o_attn = _pair_load(oa_p, oa_s, n_first)
    o_conv = _pair_load(oc_p, oc_s, n_first)
    mix = jnp.dot(o_attn.astype(BF16), wa_ref[...], preferred_element_type=F32)
    mix = mix + jnp.dot(o_conv.astype(BF16), wc_ref[...], preferred_element_type=F32)
    x1 = _layer_norm(dn_alpha * _pair_load(x_p, x_s, n_first) + mix, g_ref[...], b_ref[...])
    x1_ref[...] = x1
    x_hi = x1.astype(BF16)
    x_lo = (x1 - x_hi.astype(F32)).astype(BF16)
    w_r = wr_ref[...]
    w_hi = w_r.astype(BF16)
    w_lo = (w_r - w_hi.astype(F32)).astype(BF16)
    logits = (jnp.dot(x_hi, w_hi, preferred_element_type=F32)
              + jnp.dot(x_lo, w_hi, preferred_element_type=F32)
              + jnp.dot(x_hi, w_lo, preferred_element_type=F32)) + br_ref[...]
    tm = logits.shape[0]
    lane = lax.broadcasted_iota(I32, (tm, LANE), 1)
    score = jnp.where(lane < n_experts, logits, -jnp.inf)
    e_out = jnp.zeros((tm, LANE), I32)
    v_out = jnp.zeros((tm, LANE), F32)
    v0 = None
    den = jnp.zeros((tm, 1), F32)
    for k in range(TOP_K):
        m = jnp.max(score, axis=-1, keepdims=True)
        idx = jnp.min(jnp.where(score == m, lane, LANE), axis=-1, keepdims=True)
        if k == 0:
            v0 = m
        ex = jnp.exp(m - v0)
        den = den + ex
        e_out = jnp.where(lane == k, idx, e_out)
        v_out = jnp.where(lane == k, ex, v_out)
        score = jnp.where(lane == idx, -jnp.inf, score)
    e_ref[...] = e_out
    gt_ref[...] = v_out / den


def _mix_ln_router(o_attn, o_conv, x, w_out_a, w_out_c, ln_g, ln_b, w_r, b_r, dn_alpha, n_experts):
    n = x[0].shape[0] + x[1].shape[0]
    d = x[0].shape[1]
    tm = _pick(x[1].shape[0], 256)
    n_first = x[0].shape[0] // tm
    row = lambda w: pl.BlockSpec((tm, w), lambda i: (i, 0))
    full = lambda a: pl.BlockSpec(a.shape, lambda i: (0, 0))
    pairs = lambda p: _pair_specs(tm, p[0].shape[1], n_first)
    return pl.pallas_call(
        functools.partial(_mix_kernel, dn_alpha=dn_alpha, n_experts=n_experts, n_first=n_first),
        out_shape=(jax.ShapeDtypeStruct((n, d), F32),
                   jax.ShapeDtypeStruct((n, LANE), I32), jax.ShapeDtypeStruct((n, LANE), F32)),
        grid=(n // tm,),
        in_specs=pairs(o_attn) + pairs(o_conv) + pairs(x) + [
            full(w_out_a), full(w_out_c), full(ln_g), full(ln_b), full(w_r), full(b_r)],
        out_specs=(row(d), row(LANE), row(LANE)),
        compiler_params=_cparams(("parallel",)),
    )(*o_attn, *o_conv, *x, w_out_a, w_out_c, ln_g, ln_b, w_r, b_r)


def _weight_pipeline(f, m, nf, chg_ref, nxt_ref, wrap_ref, cnt_ref, copies, cast):
    @pl.when((f == 0) & (m == 0))
    def _():
        cnt_ref[0] = 0
        for c in copies(nxt_ref[0], 0, 0):
            c.start()

    @pl.when(chg_ref[m] == 1)
    def _():
        slot = cnt_ref[0] % 2
        f_next = f + wrap_ref[m]

        @pl.when(f_next < nf)
        def _():
            for c in copies(nxt_ref[m + 1], f_next, 1 - slot):
                c.start()

        for c in copies(0, 0, slot):
            c.wait()
        cast(slot)
        cnt_ref[0] = cnt_ref[0] + 1


def _gate_up_kernel(be_ref, br_ref, chg_ref, nxt_ref, wrap_ref, x_ref, w_hbm, bg_ref, bu_ref, a_ref,
                    wbuf, wgb, wub, sem, cnt_ref, *, nf, tf):
    f = pl.program_id(0)
    m = pl.program_id(1)
    rows = br_ref[m]
    half = MOE_TM // 2

    def copies(e, fi, slot):
        return [pltpu.make_async_copy(
            w_hbm.at[e, :, pl.ds(pl.multiple_of((part * nf + fi) * tf, tf), tf)],
            wbuf.at[slot, part], sem.at[slot, part]) for part in range(2)]

    def cast(slot):
        wgb[...] = wbuf[slot, 0].astype(BF16)
        wub[...] = wbuf[slot, 1].astype(BF16)

    _weight_pipeline(f, m, nf, chg_ref, nxt_ref, wrap_ref, cnt_ref, copies, cast)

    def act(x):
        hg = jnp.dot(x, wgb[...], preferred_element_type=F32) + bg_ref[0]
        hu = jnp.dot(x, wub[...], preferred_element_type=F32) + bu_ref[0]
        hg = jnp.minimum(hg, SWIGLU_LIMIT)
        hu = jnp.clip(hu, -SWIGLU_LIMIT, SWIGLU_LIMIT)
        return (hg * jax.nn.sigmoid(SWIGLU_ALPHA * hg) * (hu + 1.0)).astype(BF16)

    @pl.when(rows > half)
    def _():
        a_ref[...] = act(x_ref[...].astype(BF16))

    @pl.when((rows > 0) & (rows <= half))
    def _():
        a_ref[0:half, :] = act(x_ref[0:half, :].astype(BF16))
        a_ref[half:, :] = jnp.zeros((MOE_TM - half, a_ref.shape[1]), BF16)

    @pl.when(rows == 0)
    def _():
        a_ref[...] = jnp.zeros_like(a_ref)


def _down_kernel(be_ref, br_ref, chg_ref, nxt_ref, wrap_ref, a_ref, w_hbm, b_ref, y_ref,
                 wbuf, wb, sem, cnt_ref, *, nj, tn):
    j = pl.program_id(0)
    m = pl.program_id(1)
    rows = br_ref[m]
    half = MOE_TM // 2

    def copies(e, ji, slot):
        return [pltpu.make_async_copy(
            w_hbm.at[e, :, pl.ds(pl.multiple_of(ji * tn, tn), tn)], wbuf.at[slot], sem.at[slot])]

    def cast(slot):
        wb[...] = wbuf[slot].astype(BF16)

    _weight_pipeline(j, m, nj, chg_ref, nxt_ref, wrap_ref, cnt_ref, copies, cast)

    @pl.when(rows > half)
    def _():
        y_ref[...] = jnp.dot(a_ref[...], wb[...], preferred_element_type=F32) + b_ref[0]

    @pl.when((rows > 0) & (rows <= half))
    def _():
        y_ref[0:half, :] = jnp.dot(a_ref[0:half, :], wb[...], preferred_element_type=F32) + b_ref[0]
        y_ref[half:, :] = jnp.zeros((MOE_TM - half, y_ref.shape[1]), F32)

    @pl.when(rows == 0)
    def _():
        y_ref[...] = jnp.zeros_like(y_ref)


def _experts(plan, xr, w_gu, b_gu, w_down, b_down):
    block_e, block_rows, chg, nxt, wrap = plan
    rows, d = xr.shape
    d_ff = w_down.shape[1]
    n_blocks = rows // MOE_TM
    tf = _pick(d_ff, 1024, LANE)
    nf = d_ff // tf
    a = pl.pallas_call(
        functools.partial(_gate_up_kernel, nf=nf, tf=tf),
        out_shape=jax.ShapeDtypeStruct((rows, d_ff), BF16),
        grid_spec=pltpu.PrefetchScalarGridSpec(
            num_scalar_prefetch=5, grid=(nf, n_blocks),
            in_specs=[pl.BlockSpec((MOE_TM, d), lambda f, m, be, *_: (m, 0)),
                      pl.BlockSpec(memory_space=pl.ANY),
                      pl.BlockSpec((1, 1, tf), lambda f, m, be, *_: (be[m], 0, f)),
                      pl.BlockSpec((1, 1, tf), lambda f, m, be, *_: (be[m], 0, nf + f))],
            out_specs=pl.BlockSpec((MOE_TM, tf), lambda f, m, be, *_: (m, f)),
            scratch_shapes=[pltpu.VMEM((2, 2, d, tf), F32), pltpu.VMEM((d, tf), BF16),
                            pltpu.VMEM((d, tf), BF16), pltpu.SemaphoreType.DMA((2, 2)),
                            pltpu.SMEM((1,), I32)]),
        compiler_params=_cparams(("arbitrary", "arbitrary")),
    )(block_e, block_rows, chg, nxt, wrap, xr, w_gu, b_gu, b_gu)
    tn = _pick(d, 1024, LANE)
    nj = d // tn
    return pl.pallas_call(
        functools.partial(_down_kernel, nj=nj, tn=tn),
        out_shape=jax.ShapeDtypeStruct((rows, d), F32),
        grid_spec=pltpu.PrefetchScalarGridSpec(
            num_scalar_prefetch=5, grid=(nj, n_blocks),
            in_specs=[pl.BlockSpec((MOE_TM, d_ff), lambda j, m, be, *_: (m, 0)),
                      pl.BlockSpec(memory_space=pl.ANY),
                      pl.BlockSpec((1, 1, tn), lambda j, m, be, *_: (be[m], 0, j))],
            out_specs=pl.BlockSpec((MOE_TM, tn), lambda j, m, be, *_: (m, j)),
            scratch_shapes=[pltpu.VMEM((2, d_ff, tn), F32), pltpu.VMEM((d_ff, tn), BF16),
                            pltpu.SemaphoreType.DMA((2,)), pltpu.SMEM((1,), I32)]),
        compiler_params=_cparams(("arbitrary", "arbitrary")),
    )(block_e, block_rows, chg, nxt, wrap, a, w_down, b_down)


def _combine_kernel(x1_ref, y_ref, gt_ref, g_ref, b_ref, o_ref, *, dn_alpha):
    gt = gt_ref[...]
    moe = y_ref[0] * gt[:, 0:1]
    for k in range(1, TOP_K):
        moe = moe + y_ref[k] * gt[:, k:k + 1]
    o_ref[...] = _layer_norm(dn_alpha * x1_ref[...] + moe, g_ref[...], b_ref[...])


def _combine_ln(x1, yk, gates, ln_g, ln_b, dn_alpha, row0, n):
    d = x1.shape[1]
    tm = _pick(n, 256)
    r0 = row0 // tm
    return pl.pallas_call(
        functools.partial(_combine_kernel, dn_alpha=dn_alpha),
        out_shape=jax.ShapeDtypeStruct((n, d), F32),
        grid=(n // tm,),
        in_specs=[pl.BlockSpec((tm, d), lambda i: (r0 + i, 0)),
                  pl.BlockSpec((TOP_K, tm, d), lambda i: (0, r0 + i, 0)),
                  pl.BlockSpec((tm, LANE), lambda i: (r0 + i, 0)),
                  pl.BlockSpec((1, d), lambda i: (0, 0)),
                  pl.BlockSpec((1, d), lambda i: (0, 0))],
        out_specs=pl.BlockSpec((tm, d), lambda i: (i, 0)),
        compiler_params=_cparams(("parallel",)),
    )(x1, yk, gates, ln_g, ln_b)


def _route(top_e, n_experts):
    n = top_e.shape[0]
    n4 = n * TOP_K
    e_flat = top_e.reshape(-1)
    entry = jnp.arange(n4, dtype=I32)
    _, order = lax.sort((e_flat, entry), num_keys=1, is_stable=True)
    experts = jnp.arange(n_experts, dtype=I32)
    counts = jnp.sum((e_flat[:, None] == experts[None, :]).astype(I32), axis=0)
    ends = jnp.cumsum(counts)
    starts = ends - counts
    padded = (counts + MOE_TM - 1) // MOE_TM * MOE_TM
    pends = jnp.cumsum(padded)
    pstarts = pends - padded
    dest_s = entry + jnp.sum(jnp.where(entry[:, None] >= ends[None, :],
                                       (padded - counts)[None, :], 0), axis=1)
    _, pos = lax.sort((order, dest_s), num_keys=1)
    n_blocks = -(-n4 // MOE_TM) + n_experts
    block0 = jnp.arange(n_blocks, dtype=I32) * MOE_TM
    block_e = jnp.minimum(jnp.sum((pends[None, :] <= block0[:, None]).astype(I32), axis=1),
                          n_experts - 1)
    block_rows = jnp.clip(counts[block_e] - (block0 - pstarts[block_e]), 0, MOE_TM)
    in_block = jnp.arange(MOE_TM, dtype=I32)[None, :]
    src = (block0 + starts[block_e] - pstarts[block_e])[:, None] + in_block
    row_tok = jnp.where(in_block < block_rows[:, None],
                        order[jnp.clip(src, 0, n4 - 1)] // TOP_K, 0).reshape(-1)
    blk = jnp.arange(n_blocks, dtype=I32)
    prev_e = jnp.concatenate([block_e[:1] - 1, block_e[:-1]])
    chg = (block_rows > 0) & (block_e != prev_e)
    later = jnp.where(chg, blk, n_blocks)
    nxt_idx = lax.cummin(jnp.concatenate([later[1:], later[:1] * 0 + n_blocks]), reverse=True)
    wrap = nxt_idx >= n_blocks
    nxt_e = block_e[jnp.where(wrap, 0, nxt_idx)]
    nxt = jnp.concatenate([block_e[:1], nxt_e])
    plan = (block_e, block_rows.astype(I32), chg.astype(I32), nxt.astype(I32), wrap.astype(I32))
    return row_tok, pos.reshape(n, TOP_K), plan


def _layer(x_prompt, x_sample, cache, state_win, state_conv, page_table,
           w_in, w_out, conv_w, cmp_k_pe, cmp_k_w1, cmp_k_w2, cmp_v_pe, cmp_v_w1, cmp_v_w2,
           ln1_g, ln1_b, w_router, b_router, w_gu, b_gu, w_down, b_down, ln2_g, ln2_b, depth):
    nb, seq_len, d = x_prompt.shape
    db, dec_seq, _ = x_sample.shape
    n_planes = N_KV_COMP * N_KV_HEADS
    n_wplanes = 2 * N_KV_HEADS
    page = cache.shape[1] // n_planes
    n_pages = page_table.shape[1]
    past_len = n_pages * page
    win_buf = state_win.shape[1] // n_wplanes
    conv_dim = conv_w.shape[1]
    n_experts = w_router.shape[1]
    n_p, n_s = nb * seq_len, db * dec_seq
    dn_alpha = (2.0 * depth) ** 0.25
    assert page == LANE and dec_seq == SUBLANE and seq_len % Q_BLOCK == 0
    assert seq_len >= WINDOW + Q_BLOCK and win_buf == WINDOW and seq_len % win_buf == 0
    assert n_p % n_s == 0 and conv_dim % LANE == 0 and n_experts <= LANE
    assert past_len + dec_seq <= 2 * 32 * SEL_BLOCK and seq_len <= 32 * SEL_BLOCK

    off_gate = ATTN_WIDTH + N_KV_COMP * KV_WIDTH + 2 * KV_WIDTH
    off_conv = off_gate + N_GATE
    w_p = jnp.concatenate([w_in[:, :off_gate], w_in[:, off_conv:], w_in[:, off_gate:off_conv],
                           jnp.zeros((d, LANE - N_GATE), w_in.dtype)], axis=1).astype(BF16)
    n_cb = w_p.shape[1] // LANE
    xp, xs = x_prompt.reshape(n_p, d), x_sample.reshape(n_s, d)
    n_tok = n_p + n_s
    proj = _proj(xp, xs, w_p, _pick(n_s, 512), _pick(w_p.shape[1], 1152, LANE))

    kv_cb = ATTN_WIDTH // (n_planes * LANE)
    tb = _pick(n_p, 256)
    kv_prompt = _interleave(proj, n_p // tb, tb, n_planes, lambda i: i, kv_cb)
    win_cb = (ATTN_WIDTH + n_planes * LANE) // (n_wplanes * LANE)
    per_seq = seq_len // win_buf
    win_prompt = _interleave(proj, nb, win_buf, n_wplanes,
                             lambda i: i * per_seq + per_seq - 1, win_cb)

    w1k = jnp.concatenate(jnp.split(cmp_k_w1, CMP_RATIO, axis=0), axis=1).astype(BF16)
    w1v = jnp.concatenate(jnp.split(cmp_v_w1, CMP_RATIO, axis=0), axis=1).astype(BF16)
    bk = _pe_bias(cmp_k_pe, cmp_k_w1.astype(BF16))
    bv = _pe_bias(cmp_v_pe, cmp_v_w1.astype(BF16))
    cmp_w = (w1k, w1v, bk, bv, cmp_k_w2.astype(BF16), cmp_v_w2.astype(BF16))
    pp = seq_len // LANE
    prompt_ids = jnp.arange(nb * pp, dtype=I32).reshape(nb, pp)
    proj_pages = proj.reshape(n_tok // LANE, LANE, n_cb * LANE)
    kc_p, vc_p = _compress(prompt_ids, proj_pages, ATTN_WIDTH // (2 * KV_WIDTH), *cmp_w)

    o_p = _prompt_attention(proj, kc_p, vc_p, nb, seq_len, n_cb)
    o_s, kv_sample, win_sample = _sample_attention(page_table, cache, proj, state_win, cmp_w,
                                                   n_p, dec_seq, n_cb)

    c_p, u_p = _short_conv(proj, conv_w, 0, nb, seq_len, seq_len)
    z = jnp.zeros((db, dec_seq - 2, conv_dim), F32)
    s1 = jnp.concatenate([state_conv[:, 1:2], z, z[:, :1]], axis=1).reshape(n_s, conv_dim)
    s2 = jnp.concatenate([state_conv, z], axis=1).reshape(n_s, conv_dim)
    c_s, u_s = _short_conv(proj, conv_w, n_p // n_s, 1, n_s, dec_seq, state=(s1, s2))
    conv_prompt = u_p.reshape(nb, SUBLANE, conv_dim)[:, SUBLANE - (CONV_WIDTH - 1):]
    conv_sample = u_s.reshape(db, dec_seq, conv_dim)[:, dec_seq - (CONV_WIDTH - 1):]

    w_r = jnp.pad(w_router, ((0, 0), (0, LANE - n_experts)))
    b_r = jnp.pad(b_router, (0, LANE - n_experts)).reshape(1, LANE)
    x1, e_idx, gates = _mix_ln_router(
        (o_p, o_s), (c_p, c_s), (xp, xs), w_out[:ATTN_WIDTH].astype(BF16),
        w_out[ATTN_WIDTH:].astype(BF16), ln1_g.reshape(1, d), ln1_b.reshape(1, d), w_r, b_r,
        dn_alpha, n_experts)

    row_tok, pos, plan = _route(e_idx[:, :TOP_K], n_experts)
    y_rows = _experts(plan, x1[row_tok], w_gu, b_gu.reshape(n_experts, 1, -1),
                      w_down, b_down.reshape(n_experts, 1, -1))

    comb = (x1, y_rows[pos.T], gates, ln2_g.reshape(1, d), ln2_b.reshape(1, d), dn_alpha)
    y_prompt = _combine_ln(*comb, 0, n_p).reshape(nb, seq_len, d)
    y_sample = _combine_ln(*comb, n_p, n_s).reshape(db, dec_seq, d)
    kv_shape = (N_KV_COMP, N_KV_HEADS, HEAD_DIM)
    win_shape = (win_buf, 2, N_KV_HEADS, HEAD_DIM)
    return (y_prompt, y_sample,
            kv_prompt.reshape((nb, seq_len) + kv_shape), kv_sample.reshape((db, dec_seq) + kv_shape),
            win_prompt.reshape((nb,) + win_shape), win_sample.reshape((db,) + win_shape),
            conv_prompt, conv_sample)


def kernel(x_prompt, x_sample, cache_kv, state_win, state_conv, page_table, w_in, w_out, conv_w, cmp_k_pe, cmp_k_w1, cmp_k_w2, cmp_v_pe, cmp_v_w1, cmp_v_w2, ln1_g, ln1_b, w_router, b_router, w_gu, b_gu, w_down, b_down, ln2_g, ln2_b):
    depth, n_pool, page = cache_kv.shape[:3]
    db, win_buf = state_win.shape[1:3]
    cache_rows = cache_kv.reshape(depth * n_pool, page * N_KV_COMP * N_KV_HEADS, HEAD_DIM)
    win_rows = state_win.reshape(depth, db, win_buf * 2 * N_KV_HEADS, HEAD_DIM)
    y_prompt, y_sample = x_prompt, x_sample
    outs = [[] for _ in range(6)]
    for layer in range(depth):
        res = _layer(y_prompt, y_sample, cache_rows, win_rows[layer], state_conv[layer],
                     page_table + layer * n_pool,
                     w_in[layer], w_out[layer], conv_w[layer],
                     cmp_k_pe[layer], cmp_k_w1[layer], cmp_k_w2[layer],
                     cmp_v_pe[layer], cmp_v_w1[layer], cmp_v_w2[layer],
                     ln1_g[layer], ln1_b[layer], w_router[layer], b_router[layer],
                     w_gu[layer], b_gu[layer], w_down[layer], b_down[layer],
                     ln2_g[layer], ln2_b[layer], depth)
        y_prompt, y_sample = res[0], res[1]
        for lst, val in zip(outs, res[2:]):
            lst.append(val)
    return (y_prompt, y_sample) + tuple(jnp.stack(o) for o in outs)
```

```python
import functools
import math

import numpy as np
import jax
import jax.numpy as jnp
from jax import lax
from jax.experimental import pallas as pl
from jax.experimental.pallas import tpu as pltpu

F32 = jnp.float32
BF16 = jnp.bfloat16
I32 = jnp.int32

HEAD_DIM = 128
N_HEADS = 8
N_KV_HEADS = 2
GQA_REP = N_HEADS // N_KV_HEADS
N_KV_COMP = 4
KV_WIDTH = N_KV_HEADS * HEAD_DIM
ATTN_WIDTH = N_HEADS * HEAD_DIM
CONV_WIDTH = 3
CMP_BLOCK = 32
CMP_STRIDE = 16
CMP_RATIO = CMP_BLOCK // CMP_STRIDE
CMP_HIDDEN = 256
SEL_BLOCK = 64
SEL_TOPK = 8
WINDOW = 512
N_BRANCH = 3
Q_BLOCK = 128
TOP_K = 4
SWIGLU_LIMIT = 7.0
SWIGLU_ALPHA = 1.702
MOE_TM = 256
LN_EPS = 1e-5
NEG = -1e30
BIG = 1e30
LANE = 128
SUBLANE = 8
VMEM_LIMIT = 56 * 1024 * 1024
SEL_CHUNK = 512
MASK_M = -2.0 ** 100
AUX_HI = 64
AUX_LO = 65

CB_KV = ATTN_WIDTH // LANE
CB_WIN = CB_KV + N_KV_COMP * N_KV_HEADS
CB_CONV = CB_WIN + 2 * N_KV_HEADS
N_GATE = N_BRANCH * N_HEADS


def _alibi_slopes(n):
    def pow2(m):
        start = 2.0 ** (-8.0 / m)
        return [start ** (i + 1) for i in range(m)]
    if math.log2(n).is_integer():
        s = pow2(n)
    else:
        c = 2 ** math.floor(math.log2(n))
        s = pow2(c) + pow2(2 * c)[0::2][:n - c]
    return [float(np.float32(v)) for v in s]


SLOPES = _alibi_slopes(N_HEADS)


def _pick(n, target, mult=SUBLANE):
    for d in range(min(n, target), 0, -1):
        if n % d == 0 and d % mult == 0:
            return d
    raise ValueError((n, target, mult))


def _cparams(sem):
    return pltpu.CompilerParams(dimension_semantics=sem, vmem_limit_bytes=VMEM_LIMIT)


def _pair_specs(tm, width, n_first_blocks):
    return [pl.BlockSpec((tm, width), lambda i, *_: (jnp.minimum(i, n_first_blocks - 1), 0)),
            pl.BlockSpec((tm, width), lambda i, *_: (jnp.maximum(i - n_first_blocks, 0), 0))]


def _pair_load(first_ref, second_ref, n_first_blocks):
    return jnp.where(pl.program_id(0) < n_first_blocks, first_ref[...], second_ref[...])


def _proj_kernel(xp_ref, xs_ref, w_ref, o_ref, *, n_first):
    x = _pair_load(xp_ref, xs_ref, n_first)
    o_ref[...] = jnp.dot(x.astype(BF16), w_ref[...], preferred_element_type=F32)


def _proj(xp, xs, w, tm, tn):
    k = xp.shape[1]
    n = w.shape[1]
    n_first = xp.shape[0] // tm
    m = xp.shape[0] + xs.shape[0]
    return pl.pallas_call(
        functools.partial(_proj_kernel, n_first=n_first),
        out_shape=jax.ShapeDtypeStruct((m, n), F32),
        grid=(m // tm, n // tn),
        in_specs=_pair_specs(tm, k, n_first) + [pl.BlockSpec((k, tn), lambda i, j: (0, j))],
        out_specs=pl.BlockSpec((tm, tn), lambda i, j: (i, j)),
        compiler_params=_cparams(("parallel", "arbitrary")),
    )(xp, xs, w)


def _pe_bias_kernel(pe_ref, w1_ref, o_ref):
    o_ref[...] = jnp.dot(pe_ref[...].astype(BF16), w1_ref[...], preferred_element_type=F32)


def _pe_bias(pe, w1_bf):
    flat = jnp.broadcast_to(pe.reshape(1, -1), (SUBLANE, pe.size))
    return pl.pallas_call(
        _pe_bias_kernel,
        out_shape=jax.ShapeDtypeStruct((SUBLANE, CMP_HIDDEN), F32),
    )(flat, w1_bf)


S_PITCH = 136


def _scatter_rows(raw, plane, base, x):
    per = SUBLANE
    for j in range(x.shape[0] // per):
        c, s0 = divmod(j * per, CMP_STRIDE)
        raw[plane, pl.ds(s0 * S_PITCH + base + c, per, stride=S_PITCH), :] = x[j * per:(j + 1) * per]


def _compress_core(raw, weights, n_chunk):
    assert n_chunk <= S_PITCH
    outs = []
    for kv, (w1_ref, b_ref, w2_ref) in enumerate(weights):
        acc = jnp.zeros((N_KV_HEADS * n_chunk, CMP_RATIO * CMP_HIDDEN), F32)
        for sp in range(CMP_STRIDE // 2):
            per_g = []
            for g in range(N_KV_HEADS):
                halves = [raw[kv * N_KV_HEADS + g, pl.ds(s * S_PITCH, n_chunk), :]
                          for s in (2 * sp, 2 * sp + 1)]
                per_g.append(jnp.concatenate(halves, axis=1))
            lhs = jnp.concatenate(per_g, axis=0).astype(BF16)
            acc = acc + jnp.dot(lhs, w1_ref[pl.ds(sp * 2 * HEAD_DIM, 2 * HEAD_DIM), :],
                                preferred_element_type=F32)
        per_kv = []
        for g in range(N_KV_HEADS):
            part0 = acc[g * n_chunk:(g + 1) * n_chunk, :CMP_HIDDEN]
            part1 = acc[g * n_chunk:(g + 1) * n_chunk, CMP_HIDDEN:]
            part1 = jnp.concatenate([part1[1:], part1[:1]], axis=0)
            pre = (b_ref[0:1, :] + part0) + part1
            per_kv.append(jnp.dot(jax.nn.gelu(pre).astype(BF16), w2_ref[...],
                                  preferred_element_type=F32))
        outs.append(per_kv)
    return outs


def _compress_kernel(ids_ref, *refs, n_pages):
    pages = refs[:n_pages]
    w1k_ref, w1v_ref, bk_ref, bv_ref, w2k_ref, w2v_ref, kc_ref, vc_ref, raw = refs[n_pages:]
    n_chunk = n_pages * (LANE // CMP_STRIDE)
    for cb in range(2 * N_KV_HEADS):
        for p in range(n_pages):
            _scatter_rows(raw, cb, p * (LANE // CMP_STRIDE),
                          pages[p][0, :, cb * HEAD_DIM:(cb + 1) * HEAD_DIM])
    kc, vc = _compress_core(raw, ((w1k_ref, bk_ref, w2k_ref), (w1v_ref, bv_ref, w2v_ref)), n_chunk)
    for g in range(N_KV_HEADS):
        kc_ref[0, g] = kc[g]
        vc_ref[0, g] = vc[g]


def _compress(page_ids, src, col_block, w1k, w1v, bk, bv, w2k, w2v):
    nb, n_pages = page_ids.shape
    n_chunk = n_pages * (LANE // CMP_STRIDE)
    width = 2 * KV_WIDTH

    def page_map(p):
        return lambda b, ids: (ids[b, p], 0, col_block)

    full = lambda shape: pl.BlockSpec(shape, lambda b, ids: (0,) * len(shape))
    in_specs = [pl.BlockSpec((1, LANE, width), page_map(p)) for p in range(n_pages)]
    in_specs += [full(w1k.shape), full(w1v.shape), full(bk.shape), full(bv.shape),
                 full(w2k.shape), full(w2v.shape)]
    out_spec = pl.BlockSpec((1, N_KV_HEADS, n_chunk, HEAD_DIM), lambda b, ids: (b, 0, 0, 0))
    out_sd = jax.ShapeDtypeStruct((nb, N_KV_HEADS, n_chunk, HEAD_DIM), F32)
    return pl.pallas_call(
        functools.partial(_compress_kernel, n_pages=n_pages),
        out_shape=(out_sd, out_sd),
        grid_spec=pltpu.PrefetchScalarGridSpec(
            num_scalar_prefetch=1, grid=(nb,), in_specs=in_specs,
            out_specs=(out_spec, out_spec),
            scratch_shapes=[pltpu.VMEM((2 * N_KV_HEADS, CMP_STRIDE * S_PITCH, HEAD_DIM), F32)]),
        compiler_params=_cparams(("parallel",)),
    )(page_ids, *([src] * n_pages), w1k, w1v, bk, bv, w2k, w2v)


def _nt_dot(a, b):
    return lax.dot_general(a, b, (((1,), (1,)), ((), ())), preferred_element_type=F32)


def _masked_softmax(s, mask):
    s = jnp.where(mask, s, NEG)
    e = jnp.where(mask, jnp.exp(s - jnp.max(s, axis=-1, keepdims=True)), 0.0)
    den = jnp.sum(e, axis=-1, keepdims=True)
    return e / jnp.where(den > 0, den, 1.0)


def _stack_heads(q, tq):
    q = q * (HEAD_DIM ** -0.5)
    return jnp.concatenate([q[:, r * HEAD_DIM:(r + 1) * HEAD_DIM] for r in range(GQA_REP)],
                           axis=0).astype(BF16)


def _row_consts(g, tq, p0):
    rows = GQA_REP * tq
    ridx = lax.broadcasted_iota(I32, (rows, 1), 0)
    slope = jnp.zeros((rows, 1), F32)
    for r in range(GQA_REP):
        slope = jnp.where(ridx // tq == r, SLOPES[g * GQA_REP + r], slope)
    t = p0 + ridx % tq
    return slope, t


def _cmp_branch(q4, kc, vc, slope, t, tq):
    n_chunk = kc.shape[0]
    s = _nt_dot(q4, kc.astype(BF16))
    c_end = lax.broadcasted_iota(I32, (1, n_chunk), 1) * CMP_STRIDE + (CMP_BLOCK - 1)
    rel = t - c_end
    s = s - slope * rel.astype(F32)
    p = _masked_softmax(s, rel >= 0)
    o_c = jnp.dot(p.astype(BF16), vc.astype(BF16), preferred_element_type=F32)
    p_sum = p[0:tq]
    for r in range(1, GQA_REP):
        p_sum = p_sum + p[r * tq:(r + 1) * tq]
    return o_c, p_sum


def _pos_aux(pos):
    pos = np.asarray(pos).reshape(-1, 1)
    lane = np.arange(LANE).reshape(1, -1)
    blk = pos // SEL_BLOCK
    aux = np.where(lane == blk, MASK_M, 0.0)
    aux = np.where(lane == AUX_HI, blk * SEL_BLOCK, aux)
    aux = np.where(lane == AUX_LO, pos % SEL_BLOCK, aux)
    assert blk.max() < AUX_HI
    return jnp.asarray(aux, F32).astype(BF16)


def _query_aux(slope, notsel4):
    lane = lax.broadcasted_iota(I32, (slope.shape[0], LANE), 1)
    base = jnp.zeros((slope.shape[0], LANE), F32) if notsel4 is None else notsel4
    return jnp.where((lane == AUX_HI) | (lane == AUX_LO), slope, base).astype(BF16)


def _not_selected(p_sum, t_row, n_sel):
    tq, n_chunk = p_sum.shape
    nj = -(-n_sel // SUBLANE) * SUBLANE
    cj = lax.broadcasted_iota(I32, (nj, n_chunk), 0) * SEL_BLOCK
    cn = lax.broadcasted_iota(I32, (nj, n_chunk), 1) * CMP_STRIDE
    overlap = jnp.maximum(jnp.minimum(cn + CMP_BLOCK, cj + SEL_BLOCK) - jnp.maximum(cn, cj), 0)
    overlap = overlap.astype(F32).astype(BF16)
    p_hi = p_sum.astype(BF16)
    p_lo = (p_sum - p_hi.astype(F32)).astype(BF16)
    imp = _nt_dot(overlap, p_hi) + _nt_dot(overlap, p_lo)
    row = lax.broadcasted_iota(I32, (nj, tq), 0)
    cur = t_row // SEL_BLOCK
    valid = (row <= cur) & (row < n_sel)
    forced = (row == 0) | (row == cur) | (row == cur - 1)
    score = jnp.where(valid, jnp.where(forced, BIG, imp), -1.0)
    notsel = jnp.ones((nj, tq), F32)
    for _ in range(min(SEL_TOPK, n_sel)):
        m = jnp.max(score, axis=0, keepdims=True)
        idx = jnp.min(jnp.where(score == m, row, nj), axis=0, keepdims=True)
        hit = row == idx
        notsel = jnp.where(hit & (m >= 0.0), 0.0, notsel)
        score = jnp.where(hit, -1.0, score)
    notsel = jnp.concatenate([notsel, jnp.ones((LANE - nj, tq), F32)], axis=0)
    if tq < LANE:
        notsel = jnp.concatenate([notsel, jnp.ones((LANE, LANE - tq), F32)], axis=1)
    return notsel.T


def _rep_rows(x):
    return jnp.concatenate([x] * GQA_REP, axis=0)


def _softmax_rows(s):
    e = jnp.exp(s - jnp.max(s, axis=-1, keepdims=True))
    return e / jnp.sum(e, axis=-1, keepdims=True)


def _gated_sum(sig, g, o_c, o_s, o_w, tq):
    outs = []
    for r in range(GQA_REP):
        h = g * GQA_REP + r
        sl = slice(r * tq, (r + 1) * tq)
        g0 = sig[:, N_BRANCH * h + 0:N_BRANCH * h + 1]
        g1 = sig[:, N_BRANCH * h + 1:N_BRANCH * h + 2]
        g2 = sig[:, N_BRANCH * h + 2:N_BRANCH * h + 3]
        outs.append(g0 * o_c[sl] + g1 * o_s[sl] + g2 * o_w[sl])
    return outs


def _prompt_attn_kernel(q_ref, gate_ref, kc_ref, vc_ref, ks_ref, vs_ref, kw_ref, vw_ref, aux_ref,
                        o_ref, ksb, vsb, kwb, vwb, *, tq, seq_len):
    i = pl.program_id(1)
    n_sel = -(-seq_len // SEL_BLOCK)
    heads = range(N_KV_HEADS)

    @pl.when(i == 0)
    def _():
        for g in heads:
            col = slice(g * HEAD_DIM, (g + 1) * HEAD_DIM)
            ksb[g, :, 0:HEAD_DIM] = ks_ref[:, col].astype(BF16)
            ksb[g, :, HEAD_DIM:] = aux_ref[...]
            kwb[g, :, 0:HEAD_DIM] = kw_ref[:, col].astype(BF16)
            kwb[g, :, HEAD_DIM:] = aux_ref[...]
            vsb[g] = vs_ref[:, col].astype(BF16)
            vwb[g] = vw_ref[:, col].astype(BF16)

    p0 = i * tq
    rows = GQA_REP * tq
    t_row = p0 + lax.broadcasted_iota(I32, (1, tq), 1)
    q_sel, q_win, o_c = [], [], []
    for g in heads:
        q4 = _stack_heads(q_ref[:, g * GQA_REP * HEAD_DIM:(g + 1) * GQA_REP * HEAD_DIM], tq)
        slope, t = _row_consts(g, tq, p0)
        oc, p_sum = _cmp_branch(q4, kc_ref[0, g], vc_ref[0, g], slope, t, tq)
        notsel4 = _rep_rows(_not_selected(p_sum, t_row, n_sel))
        q_sel.append(jnp.concatenate([q4, _query_aux(slope, notsel4)], axis=1))
        q_win.append(jnp.concatenate([q4, _query_aux(slope, None)], axis=1))
        o_c.append(oc)
    _, t = _row_consts(0, tq, p0)

    def step(g, c, carry, causal):
        m, l, acc = carry
        start = pl.multiple_of(c * SEL_CHUNK, SEL_CHUNK)
        s = _nt_dot(q_sel[g], ksb[g, pl.ds(start, SEL_CHUNK), :])
        if causal:
            pos = start + lax.broadcasted_iota(I32, (1, SEL_CHUNK), 1)
            s = jnp.where(pos <= t, s, NEG)
        m_new = jnp.maximum(m, jnp.max(s, axis=-1, keepdims=True))
        a = jnp.exp(m - m_new)
        e = jnp.exp(s - m_new)
        l = a * l + jnp.sum(e, axis=-1, keepdims=True)
        acc = a * acc + jnp.dot(e.astype(BF16), vsb[g, pl.ds(start, SEL_CHUNK), :],
                                preferred_element_type=F32)
        return m_new, l, acc

    def chunk(c, carries, causal):
        return tuple(step(g, c, carries[g], causal) for g in heads)

    last = (p0 + tq - 1) // SEL_CHUNK
    init = tuple((jnp.full((rows, 1), NEG, F32), jnp.zeros((rows, 1), F32),
                  jnp.zeros((rows, HEAD_DIM), F32)) for _ in heads)
    carries = lax.fori_loop(0, last, lambda c, cr: chunk(c, cr, False), init)
    carries = chunk(last, carries, True)

    n_win = WINDOW + tq
    start = pl.multiple_of(jnp.maximum(p0 - WINDOW, 0), tq)
    rel = ((p0 - start) + lax.broadcasted_iota(I32, (tq, n_win), 0)
           - lax.broadcasted_iota(I32, (tq, n_win), 1))
    band = _rep_rows(jnp.where((rel >= 0) & (rel < WINDOW), 0.0, NEG))
    sig = jax.nn.sigmoid(gate_ref[...])
    for g in heads:
        _, l, acc = carries[g]
        o_s = acc / l
        s = _nt_dot(q_win[g], kwb[g, pl.ds(start, n_win), :]) + band
        o_w = jnp.dot(_softmax_rows(s).astype(BF16), vwb[g, pl.ds(start, n_win), :],
                      preferred_element_type=F32)
        outs = _gated_sum(sig, g, o_c[g], o_s, o_w, tq)
        for r in range(GQA_REP):
            h = g * GQA_REP + r
            o_ref[:, h * HEAD_DIM:(h + 1) * HEAD_DIM] = outs[r]


def _prompt_attention(proj, kc, vc, n_batch, seq_len, n_cols_blocks):
    tq = Q_BLOCK
    nqb = seq_len // tq
    gate_cb = n_cols_blocks - 1
    aux = _pos_aux(np.arange(seq_len))
    kv_spec = lambda cb: pl.BlockSpec((seq_len, KV_WIDTH), lambda b, i: (b, cb * LANE // KV_WIDTH))
    c_spec = pl.BlockSpec((1,) + kc.shape[1:], lambda b, i: (b, 0, 0, 0))
    bf = lambda width: pltpu.VMEM((N_KV_HEADS, seq_len, width), BF16)
    return pl.pallas_call(
        functools.partial(_prompt_attn_kernel, tq=tq, seq_len=seq_len),
        out_shape=jax.ShapeDtypeStruct((n_batch * seq_len, ATTN_WIDTH), F32),
        grid=(n_batch, nqb),
        in_specs=[pl.BlockSpec((tq, ATTN_WIDTH), lambda b, i: (b * nqb + i, 0)),
                  pl.BlockSpec((tq, LANE), lambda b, i: (b * nqb + i, gate_cb)),
                  c_spec, c_spec,
                  kv_spec(CB_KV + 2 * N_KV_HEADS), kv_spec(CB_KV + 3 * N_KV_HEADS),
                  kv_spec(CB_WIN), kv_spec(CB_WIN + N_KV_HEADS),
                  pl.BlockSpec(aux.shape, lambda b, i: (0, 0))],
        out_specs=pl.BlockSpec((tq, ATTN_WIDTH), lambda b, i: (b * nqb + i, 0)),
        scratch_shapes=[bf(2 * HEAD_DIM), bf(HEAD_DIM), bf(2 * HEAD_DIM), bf(HEAD_DIM)],
        compiler_params=_cparams(("parallel", "arbitrary")),
    )(proj, proj, kc, vc, proj, proj, proj, proj, aux)


def _sample_attn_kernel(pt_ref, *refs, n_pages, tq, past_len, win_buf):
    pages = refs[:n_pages]
    (q_ref, kvn_ref, winn_ref, gate_ref, win_ref, w1k_ref, w1v_ref, bk_ref, bv_ref, w2k_ref,
     w2v_ref, saux_ref, waux_ref, o_ref, kvo_ref, wino_ref, raw) = refs[n_pages:]
    seq_len = past_len + tq
    n_sel = -(-seq_len // SEL_BLOCK)
    n_planes = N_KV_COMP * N_KV_HEADS
    n_wplanes = 2 * N_KV_HEADS
    page = pages[0].shape[1] // n_planes
    n_chunk = past_len // CMP_STRIDE
    zpad = jnp.zeros((LANE - tq, HEAD_DIM), BF16)

    for c in range(n_planes):
        kvo_ref[pl.ds(c, tq, stride=n_planes), :] = kvn_ref[:, c * HEAD_DIM:(c + 1) * HEAD_DIM]
    keep = (win_buf - tq) * n_wplanes
    wino_ref[0, 0:keep, :] = win_ref[0, tq * n_wplanes:win_buf * n_wplanes, :]
    for c in range(n_wplanes):
        wino_ref[0, pl.ds(keep + c, tq, stride=n_wplanes), :] = winn_ref[:, c * HEAD_DIM:(c + 1) * HEAD_DIM]

    for cb in range(2 * N_KV_HEADS):
        for p in range(n_pages):
            _scatter_rows(raw, cb, p * (page // CMP_STRIDE),
                          pages[p][0, pl.ds(cb, page, stride=n_planes), :])
    kc, vc = _compress_core(raw, ((w1k_ref, bk_ref, w2k_ref), (w1v_ref, bv_ref, w2v_ref)), n_chunk)

    sig = jax.nn.sigmoid(gate_ref[...])
    q4s, consts, o_cs, p_sums = [], [], [], []
    for g in range(N_KV_HEADS):
        q4 = _stack_heads(q_ref[:, g * GQA_REP * HEAD_DIM:(g + 1) * GQA_REP * HEAD_DIM], tq)
        slope, t = _row_consts(g, tq, past_len)
        o_c, p_sum = _cmp_branch(q4, kc[g], vc[g], slope, t, tq)
        q4s.append(q4); consts.append((slope, t)); o_cs.append(o_c); p_sums.append(p_sum)
    t_row = past_len + lax.broadcasted_iota(I32, (1, N_KV_HEADS * tq), 1) % tq
    notsel = _not_selected(jnp.concatenate(p_sums, axis=0), t_row, n_sel)

    for g in range(N_KV_HEADS):
        q4, (slope, t), o_c = q4s[g], consts[g], o_cs[g]
        q_sel = jnp.concatenate([q4, _query_aux(slope, _rep_rows(notsel[g * tq:(g + 1) * tq]))], axis=1)
        q_win = jnp.concatenate([q4, _query_aux(slope, None)], axis=1)

        kpl = 2 * N_KV_HEADS + g
        vpl = 3 * N_KV_HEADS + g
        k = jnp.concatenate(
            [p[0, pl.ds(kpl, page, stride=n_planes), :].astype(BF16) for p in pages]
            + [kvn_ref[:, kpl * HEAD_DIM:(kpl + 1) * HEAD_DIM].astype(BF16), zpad], axis=0)
        v = jnp.concatenate(
            [p[0, pl.ds(vpl, page, stride=n_planes), :].astype(BF16) for p in pages]
            + [kvn_ref[:, vpl * HEAD_DIM:(vpl + 1) * HEAD_DIM].astype(BF16), zpad], axis=0)
        pos = lax.broadcasted_iota(I32, (1, k.shape[0]), 1)
        s = _nt_dot(q_sel, jnp.concatenate([k, saux_ref[...]], axis=1))
        p = _softmax_rows(jnp.where(pos <= t, s, NEG))
        o_s = jnp.dot(p.astype(BF16), v, preferred_element_type=F32)

        kw = jnp.concatenate([win_ref[0, pl.ds(g, win_buf, stride=n_wplanes), :].astype(BF16),
                              winn_ref[:, g * HEAD_DIM:(g + 1) * HEAD_DIM].astype(BF16), zpad],
                             axis=0)
        vp = N_KV_HEADS + g
        vw = jnp.concatenate([win_ref[0, pl.ds(vp, win_buf, stride=n_wplanes), :].astype(BF16),
                              winn_ref[:, vp * HEAD_DIM:(vp + 1) * HEAD_DIM].astype(BF16), zpad],
                             axis=0)
        kpos = (past_len - win_buf) + lax.broadcasted_iota(I32, (1, kw.shape[0]), 1)
        rel = t - kpos
        s = _nt_dot(q_win, jnp.concatenate([kw, waux_ref[...]], axis=1))
        p = _softmax_rows(jnp.where((rel >= 0) & (rel < WINDOW), s, NEG))
        o_w = jnp.dot(p.astype(BF16), vw, preferred_element_type=F32)

        outs = _gated_sum(sig, g, o_c, o_s, o_w, tq)
        for r in range(GQA_REP):
            h = g * GQA_REP + r
            o_ref[:, h * HEAD_DIM:(h + 1) * HEAD_DIM] = outs[r]


def _sample_attention(page_table, cache, proj, state_win, cmp_w, n_prompt_rows, tq, n_cols_blocks):
    db, n_pages = page_table.shape
    n_planes = N_KV_COMP * N_KV_HEADS
    n_wplanes = 2 * N_KV_HEADS
    page = cache.shape[1] // n_planes
    past_len = n_pages * page
    win_buf = state_win.shape[1] // n_wplanes
    r0 = n_prompt_rows // tq
    gate_cb = n_cols_blocks - 1
    sel_aux = _pos_aux(np.arange(past_len + LANE))
    win_aux = _pos_aux(past_len - win_buf + np.arange(win_buf + LANE))

    def page_map(p):
        return lambda b, pt: (pt[b, p], 0, 0)

    full = lambda a: pl.BlockSpec(a.shape, lambda b, pt: (0,) * a.ndim)
    in_specs = [pl.BlockSpec((1,) + cache.shape[1:], page_map(p)) for p in range(n_pages)]
    in_specs += [
        pl.BlockSpec((tq, ATTN_WIDTH), lambda b, pt: (r0 + b, 0)),
        pl.BlockSpec((tq, N_KV_COMP * KV_WIDTH), lambda b, pt: (r0 + b, 1)),
        pl.BlockSpec((tq, 2 * KV_WIDTH), lambda b, pt: (r0 + b, CB_WIN * LANE // (2 * KV_WIDTH))),
        pl.BlockSpec((tq, LANE), lambda b, pt: (r0 + b, gate_cb)),
        pl.BlockSpec((1,) + state_win.shape[1:], lambda b, pt: (b, 0, 0)),
    ] + [full(w) for w in cmp_w] + [full(sel_aux), full(win_aux)]
    return pl.pallas_call(
        functools.partial(_sample_attn_kernel, n_pages=n_pages, tq=tq, past_len=past_len,
                          win_buf=win_buf),
        out_shape=(jax.ShapeDtypeStruct((db * tq, ATTN_WIDTH), F32),
                   jax.ShapeDtypeStruct((db * tq * n_planes, HEAD_DIM), F32),
                   jax.ShapeDtypeStruct(state_win.shape, F32)),
        grid_spec=pltpu.PrefetchScalarGridSpec(
            num_scalar_prefetch=1, grid=(db,), in_specs=in_specs,
            out_specs=(pl.BlockSpec((tq, ATTN_WIDTH), lambda b, pt: (b, 0)),
                       pl.BlockSpec((tq * n_planes, HEAD_DIM), lambda b, pt: (b, 0)),
                       pl.BlockSpec((1,) + state_win.shape[1:], lambda b, pt: (b, 0, 0))),
            scratch_shapes=[pltpu.VMEM((2 * N_KV_HEADS, CMP_STRIDE * S_PITCH, HEAD_DIM), F32)]),
        compiler_params=_cparams(("parallel",)),
    )(page_table, *([cache] * n_pages), proj, proj, proj, proj, state_win, *cmp_w, sel_aux, win_aux)


def _interleave_kernel(x_ref, o_ref, *, n_planes):
    rows = x_ref.shape[0]
    for c in range(n_planes):
        o_ref[pl.ds(c, rows, stride=n_planes), :] = x_ref[:, c * LANE:(c + 1) * LANE]


def _interleave(proj, n_blocks, block_rows, n_planes, row_block_of, col_block):
    return pl.pallas_call(
        functools.partial(_interleave_kernel, n_planes=n_planes),
        out_shape=jax.ShapeDtypeStruct((n_blocks * block_rows * n_planes, LANE), F32),
        grid=(n_blocks,),
        in_specs=[pl.BlockSpec((block_rows, n_planes * LANE), lambda i: (row_block_of(i), col_block))],
        out_specs=pl.BlockSpec((block_rows * n_planes, LANE), lambda i: (i, 0)),
        compiler_params=_cparams(("parallel",)),
    )(proj)


def _conv_kernel(*refs, seq_len, has_state):
    if has_state:
        b_ref, c_ref, h_ref, w_ref, s1_ref, s2_ref, y_ref, u_ref = refs
    else:
        b_ref, c_ref, h_ref, w_ref, y_ref, u_ref = refs
    u = c_ref[...] * h_ref[...]
    rows = u.shape[0]
    t = lax.broadcasted_iota(I32, (rows, 1), 0) % seq_len
    p1 = jnp.where(t >= 1, pltpu.roll(u, 1, 0), 0.0)
    p2 = jnp.where(t >= 2, pltpu.roll(u, 2, 0), 0.0)
    if has_state:
        p1 = p1 + s1_ref[...]
        p2 = p2 + s2_ref[...]
    y = w_ref[0:1, :] * p2
    y = y + w_ref[1:2, :] * p1
    y = y + w_ref[2:3, :] * u
    y_ref[...] = b_ref[...] * y
    u_ref[...] = u[rows - u_ref.shape[0]:, :]


def _short_conv(proj, conv_w, row_block0, n_blocks, block_rows, seq_len, state=None):
    conv_dim = conv_w.shape[1]
    n_ct = conv_dim // LANE
    tail = SUBLANE if state is None else block_rows
    spec = lambda cb: pl.BlockSpec((block_rows, LANE), lambda r, c: (row_block0 + r, cb + c))
    in_specs = [spec(CB_CONV), spec(CB_CONV + n_ct), spec(CB_CONV + 2 * n_ct),
                pl.BlockSpec((CONV_WIDTH, LANE), lambda r, c: (0, c))]
    args = [proj, proj, proj, conv_w]
    if state is not None:
        st_spec = pl.BlockSpec((block_rows, LANE), lambda r, c: (r, c))
        in_specs += [st_spec, st_spec]
        args += list(state)
    return pl.pallas_call(
        functools.partial(_conv_kernel, seq_len=seq_len, has_state=state is not None),
        out_shape=(jax.ShapeDtypeStruct((n_blocks * block_rows, conv_dim), F32),
                   jax.ShapeDtypeStruct((n_blocks * tail, conv_dim), F32)),
        grid=(n_blocks, n_ct),
        in_specs=in_specs,
        out_specs=(pl.BlockSpec((block_rows, LANE), lambda r, c: (r, c)),
                   pl.BlockSpec((tail, LANE), lambda r, c: (r, c))),
        compiler_params=_cparams(("parallel", "parallel")),
    )(*args)


def _layer_norm(x, g, b):
    mu = jnp.mean(x, axis=-1, keepdims=True)
    var = jnp.mean(jnp.square(x - mu), axis=-1, keepdims=True)
    return (x - mu) * lax.rsqrt(var + LN_EPS) * g + b


def _mix_kernel(oa_p, oa_s, oc_p, oc_s, x_p, x_s, wa_ref, wc_ref, g_ref, b_ref, wr_ref, br_ref,
                x1_ref, e_ref, gt_ref, *, dn_alpha, n_experts, n_first):
    o_attn = _pair_load(oa_p, oa_s, n_first)
    o_conv = _pair_load(oc_p, oc_s, n_first)
    mix = jnp.dot(o_attn.astype(BF16), wa_ref[...], preferred_element_type=F32)
    mix = mix + jnp.dot(o_conv.astype(BF16), wc_ref[...], preferred_element_type=F32)
    x1 = _layer_norm(dn_alpha * _pair_load(x_p, x_s, n_first) + mix, g_ref[...], b_ref[...])
    x1_ref[...] = x1
    x_hi = x1.astype(BF16)
    x_lo = (x1 - x_hi.astype(F32)).astype(BF16)
    w_r = wr_ref[...]
    w_hi = w_r.astype(BF16)
    w_lo = (w_r - w_hi.astype(F32)).astype(BF16)
    logits = (jnp.dot(x_hi, w_hi, preferred_element_type=F32)
              + jnp.dot(x_lo, w_hi, preferred_element_type=F32)
              + jnp.dot(x_hi, w_lo, preferred_element_type=F32)) + br_ref[...]
    tm = logits.shape[0]
    lane = lax.broadcasted_iota(I32, (tm, LANE), 1)
    score = jnp.where(lane < n_experts, logits, -jnp.inf)
    e_out = jnp.zeros((tm, LANE), I32)
    v_out = jnp.zeros((tm, LANE), F32)
    v0 = None
    den = jnp.zeros((tm, 1), F32)
    for k in range(TOP_K):
        m = jnp.max(score, axis=-1, keepdims=True)
        idx = jnp.min(jnp.where(score == m, lane, LANE), axis=-1, keepdims=True)
        if k == 0:
            v0 = m
        ex = jnp.exp(m - v0)
        den = den + ex
        e_out = jnp.where(lane == k, idx, e_out)
        v_out = jnp.where(lane == k, ex, v_out)
        score = jnp.where(lane == idx, -jnp.inf, score)
    e_ref[...] = e_out
    gt_ref[...] = v_out / den


def _mix_ln_router(o_attn, o_conv, x, w_out_a, w_out_c, ln_g, ln_b, w_r, b_r, dn_alpha, n_experts):
    n = x[0].shape[0] + x[1].shape[0]
    d = x[0].shape[1]
    tm = _pick(x[1].shape[0], 256)
    n_first = x[0].shape[0] // tm
    row = lambda w: pl.BlockSpec((tm, w), lambda i: (i, 0))
    full = lambda a: pl.BlockSpec(a.shape, lambda i: (0, 0))
    pairs = lambda p: _pair_specs(tm, p[0].shape[1], n_first)
    return pl.pallas_call(
        functools.partial(_mix_kernel, dn_alpha=dn_alpha, n_experts=n_experts, n_first=n_first),
        out_shape=(jax.ShapeDtypeStruct((n, d), F32),
                   jax.ShapeDtypeStruct((n, LANE), I32), jax.ShapeDtypeStruct((n, LANE), F32)),
        grid=(n // tm,),
        in_specs=pairs(o_attn) + pairs(o_conv) + pairs(x) + [
            full(w_out_a), full(w_out_c), full(ln_g), full(ln_b), full(w_r), full(b_r)],
        out_specs=(row(d), row(LANE), row(LANE)),
        compiler_params=_cparams(("parallel",)),
    )(*o_attn, *o_conv, *x, w_out_a, w_out_c, ln_g, ln_b, w_r, b_r)


def _weight_pipeline(f, m, nf, chg_ref, nxt_ref, wrap_ref, cnt_ref, copies, cast):
    @pl.when((f == 0) & (m == 0))
    def _():
        cnt_ref[0] = 0
        for c in copies(nxt_ref[0], 0, 0):
            c.start()

    @pl.when(chg_ref[m] == 1)
    def _():
        slot = cnt_ref[0] % 2
        f_next = f + wrap_ref[m]

        @pl.when(f_next < nf)
        def _():
            for c in copies(nxt_ref[m + 1], f_next, 1 - slot):
                c.start()

        for c in copies(0, 0, slot):
            c.wait()
        cast(slot)
        cnt_ref[0] = cnt_ref[0] + 1


def _gate_up_kernel(be_ref, br_ref, chg_ref, nxt_ref, wrap_ref, x_ref, w_hbm, bg_ref, bu_ref, a_ref,
                    wbuf, wgb, wub, sem, cnt_ref, *, nf, tf):
    f = pl.program_id(0)
    m = pl.program_id(1)
    rows = br_ref[m]
    half = MOE_TM // 2

    def copies(e, fi, slot):
        return [pltpu.make_async_copy(
            w_hbm.at[e, :, pl.ds(pl.multiple_of((part * nf + fi) * tf, tf), tf)],
            wbuf.at[slot, part], sem.at[slot, part]) for part in range(2)]

    def cast(slot):
        wgb[...] = wbuf[slot, 0].astype(BF16)
        wub[...] = wbuf[slot, 1].astype(BF16)

    _weight_pipeline(f, m, nf, chg_ref, nxt_ref, wrap_ref, cnt_ref, copies, cast)

    def act(x):
        hg = jnp.dot(x, wgb[...], preferred_element_type=F32) + bg_ref[0]
        hu = jnp.dot(x, wub[...], preferred_element_type=F32) + bu_ref[0]
        hg = jnp.minimum(hg, SWIGLU_LIMIT)
        hu = jnp.clip(hu, -SWIGLU_LIMIT, SWIGLU_LIMIT)
        return (hg * jax.nn.sigmoid(SWIGLU_ALPHA * hg) * (hu + 1.0)).astype(BF16)

    @pl.when(rows > half)
    def _():
        a_ref[...] = act(x_ref[...].astype(BF16))

    @pl.when((rows > 0) & (rows <= half))
    def _():
        a_ref[0:half, :] = act(x_ref[0:half, :].astype(BF16))
        a_ref[half:, :] = jnp.zeros((MOE_TM - half, a_ref.shape[1]), BF16)

    @pl.when(rows == 0)
    def _():
        a_ref[...] = jnp.zeros_like(a_ref)


def _down_kernel(be_ref, br_ref, chg_ref, nxt_ref, wrap_ref, a_ref, w_hbm, b_ref, y_ref,
                 wbuf, wb, sem, cnt_ref, *, nj, tn):
    j = pl.program_id(0)
    m = pl.program_id(1)
    rows = br_ref[m]
    half = MOE_TM // 2

    def copies(e, ji, slot):
        return [pltpu.make_async_copy(
            w_hbm.at[e, :, pl.ds(pl.multiple_of(ji * tn, tn), tn)], wbuf.at[slot], sem.at[slot])]

    def cast(slot):
        wb[...] = wbuf[slot].astype(BF16)

    _weight_pipeline(j, m, nj, chg_ref, nxt_ref, wrap_ref, cnt_ref, copies, cast)

    @pl.when(rows > half)
    def _():
        y_ref[...] = jnp.dot(a_ref[...], wb[...], preferred_element_type=F32) + b_ref[0]

    @pl.when((rows > 0) & (rows <= half))
    def _():
        y_ref[0:half, :] = jnp.dot(a_ref[0:half, :], wb[...], preferred_element_type=F32) + b_ref[0]
        y_ref[half:, :] = jnp.zeros((MOE_TM - half, y_ref.shape[1]), F32)

    @pl.when(rows == 0)
    def _():
        y_ref[...] = jnp.zeros_like(y_ref)


def _experts(plan, xr, w_gu, b_gu, w_down, b_down):
    block_e, block_rows, chg, nxt, wrap = plan
    rows, d = xr.shape
    d_ff = w_down.shape[1]
    n_blocks = rows // MOE_TM
    tf = _pick(d_ff, 1024, LANE)
    nf = d_ff // tf
    a = pl.pallas_call(
        functools.partial(_gate_up_kernel, nf=nf, tf=tf),
        out_shape=jax.ShapeDtypeStruct((rows, d_ff), BF16),
        grid_spec=pltpu.PrefetchScalarGridSpec(
            num_scalar_prefetch=5, grid=(nf, n_blocks),
            in_specs=[pl.BlockSpec((MOE_TM, d), lambda f, m, be, *_: (m, 0)),
                      pl.BlockSpec(memory_space=pl.ANY),
                      pl.BlockSpec((1, 1, tf), lambda f, m, be, *_: (be[m], 0, f)),
                      pl.BlockSpec((1, 1, tf), lambda f, m, be, *_: (be[m], 0, nf + f))],
            out_specs=pl.BlockSpec((MOE_TM, tf), lambda f, m, be, *_: (m, f)),
            scratch_shapes=[pltpu.VMEM((2, 2, d, tf), F32), pltpu.VMEM((d, tf), BF16),
                            pltpu.VMEM((d, tf), BF16), pltpu.SemaphoreType.DMA((2, 2)),
                            pltpu.SMEM((1,), I32)]),
        compiler_params=_cparams(("arbitrary", "arbitrary")),
    )(block_e, block_rows, chg, nxt, wrap, xr, w_gu, b_gu, b_gu)
    tn = _pick(d, 1024, LANE)
    nj = d // tn
    return pl.pallas_call(
        functools.partial(_down_kernel, nj=nj, tn=tn),
        out_shape=jax.ShapeDtypeStruct((rows, d), F32),
        grid_spec=pltpu.PrefetchScalarGridSpec(
            num_scalar_prefetch=5, grid=(nj, n_blocks),
            in_specs=[pl.BlockSpec((MOE_TM, d_ff), lambda j, m, be, *_: (m, 0)),
                      pl.BlockSpec(memory_space=pl.ANY),
                      pl.BlockSpec((1, 1, tn), lambda j, m, be, *_: (be[m], 0, j))],
            out_specs=pl.BlockSpec((MOE_TM, tn), lambda j, m, be, *_: (m, j)),
            scratch_shapes=[pltpu.VMEM((2, d_ff, tn), F32), pltpu.VMEM((d_ff, tn), BF16),
                            pltpu.SemaphoreType.DMA((2,)), pltpu.SMEM((1,), I32)]),
        compiler_params=_cparams(("arbitrary", "arbitrary")),
    )(block_e, block_rows, chg, nxt, wrap, a, w_down, b_down)


def _combine_kernel(x1_ref, y_ref, gt_ref, g_ref, b_ref, o_ref, *, dn_alpha):
    gt = gt_ref[...]
    moe = y_ref[0] * gt[:, 0:1]
    for k in range(1, TOP_K):
        moe = moe + y_ref[k] * gt[:, k:k + 1]
    o_ref[...] = _layer_norm(dn_alpha * x1_ref[...] + moe, g_ref[...], b_ref[...])


def _combine_ln(x1, yk, gates, ln_g, ln_b, dn_alpha, row0, n):
    d = x1.shape[1]
    tm = _pick(n, 256)
    r0 = row0 // tm
    return pl.pallas_call(
        functools.partial(_combine_kernel, dn_alpha=dn_alpha),
        out_shape=jax.ShapeDtypeStruct((n, d), F32),
        grid=(n // tm,),
        in_specs=[pl.BlockSpec((tm, d), lambda i: (r0 + i, 0)),
                  pl.BlockSpec((TOP_K, tm, d), lambda i: (0, r0 + i, 0)),
                  pl.BlockSpec((tm, LANE), lambda i: (r0 + i, 0)),
                  pl.BlockSpec((1, d), lambda i: (0, 0)),
                  pl.BlockSpec((1, d), lambda i: (0, 0))],
        out_specs=pl.BlockSpec((tm, d), lambda i: (i, 0)),
        compiler_params=_cparams(("parallel",)),
    )(x1, yk, gates, ln_g, ln_b)


def _route(top_e, n_experts):
    n = top_e.shape[0]
    n4 = n * TOP_K
    e_flat = top_e.reshape(-1)
    entry = jnp.arange(n4, dtype=I32)
    _, order = lax.sort((e_flat, entry), num_keys=1, is_stable=True)
    experts = jnp.arange(n_experts, dtype=I32)
    counts = jnp.sum((e_flat[:, None] == experts[None, :]).astype(I32), axis=0)
    ends = jnp.cumsum(counts)
    starts = ends - counts
    padded = (counts + MOE_TM - 1) // MOE_TM * MOE_TM
    pends = jnp.cumsum(padded)
    pstarts = pends - padded
    dest_s = entry + jnp.sum(jnp.where(entry[:, None] >= ends[None, :],
                                       (padded - counts)[None, :], 0), axis=1)
    _, pos = lax.sort((order, dest_s), num_keys=1)
    n_blocks = -(-n4 // MOE_TM) + n_experts
    block0 = jnp.arange(n_blocks, dtype=I32) * MOE_TM
    block_e = jnp.minimum(jnp.sum((pends[None, :] <= block0[:, None]).astype(I32), axis=1),
                          n_experts - 1)
    block_rows = jnp.clip(counts[block_e] - (block0 - pstarts[block_e]), 0, MOE_TM)
    in_block = jnp.arange(MOE_TM, dtype=I32)[None, :]
    src = (block0 + starts[block_e] - pstarts[block_e])[:, None] + in_block
    row_tok = jnp.where(in_block < block_rows[:, None],
                        order[jnp.clip(src, 0, n4 - 1)] // TOP_K, 0).reshape(-1)
    blk = jnp.arange(n_blocks, dtype=I32)
    prev_e = jnp.concatenate([block_e[:1] - 1, block_e[:-1]])
    chg = (block_rows > 0) & (block_e != prev_e)
    later = jnp.where(chg, blk, n_blocks)
    nxt_idx = lax.cummin(jnp.concatenate([later[1:], later[:1] * 0 + n_blocks]), reverse=True)
    wrap = nxt_idx >= n_blocks
    nxt_e = block_e[jnp.where(wrap, 0, nxt_idx)]
    nxt = jnp.concatenate([block_e[:1], nxt_e])
    plan = (block_e, block_rows.astype(I32), chg.astype(I32), nxt.astype(I32), wrap.astype(I32))
    return row_tok, pos.reshape(n, TOP_K), plan


def _layer(x_prompt, x_sample, cache, state_win, state_conv, page_table,
           w_in, w_out, conv_w, cmp_k_pe, cmp_k_w1, cmp_k_w2, cmp_v_pe, cmp_v_w1, cmp_v_w2,
           ln1_g, ln1_b, w_router, b_router, w_gu, b_gu, w_down, b_down, ln2_g, ln2_b, depth):
    nb, seq_len, d = x_prompt.shape
    db, dec_seq, _ = x_sample.shape
    n_planes = N_KV_COMP * N_KV_HEADS
    n_wplanes = 2 * N_KV_HEADS
    page = cache.shape[1] // n_planes
    n_pages = page_table.shape[1]
    past_len = n_pages * page
    win_buf = state_win.shape[1] // n_wplanes
    conv_dim = conv_w.shape[1]
    n_experts = w_router.shape[1]
    n_p, n_s = nb * seq_len, db * dec_seq
    dn_alpha = (2.0 * depth) ** 0.25
    assert page == LANE and dec_seq == SUBLANE and seq_len % Q_BLOCK == 0
    assert seq_len >= WINDOW + Q_BLOCK and win_buf == WINDOW and seq_len % win_buf == 0
    assert n_p % n_s == 0 and conv_dim % LANE == 0 and n_experts <= LANE
    assert past_len + dec_seq <= 2 * 32 * SEL_BLOCK and seq_len <= 32 * SEL_BLOCK

    off_gate = ATTN_WIDTH + N_KV_COMP * KV_WIDTH + 2 * KV_WIDTH
    off_conv = off_gate + N_GATE
    w_p = jnp.concatenate([w_in[:, :off_gate], w_in[:, off_conv:], w_in[:, off_gate:off_conv],
                           jnp.zeros((d, LANE - N_GATE), w_in.dtype)], axis=1).astype(BF16)
    n_cb = w_p.shape[1] // LANE
    xp, xs = x_prompt.reshape(n_p, d), x_sample.reshape(n_s, d)
    n_tok = n_p + n_s
    proj = _proj(xp, xs, w_p, _pick(n_s, 512), _pick(w_p.shape[1], 1152, LANE))

    kv_cb = ATTN_WIDTH // (n_planes * LANE)
    tb = _pick(n_p, 256)
    kv_prompt = _interleave(proj, n_p // tb, tb, n_planes, lambda i: i, kv_cb)
    win_cb = (ATTN_WIDTH + n_planes * LANE) // (n_wplanes * LANE)
    per_seq = seq_len // win_buf
    win_prompt = _interleave(proj, nb, win_buf, n_wplanes,
                             lambda i: i * per_seq + per_seq - 1, win_cb)

    w1k = jnp.concatenate(jnp.split(cmp_k_w1, CMP_RATIO, axis=0), axis=1).astype(BF16)
    w1v = jnp.concatenate(jnp.split(cmp_v_w1, CMP_RATIO, axis=0), axis=1).astype(BF16)
    bk = _pe_bias(cmp_k_pe, cmp_k_w1.astype(BF16))
    bv = _pe_bias(cmp_v_pe, cmp_v_w1.astype(BF16))
    cmp_w = (w1k, w1v, bk, bv, cmp_k_w2.astype(BF16), cmp_v_w2.astype(BF16))
    pp = seq_len // LANE
    prompt_ids = jnp.arange(nb * pp, dtype=I32).reshape(nb, pp)
    proj_pages = proj.reshape(n_tok // LANE, LANE, n_cb * LANE)
    kc_p, vc_p = _compress(prompt_ids, proj_pages, ATTN_WIDTH // (2 * KV_WIDTH), *cmp_w)

    o_p = _prompt_attention(proj, kc_p, vc_p, nb, seq_len, n_cb)
    o_s, kv_sample, win_sample = _sample_attention(page_table, cache, proj, state_win, cmp_w,
                                                   n_p, dec_seq, n_cb)

    c_p, u_p = _short_conv(proj, conv_w, 0, nb, seq_len, seq_len)
    z = jnp.zeros((db, dec_seq - 2, conv_dim), F32)
    s1 = jnp.concatenate([state_conv[:, 1:2], z, z[:, :1]], axis=1).reshape(n_s, conv_dim)
    s2 = jnp.concatenate([state_conv, z], axis=1).reshape(n_s, conv_dim)
    c_s, u_s = _short_conv(proj, conv_w, n_p // n_s, 1, n_s, dec_seq, state=(s1, s2))
    conv_prompt = u_p.reshape(nb, SUBLANE, conv_dim)[:, SUBLANE - (CONV_WIDTH - 1):]
    conv_sample = u_s.reshape(db, dec_seq, conv_dim)[:, dec_seq - (CONV_WIDTH - 1):]

    w_r = jnp.pad(w_router, ((0, 0), (0, LANE - n_experts)))
    b_r = jnp.pad(b_router, (0, LANE - n_experts)).reshape(1, LANE)
    x1, e_idx, gates = _mix_ln_router(
        (o_p, o_s), (c_p, c_s), (xp, xs), w_out[:ATTN_WIDTH].astype(BF16),
        w_out[ATTN_WIDTH:].astype(BF16), ln1_g.reshape(1, d), ln1_b.reshape(1, d), w_r, b_r,
        dn_alpha, n_experts)

    row_tok, pos, plan = _route(e_idx[:, :TOP_K], n_experts)
    y_rows = _experts(plan, x1[row_tok], w_gu, b_gu.reshape(n_experts, 1, -1),
                      w_down, b_down.reshape(n_experts, 1, -1))

    comb = (x1, y_rows[pos.T], gates, ln2_g.reshape(1, d), ln2_b.reshape(1, d), dn_alpha)
    y_prompt = _combine_ln(*comb, 0, n_p).reshape(nb, seq_len, d)
    y_sample = _combine_ln(*comb, n_p, n_s).reshape(db, dec_seq, d)
    kv_shape = (N_KV_COMP, N_KV_HEADS, HEAD_DIM)
    win_shape = (win_buf, 2, N_KV_HEADS, HEAD_DIM)
    return (y_prompt, y_sample,
            kv_prompt.reshape((nb, seq_len) + kv_shape), kv_sample.reshape((db, dec_seq) + kv_shape),
            win_prompt.reshape((nb,) + win_shape), win_sample.reshape((db,) + win_shape),
            conv_prompt, conv_sample)


def kernel(x_prompt, x_sample, cache_kv, state_win, state_conv, page_table, w_in, w_out, conv_w, cmp_k_pe, cmp_k_w1, cmp_k_w2, cmp_v_pe, cmp_v_w1, cmp_v_w2, ln1_g, ln1_b, w_router, b_router, w_gu, b_gu, w_down, b_down, ln2_g, ln2_b):
    depth, n_pool, page = cache_kv.shape[:3]
    db, win_buf = state_win.shape[1:3]
    cache_rows = cache_kv.reshape(depth * n_pool, page * N_KV_COMP * N_KV_HEADS, HEAD_DIM)
    win_rows = state_win.reshape(depth, db, win_buf * 2 * N_KV_HEADS, HEAD_DIM)
    y_prompt, y_sample = x_prompt, x_sample
    outs = [[] for _ in range(6)]
    for layer in range(depth):
        res = _layer(y_prompt, y_sample, cache_rows, win_rows[layer], state_conv[layer],
                     page_table + layer * n_pool,
                     w_in[layer], w_out[layer], conv_w[layer],
                     cmp_k_pe[layer], cmp_k_w1[layer], cmp_k_w2[layer],
                     cmp_v_pe[layer], cmp_v_w1[layer], cmp_v_w2[layer],
                     ln1_g[layer], ln1_b[layer], w_router[layer], b_router[layer],
                     w_gu[layer], b_gu[layer], w_down[layer], b_down[layer],
                     ln2_g[layer], ln2_b[layer], depth)
        y_prompt, y_sample = res[0], res[1]
        for lst, val in zip(outs, res[2:]):
            lst.append(val)
    return (y_prompt, y_sample) + tuple(jnp.stack(o) for o in outs)
```

```python
import functools
import math

import numpy as np
import jax
import jax.numpy as jnp
from jax import lax
from jax.experimental import pallas as pl
from jax.experimental.pallas import tpu as pltpu

F32 = jnp.float32
BF16 = jnp.bfloat16
I32 = jnp.int32

HEAD_DIM = 128
N_HEADS = 8
N_KV_HEADS = 2
GQA_REP = N_HEADS // N_KV_HEADS
N_KV_COMP = 4
KV_WIDTH = N_KV_HEADS * HEAD_DIM
ATTN_WIDTH = N_HEADS * HEAD_DIM
CONV_WIDTH = 3
CMP_BLOCK = 32
CMP_STRIDE = 16
CMP_RATIO = CMP_BLOCK // CMP_STRIDE
CMP_HIDDEN = 256
SEL_BLOCK = 64
SEL_TOPK = 8
WINDOW = 512
N_BRANCH = 3
Q_BLOCK = 128
TOP_K = 4
SWIGLU_LIMIT = 7.0
SWIGLU_ALPHA = 1.702
MOE_TM = 256
LN_EPS = 1e-5
NEG = -1e30
BIG = 1e30
LANE = 128
SUBLANE = 8
VMEM_LIMIT = 56 * 1024 * 1024
SEL_CHUNK = 512
SAMPLE_SUB = 2
MASK_M = -2.0 ** 100
AUX_HI = 64
AUX_LO = 65

CB_KV = ATTN_WIDTH // LANE
CB_WIN = CB_KV + N_KV_COMP * N_KV_HEADS
CB_CONV = CB_WIN + 2 * N_KV_HEADS
N_GATE = N_BRANCH * N_HEADS


def _alibi_slopes(n):
    def pow2(m):
        start = 2.0 ** (-8.0 / m)
        return [start ** (i + 1) for i in range(m)]
    if math.log2(n).is_integer():
        s = pow2(n)
    else:
        c = 2 ** math.floor(math.log2(n))
        s = pow2(c) + pow2(2 * c)[0::2][:n - c]
    return [float(np.float32(v)) for v in s]


SLOPES = _alibi_slopes(N_HEADS)


def _pick(n, target, mult=SUBLANE):
    for d in range(min(n, target), 0, -1):
        if n % d == 0 and d % mult == 0:
            return d
    raise ValueError((n, target, mult))


def _cparams(sem):
    return pltpu.CompilerParams(dimension_semantics=sem, vmem_limit_bytes=VMEM_LIMIT)


def _pair_specs(tm, width, n_first_blocks):
    return [pl.BlockSpec((tm, width), lambda i, *_: (jnp.minimum(i, n_first_blocks - 1), 0)),
            pl.BlockSpec((tm, width), lambda i, *_: (jnp.maximum(i - n_first_blocks, 0), 0))]


def _pair_load(first_ref, second_ref, n_first_blocks):
    return jnp.where(pl.program_id(0) < n_first_blocks, first_ref[...], second_ref[...])


def _proj_kernel(xp_ref, xs_ref, w_ref, o_ref, *, n_first):
    x = _pair_load(xp_ref, xs_ref, n_first)
    o_ref[...] = jnp.dot(x.astype(BF16), w_ref[...], preferred_element_type=F32)


def _proj(xp, xs, w, tm, tn):
    k = xp.shape[1]
    n = w.shape[1]
    n_first = xp.shape[0] // tm
    m = xp.shape[0] + xs.shape[0]
    return pl.pallas_call(
        functools.partial(_proj_kernel, n_first=n_first),
        out_shape=jax.ShapeDtypeStruct((m, n), F32),
        grid=(m // tm, n // tn),
        in_specs=_pair_specs(tm, k, n_first) + [pl.BlockSpec((k, tn), lambda i, j: (0, j))],
        out_specs=pl.BlockSpec((tm, tn), lambda i, j: (i, j)),
        compiler_params=_cparams(("parallel", "arbitrary")),
    )(xp, xs, w)


def _pe_bias_kernel(pe_ref, w1_ref, o_ref):
    o_ref[...] = jnp.dot(pe_ref[...].astype(BF16), w1_ref[...], preferred_element_type=F32)


def _pe_bias(pe, w1_bf):
    flat = jnp.broadcast_to(pe.reshape(1, -1), (SUBLANE, pe.size))
    return pl.pallas_call(
        _pe_bias_kernel,
        out_shape=jax.ShapeDtypeStruct((SUBLANE, CMP_HIDDEN), F32),
    )(flat, w1_bf)


S_PITCH = 136


def _scatter_rows(raw, plane, base, x):
    per = SUBLANE
    for j in range(x.shape[0] // per):
        c, s0 = divmod(j * per, CMP_STRIDE)
        raw[plane, pl.ds(s0 * S_PITCH + base + c, per, stride=S_PITCH), :] = x[j * per:(j + 1) * per]


def _compress_core(raw, weights, n_chunk, plane0=0):
    assert n_chunk <= S_PITCH
    outs = []
    for kv, (w1_ref, b_ref, w2_ref) in enumerate(weights):
        acc = jnp.zeros((N_KV_HEADS * n_chunk, CMP_RATIO * CMP_HIDDEN), F32)
        for sp in range(CMP_STRIDE // 2):
            per_g = []
            for g in range(N_KV_HEADS):
                halves = [raw[plane0 + kv * N_KV_HEADS + g, pl.ds(s * S_PITCH, n_chunk), :]
                          for s in (2 * sp, 2 * sp + 1)]
                per_g.append(jnp.concatenate(halves, axis=1))
            lhs = jnp.concatenate(per_g, axis=0).astype(BF16)
            acc = acc + jnp.dot(lhs, w1_ref[pl.ds(sp * 2 * HEAD_DIM, 2 * HEAD_DIM), :],
                                preferred_element_type=F32)
        per_kv = []
        for g in range(N_KV_HEADS):
            part0 = acc[g * n_chunk:(g + 1) * n_chunk, :CMP_HIDDEN]
            part1 = acc[g * n_chunk:(g + 1) * n_chunk, CMP_HIDDEN:]
            part1 = jnp.concatenate([part1[1:], part1[:1]], axis=0)
            pre = (b_ref[0:1, :] + part0) + part1
            per_kv.append(jnp.dot(jax.nn.gelu(pre).astype(BF16), w2_ref[...],
                                  preferred_element_type=F32))
        outs.append(per_kv)
    return outs


def _compress_kernel(ids_ref, *refs, n_pages):
    pages = refs[:n_pages]
    w1k_ref, w1v_ref, bk_ref, bv_ref, w2k_ref, w2v_ref, kc_ref, vc_ref, raw = refs[n_pages:]
    n_chunk = n_pages * (LANE // CMP_STRIDE)
    for cb in range(2 * N_KV_HEADS):
        for p in range(n_pages):
            _scatter_rows(raw, cb, p * (LANE // CMP_STRIDE),
                          pages[p][0, :, cb * HEAD_DIM:(cb + 1) * HEAD_DIM])
    kc, vc = _compress_core(raw, ((w1k_ref, bk_ref, w2k_ref), (w1v_ref, bv_ref, w2v_ref)), n_chunk)
    for g in range(N_KV_HEADS):
        kc_ref[0, g] = kc[g]
        vc_ref[0, g] = vc[g]


def _compress(page_ids, src, col_block, w1k, w1v, bk, bv, w2k, w2v):
    nb, n_pages = page_ids.shape
    n_chunk = n_pages * (LANE // CMP_STRIDE)
    width = 2 * KV_WIDTH

    def page_map(p):
        return lambda b, ids: (ids[b, p], 0, col_block)

    full = lambda shape: pl.BlockSpec(shape, lambda b, ids: (0,) * len(shape))
    in_specs = [pl.BlockSpec((1, LANE, width), page_map(p)) for p in range(n_pages)]
    in_specs += [full(w1k.shape), full(w1v.shape), full(bk.shape), full(bv.shape),
                 full(w2k.shape), full(w2v.shape)]
    out_spec = pl.BlockSpec((1, N_KV_HEADS, n_chunk, HEAD_DIM), lambda b, ids: (b, 0, 0, 0))
    out_sd = jax.ShapeDtypeStruct((nb, N_KV_HEADS, n_chunk, HEAD_DIM), F32)
    return pl.pallas_call(
        functools.partial(_compress_kernel, n_pages=n_pages),
        out_shape=(out_sd, out_sd),
        grid_spec=pltpu.PrefetchScalarGridSpec(
            num_scalar_prefetch=1, grid=(nb,), in_specs=in_specs,
            out_specs=(out_spec, out_spec),
            scratch_shapes=[pltpu.VMEM((2 * N_KV_HEADS, CMP_STRIDE * S_PITCH, HEAD_DIM), F32)]),
        compiler_params=_cparams(("parallel",)),
    )(page_ids, *([src] * n_pages), w1k, w1v, bk, bv, w2k, w2v)


def _nt_dot(a, b):
    return lax.dot_general(a, b, (((1,), (1,)), ((), ())), preferred_element_type=F32)


def _masked_softmax(s, mask):
    s = jnp.where(mask, s, NEG)
    e = jnp.where(mask, jnp.exp(s - jnp.max(s, axis=-1, keepdims=True)), 0.0)
    den = jnp.sum(e, axis=-1, keepdims=True)
    return e / jnp.where(den > 0, den, 1.0)


def _stack_heads(q, tq):
    q = q * (HEAD_DIM ** -0.5)
    return jnp.concatenate([q[:, r * HEAD_DIM:(r + 1) * HEAD_DIM] for r in range(GQA_REP)],
                           axis=0).astype(BF16)


def _row_consts(g, tq, p0):
    rows = GQA_REP * tq
    ridx = lax.broadcasted_iota(I32, (rows, 1), 0)
    slope = jnp.zeros((rows, 1), F32)
    for r in range(GQA_REP):
        slope = jnp.where(ridx // tq == r, SLOPES[g * GQA_REP + r], slope)
    t = p0 + ridx % tq
    return slope, t


def _cmp_branch(q4, kc, vc, slope, t, tq):
    n_chunk = kc.shape[0]
    s = _nt_dot(q4, kc.astype(BF16))
    c_end = lax.broadcasted_iota(I32, (1, n_chunk), 1) * CMP_STRIDE + (CMP_BLOCK - 1)
    rel = t - c_end
    s = s - slope * rel.astype(F32)
    p = _masked_softmax(s, rel >= 0)
    o_c = jnp.dot(p.astype(BF16), vc.astype(BF16), preferred_element_type=F32)
    p_sum = p[0:tq]
    for r in range(1, GQA_REP):
        p_sum = p_sum + p[r * tq:(r + 1) * tq]
    return o_c, p_sum


def _pos_aux(pos):
    pos = np.asarray(pos).reshape(-1, 1)
    lane = np.arange(LANE).reshape(1, -1)
    blk = pos // SEL_BLOCK
    aux = np.where(lane == blk, MASK_M, 0.0)
    aux = np.where(lane == AUX_HI, blk * SEL_BLOCK, aux)
    aux = np.where(lane == AUX_LO, pos % SEL_BLOCK, aux)
    assert blk.max() < AUX_HI
    return jnp.asarray(aux, F32).astype(BF16)


def _query_aux(slope, notsel4):
    lane = lax.broadcasted_iota(I32, (slope.shape[0], LANE), 1)
    base = jnp.zeros((slope.shape[0], LANE), F32) if notsel4 is None else notsel4
    return jnp.where((lane == AUX_HI) | (lane == AUX_LO), slope, base).astype(BF16)


def _not_selected(p_sum, t_row, n_sel):
    tq, n_chunk = p_sum.shape
    nj = -(-n_sel // SUBLANE) * SUBLANE
    cj = lax.broadcasted_iota(I32, (nj, n_chunk), 0) * SEL_BLOCK
    cn = lax.broadcasted_iota(I32, (nj, n_chunk), 1) * CMP_STRIDE
    overlap = jnp.maximum(jnp.minimum(cn + CMP_BLOCK, cj + SEL_BLOCK) - jnp.maximum(cn, cj), 0)
    overlap = overlap.astype(F32).astype(BF16)
    p_hi = p_sum.astype(BF16)
    p_lo = (p_sum - p_hi.astype(F32)).astype(BF16)
    imp = _nt_dot(overlap, p_hi) + _nt_dot(overlap, p_lo)
    row = lax.broadcasted_iota(I32, (nj, tq), 0)
    cur = t_row // SEL_BLOCK
    valid = (row <= cur) & (row < n_sel)
    forced = (row == 0) | (row == cur) | (row == cur - 1)
    score = jnp.where(valid, jnp.where(forced, BIG, imp), -1.0)
    notsel = jnp.ones((nj, tq), F32)
    for _ in range(min(SEL_TOPK, n_sel)):
        m = jnp.max(score, axis=0, keepdims=True)
        idx = jnp.min(jnp.where(score == m, row, nj), axis=0, keepdims=True)
        hit = row == idx
        notsel = jnp.where(hit & (m >= 0.0), 0.0, notsel)
        score = jnp.where(hit, -1.0, score)
    notsel = jnp.concatenate([notsel, jnp.ones((LANE - nj, tq), F32)], axis=0)
    if tq < LANE:
        notsel = jnp.concatenate([notsel, jnp.ones((LANE, LANE - tq), F32)], axis=1)
    return notsel.T


def _rep_rows(x):
    return jnp.concatenate([x] * GQA_REP, axis=0)


def _softmax_rows(s):
    e = jnp.exp(s - jnp.max(s, axis=-1, keepdims=True))
    return e / jnp.sum(e, axis=-1, keepdims=True)


def _gated_sum(sig, g, o_c, o_s, o_w, tq):
    outs = []
    for r in range(GQA_REP):
        h = g * GQA_REP + r
        sl = slice(r * tq, (r + 1) * tq)
        g0 = sig[:, N_BRANCH * h + 0:N_BRANCH * h + 1]
        g1 = sig[:, N_BRANCH * h + 1:N_BRANCH * h + 2]
        g2 = sig[:, N_BRANCH * h + 2:N_BRANCH * h + 3]
        outs.append(g0 * o_c[sl] + g1 * o_s[sl] + g2 * o_w[sl])
    return outs


def _prompt_attn_kernel(q_ref, gate_ref, kc_ref, vc_ref, ks_ref, vs_ref, kw_ref, vw_ref, aux_ref,
                        o_ref, ksb, vsb, kwb, vwb, *, tq, seq_len):
    i = pl.program_id(1)
    n_sel = -(-seq_len // SEL_BLOCK)
    heads = range(N_KV_HEADS)

    @pl.when(i == 0)
    def _():
        for g in heads:
            col = slice(g * HEAD_DIM, (g + 1) * HEAD_DIM)
            ksb[g, :, 0:HEAD_DIM] = ks_ref[:, col].astype(BF16)
            ksb[g, :, HEAD_DIM:] = aux_ref[...]
            kwb[g, :, 0:HEAD_DIM] = kw_ref[:, col].astype(BF16)
            kwb[g, :, HEAD_DIM:] = aux_ref[...]
            vsb[g] = vs_ref[:, col].astype(BF16)
            vwb[g] = vw_ref[:, col].astype(BF16)

    p0 = i * tq
    rows = GQA_REP * tq
    t_row = p0 + lax.broadcasted_iota(I32, (1, tq), 1)
    q_sel, q_win, o_c = [], [], []
    for g in heads:
        q4 = _stack_heads(q_ref[:, g * GQA_REP * HEAD_DIM:(g + 1) * GQA_REP * HEAD_DIM], tq)
        slope, t = _row_consts(g, tq, p0)
        oc, p_sum = _cmp_branch(q4, kc_ref[0, g], vc_ref[0, g], slope, t, tq)
        notsel4 = _rep_rows(_not_selected(p_sum, t_row, n_sel))
        q_sel.append(jnp.concatenate([q4, _query_aux(slope, notsel4)], axis=1))
        q_win.append(jnp.concatenate([q4, _query_aux(slope, None)], axis=1))
        o_c.append(oc)
    _, t = _row_consts(0, tq, p0)

    def step(g, c, carry, causal):
        m, l, acc = carry
        start = pl.multiple_of(c * SEL_CHUNK, SEL_CHUNK)
        s = _nt_dot(q_sel[g], ksb[g, pl.ds(start, SEL_CHUNK), :])
        if causal:
            pos = start + lax.broadcasted_iota(I32, (1, SEL_CHUNK), 1)
            s = jnp.where(pos <= t, s, NEG)
        m_new = jnp.maximum(m, jnp.max(s, axis=-1, keepdims=True))
        a = jnp.exp(m - m_new)
        e = jnp.exp(s - m_new)
        l = a * l + jnp.sum(e, axis=-1, keepdims=True)
        acc = a * acc + jnp.dot(e.astype(BF16), vsb[g, pl.ds(start, SEL_CHUNK), :],
                                preferred_element_type=F32)
        return m_new, l, acc

    def chunk(c, carries, causal):
        return tuple(step(g, c, carries[g], causal) for g in heads)

    last = (p0 + tq - 1) // SEL_CHUNK
    init = tuple((jnp.full((rows, 1), NEG, F32), jnp.zeros((rows, 1), F32),
                  jnp.zeros((rows, HEAD_DIM), F32)) for _ in heads)
    carries = lax.fori_loop(0, last, lambda c, cr: chunk(c, cr, False), init)
    carries = chunk(last, carries, True)

    n_win = WINDOW + tq
    start = pl.multiple_of(jnp.maximum(p0 - WINDOW, 0), tq)
    rel = ((p0 - start) + lax.broadcasted_iota(I32, (tq, n_win), 0)
           - lax.broadcasted_iota(I32, (tq, n_win), 1))
    band = _rep_rows(jnp.where((rel >= 0) & (rel < WINDOW), 0.0, NEG))
    sig = jax.nn.sigmoid(gate_ref[...])
    for g in heads:
        _, l, acc = carries[g]
        o_s = acc / l
        s = _nt_dot(q_win[g], kwb[g, pl.ds(start, n_win), :]) + band
        o_w = jnp.dot(_softmax_rows(s).astype(BF16), vwb[g, pl.ds(start, n_win), :],
                      preferred_element_type=F32)
        outs = _gated_sum(sig, g, o_c[g], o_s, o_w, tq)
        for r in range(GQA_REP):
            h = g * GQA_REP + r
            o_ref[:, h * HEAD_DIM:(h + 1) * HEAD_DIM] = outs[r]


def _prompt_attention(proj, kc, vc, n_batch, seq_len, n_cols_blocks):
    tq = Q_BLOCK
    nqb = seq_len // tq
    gate_cb = n_cols_blocks - 1
    aux = _pos_aux(np.arange(seq_len))
    kv_spec = lambda cb: pl.BlockSpec((seq_len, KV_WIDTH), lambda b, i: (b, cb * LANE // KV_WIDTH))
    c_spec = pl.BlockSpec((1,) + kc.shape[1:], lambda b, i: (b, 0, 0, 0))
    bf = lambda width: pltpu.VMEM((N_KV_HEADS, seq_len, width), BF16)
    return pl.pallas_call(
        functools.partial(_prompt_attn_kernel, tq=tq, seq_len=seq_len),
        out_shape=jax.ShapeDtypeStruct((n_batch * seq_len, ATTN_WIDTH), F32),
        grid=(n_batch, nqb),
        in_specs=[pl.BlockSpec((tq, ATTN_WIDTH), lambda b, i: (b * nqb + i, 0)),
                  pl.BlockSpec((tq, LANE), lambda b, i: (b * nqb + i, gate_cb)),
                  c_spec, c_spec,
                  kv_spec(CB_KV + 2 * N_KV_HEADS), kv_spec(CB_KV + 3 * N_KV_HEADS),
                  kv_spec(CB_WIN), kv_spec(CB_WIN + N_KV_HEADS),
                  pl.BlockSpec(aux.shape, lambda b, i: (0, 0))],
        out_specs=pl.BlockSpec((tq, ATTN_WIDTH), lambda b, i: (b * nqb + i, 0)),
        scratch_shapes=[bf(2 * HEAD_DIM), bf(HEAD_DIM), bf(2 * HEAD_DIM), bf(HEAD_DIM)],
        compiler_params=_cparams(("parallel", "arbitrary")),
    )(proj, proj, kc, vc, proj, proj, proj, proj, aux)


def _sample_attn_kernel(pt_ref, *refs, n_pages, n_sub, tq, past_len, win_buf):
    refs_rest = refs[n_sub * n_pages:]
    for u in range(n_sub):
        _sample_attn_one(u, refs[u * n_pages:(u + 1) * n_pages], refs_rest, tq, past_len, win_buf)


def _sample_attn_one(u, pages, refs, tq, past_len, win_buf):
    (q_ref, kvn_ref, winn_ref, gate_ref, win_ref, w1k_ref, w1v_ref, bk_ref, bv_ref, w2k_ref,
     w2v_ref, saux_ref, waux_ref, o_ref, kvo_ref, wino_ref, raw) = refs
    rs = slice(u * tq, (u + 1) * tq)
    plane0 = u * 2 * N_KV_HEADS
    n_pages = len(pages)
    seq_len = past_len + tq
    n_sel = -(-seq_len // SEL_BLOCK)
    n_planes = N_KV_COMP * N_KV_HEADS
    n_wplanes = 2 * N_KV_HEADS
    page = pages[0].shape[1] // n_planes
    n_chunk = past_len // CMP_STRIDE
    zpad = jnp.zeros((LANE - tq, HEAD_DIM), BF16)

    for c in range(n_planes):
        kvo_ref[pl.ds(u * tq * n_planes + c, tq, stride=n_planes), :] = kvn_ref[rs, c * HEAD_DIM:(c + 1) * HEAD_DIM]
    keep = (win_buf - tq) * n_wplanes
    wino_ref[u, 0:keep, :] = win_ref[u, tq * n_wplanes:win_buf * n_wplanes, :]
    for c in range(n_wplanes):
        wino_ref[u, pl.ds(keep + c, tq, stride=n_wplanes), :] = winn_ref[rs, c * HEAD_DIM:(c + 1) * HEAD_DIM]

    for cb in range(2 * N_KV_HEADS):
        for p in range(n_pages):
            _scatter_rows(raw, plane0 + cb, p * (page // CMP_STRIDE),
                          pages[p][0, pl.ds(cb, page, stride=n_planes), :])
    kc, vc = _compress_core(raw, ((w1k_ref, bk_ref, w2k_ref), (w1v_ref, bv_ref, w2v_ref)), n_chunk,
                            plane0)

    sig = jax.nn.sigmoid(gate_ref[rs, :])
    q4s, consts, o_cs, p_sums = [], [], [], []
    for g in range(N_KV_HEADS):
        q4 = _stack_heads(q_ref[rs, g * GQA_REP * HEAD_DIM:(g + 1) * GQA_REP * HEAD_DIM], tq)
        slope, t = _row_consts(g, tq, past_len)
        o_c, p_sum = _cmp_branch(q4, kc[g], vc[g], slope, t, tq)
        q4s.append(q4); consts.append((slope, t)); o_cs.append(o_c); p_sums.append(p_sum)
    t_row = past_len + lax.broadcasted_iota(I32, (1, N_KV_HEADS * tq), 1) % tq
    notsel = _not_selected(jnp.concatenate(p_sums, axis=0), t_row, n_sel)

    for g in range(N_KV_HEADS):
        q4, (slope, t), o_c = q4s[g], consts[g], o_cs[g]
        q_sel = jnp.concatenate([q4, _query_aux(slope, _rep_rows(notsel[g * tq:(g + 1) * tq]))], axis=1)
        q_win = jnp.concatenate([q4, _query_aux(slope, None)], axis=1)

        kpl = 2 * N_KV_HEADS + g
        vpl = 3 * N_KV_HEADS + g
        k = jnp.concatenate(
            [p[0, pl.ds(kpl, page, stride=n_planes), :].astype(BF16) for p in pages]
            + [kvn_ref[rs, kpl * HEAD_DIM:(kpl + 1) * HEAD_DIM].astype(BF16), zpad], axis=0)
        v = jnp.concatenate(
            [p[0, pl.ds(vpl, page, stride=n_planes), :].astype(BF16) for p in pages]
            + [kvn_ref[rs, vpl * HEAD_DIM:(vpl + 1) * HEAD_DIM].astype(BF16), zpad], axis=0)
        pos = lax.broadcasted_iota(I32, (1, k.shape[0]), 1)
        s = _nt_dot(q_sel, jnp.concatenate([k, saux_ref[...]], axis=1))
        p = _softmax_rows(jnp.where(pos <= t, s, NEG))
        o_s = jnp.dot(p.astype(BF16), v, preferred_element_type=F32)

        kw = jnp.concatenate([win_ref[u, pl.ds(g, win_buf, stride=n_wplanes), :].astype(BF16),
                              winn_ref[rs, g * HEAD_DIM:(g + 1) * HEAD_DIM].astype(BF16), zpad],
                             axis=0)
        vp = N_KV_HEADS + g
        vw = jnp.concatenate([win_ref[u, pl.ds(vp, win_buf, stride=n_wplanes), :].astype(BF16),
                              winn_ref[rs, vp * HEAD_DIM:(vp + 1) * HEAD_DIM].astype(BF16), zpad],
                             axis=0)
        kpos = (past_len - win_buf) + lax.broadcasted_iota(I32, (1, kw.shape[0]), 1)
        rel = t - kpos
        s = _nt_dot(q_win, jnp.concatenate([kw, waux_ref[...]], axis=1))
        p = _softmax_rows(jnp.where((rel >= 0) & (rel < WINDOW), s, NEG))
        o_w = jnp.dot(p.astype(BF16), vw, preferred_element_type=F32)

        outs = _gated_sum(sig, g, o_c, o_s, o_w, tq)
        for r in range(GQA_REP):
            h = g * GQA_REP + r
            o_ref[rs, h * HEAD_DIM:(h + 1) * HEAD_DIM] = outs[r]


def _sample_attention(page_table, cache, proj, state_win, cmp_w, n_prompt_rows, tq, n_cols_blocks):
    db, n_pages = page_table.shape
    n_planes = N_KV_COMP * N_KV_HEADS
    n_wplanes = 2 * N_KV_HEADS
    page = cache.shape[1] // n_planes
    past_len = n_pages * page
    win_buf = state_win.shape[1] // n_wplanes
    n_sub = SAMPLE_SUB if db % SAMPLE_SUB == 0 and (n_prompt_rows // tq) % SAMPLE_SUB == 0 else 1
    rows = n_sub * tq
    r0 = n_prompt_rows // rows
    gate_cb = n_cols_blocks - 1
    sel_aux = _pos_aux(np.arange(past_len + LANE))
    win_aux = _pos_aux(past_len - win_buf + np.arange(win_buf + LANE))

    def page_map(u, p):
        return lambda b, pt: (pt[n_sub * b + u, p], 0, 0)

    full = lambda a: pl.BlockSpec(a.shape, lambda b, pt: (0,) * a.ndim)
    in_specs = [pl.BlockSpec((1,) + cache.shape[1:], page_map(u, p))
                for u in range(n_sub) for p in range(n_pages)]
    in_specs += [
        pl.BlockSpec((rows, ATTN_WIDTH), lambda b, pt: (r0 + b, 0)),
        pl.BlockSpec((rows, N_KV_COMP * KV_WIDTH), lambda b, pt: (r0 + b, 1)),
        pl.BlockSpec((rows, 2 * KV_WIDTH), lambda b, pt: (r0 + b, CB_WIN * LANE // (2 * KV_WIDTH))),
        pl.BlockSpec((rows, LANE), lambda b, pt: (r0 + b, gate_cb)),
        pl.BlockSpec((n_sub,) + state_win.shape[1:], lambda b, pt: (b, 0, 0)),
    ] + [full(w) for w in cmp_w] + [full(sel_aux), full(win_aux)]
    return pl.pallas_call(
        functools.partial(_sample_attn_kernel, n_pages=n_pages, n_sub=n_sub, tq=tq,
                          past_len=past_len, win_buf=win_buf),
        out_shape=(jax.ShapeDtypeStruct((db * tq, ATTN_WIDTH), F32),
                   jax.ShapeDtypeStruct((db * tq * n_planes, HEAD_DIM), F32),
                   jax.ShapeDtypeStruct(state_win.shape, F32)),
        grid_spec=pltpu.PrefetchScalarGridSpec(
            num_scalar_prefetch=1, grid=(db // n_sub,), in_specs=in_specs,
            out_specs=(pl.BlockSpec((rows, ATTN_WIDTH), lambda b, pt: (b, 0)),
                       pl.BlockSpec((rows * n_planes, HEAD_DIM), lambda b, pt: (b, 0)),
                       pl.BlockSpec((n_sub,) + state_win.shape[1:], lambda b, pt: (b, 0, 0))),
            scratch_shapes=[pltpu.VMEM((n_sub * 2 * N_KV_HEADS, CMP_STRIDE * S_PITCH, HEAD_DIM),
                                       F32)]),
        compiler_params=_cparams(("parallel",)),
    )(page_table, *([cache] * (n_sub * n_pages)), proj, proj, proj, proj, state_win, *cmp_w, sel_aux, win_aux)


def _interleave_kernel(x_ref, o_ref, *, n_planes):
    rows = x_ref.shape[0]
    for c in range(n_planes):
        o_ref[pl.ds(c, rows, stride=n_planes), :] = x_ref[:, c * LANE:(c + 1) * LANE]


def _interleave(proj, n_blocks, block_rows, n_planes, row_block_of, col_block):
    return pl.pallas_call(
        functools.partial(_interleave_kernel, n_planes=n_planes),
        out_shape=jax.ShapeDtypeStruct((n_blocks * block_rows * n_planes, LANE), F32),
        grid=(n_blocks,),
        in_specs=[pl.BlockSpec((block_rows, n_planes * LANE), lambda i: (row_block_of(i), col_block))],
        out_specs=pl.BlockSpec((block_rows * n_planes, LANE), lambda i: (i, 0)),
        compiler_params=_cparams(("parallel",)),
    )(proj)


def _conv_kernel(*refs, seq_len, has_state):
    if has_state:
        b_ref, c_ref, h_ref, w_ref, s1_ref, s2_ref, y_ref, u_ref = refs
    else:
        b_ref, c_ref, h_ref, w_ref, y_ref, u_ref = refs
    u = c_ref[...] * h_ref[...]
    rows = u.shape[0]
    t = lax.broadcasted_iota(I32, (rows, 1), 0) % seq_len
    p1 = jnp.where(t >= 1, pltpu.roll(u, 1, 0), 0.0)
    p2 = jnp.where(t >= 2, pltpu.roll(u, 2, 0), 0.0)
    if has_state:
        p1 = p1 + s1_ref[...]
        p2 = p2 + s2_ref[...]
    y = w_ref[0:1, :] * p2
    y = y + w_ref[1:2, :] * p1
    y = y + w_ref[2:3, :] * u
    y_ref[...] = b_ref[...] * y
    u_ref[...] = u[rows - u_ref.shape[0]:, :]


def _short_conv(proj, conv_w, row_block0, n_blocks, block_rows, seq_len, state=None):
    conv_dim = conv_w.shape[1]
    n_ct = conv_dim // LANE
    tail = SUBLANE if state is None else block_rows
    spec = lambda cb: pl.BlockSpec((block_rows, LANE), lambda r, c: (row_block0 + r, cb + c))
    in_specs = [spec(CB_CONV), spec(CB_CONV + n_ct), spec(CB_CONV + 2 * n_ct),
                pl.BlockSpec((CONV_WIDTH, LANE), lambda r, c: (0, c))]
    args = [proj, proj, proj, conv_w]
    if state is not None:
        st_spec = pl.BlockSpec((block_rows, LANE), lambda r, c: (r, c))
        in_specs += [st_spec, st_spec]
        args += list(state)
    return pl.pallas_call(
        functools.partial(_conv_kernel, seq_len=seq_len, has_state=state is not None),
        out_shape=(jax.ShapeDtypeStruct((n_blocks * block_rows, conv_dim), F32),
                   jax.ShapeDtypeStruct((n_blocks * tail, conv_dim), F32)),
        grid=(n_blocks, n_ct),
        in_specs=in_specs,
        out_specs=(pl.BlockSpec((block_rows, LANE), lambda r, c: (r, c)),
                   pl.BlockSpec((tail, LANE), lambda r, c: (r, c))),
        compiler_params=_cparams(("parallel", "parallel")),
    )(*args)


def _layer_norm(x, g, b):
    mu = jnp.mean(x, axis=-1, keepdims=True)
    var = jnp.mean(jnp.square(x - mu), axis=-1, keepdims=True)
    return (x - mu) * lax.rsqrt(var + LN_EPS) * g + b


def _mix_kernel(oa_p, oa_s, oc_p, oc_s, x_p, x_s, wa_ref, wc_ref, g_ref, b_ref, wr_ref, br_ref,
                x1_ref, e_ref, gt_ref, *, dn_alpha, n_experts, n_first):
    o_attn = _pair_load(oa_p, oa_s, n_first)
    o_conv = _pair_load(oc_p, oc_s, n_first)
    mix = jnp.dot(o_attn.astype(BF16), wa_ref[...], preferred_element_type=F32)
    mix = mix + jnp.dot(o_conv.astype(BF16), wc_ref[...], preferred_element_type=F32)
    x1 = _layer_norm(dn_alpha * _pair_load(x_p, x_s, n_first) + mix, g_ref[...], b_ref[...])
    x1_ref[...] = x1
    x_hi = x1.astype(BF16)
    x_lo = (x1 - x_hi.astype(F32)).astype(BF16)
    w_r = wr_ref[...]
    w_hi = w_r.astype(BF16)
    w_lo = (w_r - w_hi.astype(F32)).astype(BF16)
    logits = (jnp.dot(x_hi, w_hi, preferred_element_type=F32)
              + jnp.dot(x_lo, w_hi, preferred_element_type=F32)
              + jnp.dot(x_hi, w_lo, preferred_element_type=F32)) + br_ref[...]
    tm = logits.shape[0]
    lane = lax.broadcasted_iota(I32, (tm, LANE), 1)
    score = jnp.where(lane < n_experts, logits, -jnp.inf)
    e_out = jnp.zeros((tm, LANE), I32)
    v_out = jnp.zeros((tm, LANE), F32)
    v0 = None
    den = jnp.zeros((tm, 1), F32)
    for k in range(TOP_K):
        m = jnp.max(score, axis=-1, keepdims=True)
        idx = jnp.min(jnp.where(score == m, lane, LANE), axis=-1, keepdims=True)
        if k == 0:
            v0 = m
        ex = jnp.exp(m - v0)
        den = den + ex
        e_out = jnp.where(lane == k, idx, e_out)
        v_out = jnp.where(lane == k, ex, v_out)
        score = jnp.where(lane == idx, -jnp.inf, score)
    e_ref[...] = e_out
    gt_ref[...] = v_out / den


def _mix_ln_router(o_attn, o_conv, x, w_out_a, w_out_c, ln_g, ln_b, w_r, b_r, dn_alpha, n_experts):
    n = x[0].shape[0] + x[1].shape[0]
    d = x[0].shape[1]
    tm = _pick(x[1].shape[0], 256)
    n_first = x[0].shape[0] // tm
    row = lambda w: pl.BlockSpec((tm, w), lambda i: (i, 0))
    full = lambda a: pl.BlockSpec(a.shape, lambda i: (0, 0))
    pairs = lambda p: _pair_specs(tm, p[0].shape[1], n_first)
    return pl.pallas_call(
        functools.partial(_mix_kernel, dn_alpha=dn_alpha, n_experts=n_experts, n_first=n_first),
        out_shape=(jax.ShapeDtypeStruct((n, d), F32),
                   jax.ShapeDtypeStruct((n, LANE), I32), jax.ShapeDtypeStruct((n, LANE), F32)),
        grid=(n // tm,),
        in_specs=pairs(o_attn) + pairs(o_conv) + pairs(x) + [
            full(w_out_a), full(w_out_c), full(ln_g), full(ln_b), full(w_r), full(b_r)],
        out_specs=(row(d), row(LANE), row(LANE)),
        compiler_params=_cparams(("parallel",)),
    )(*o_attn, *o_conv, *x, w_out_a, w_out_c, ln_g, ln_b, w_r, b_r)


def _weight_pipeline(f, m, nf, chg_ref, nxt_ref, wrap_ref, cnt_ref, copies, cast):
    @pl.when((f == 0) & (m == 0))
    def _():
        cnt_ref[0] = 0
        for c in copies(nxt_ref[0], 0, 0):
            c.start()

    @pl.when(chg_ref[m] == 1)
    def _():
        slot = cnt_ref[0] % 2
        f_next = f + wrap_ref[m]

        @pl.when(f_next < nf)
        def _():
            for c in copies(nxt_ref[m + 1], f_next, 1 - slot):
                c.start()

        for c in copies(0, 0, slot):
            c.wait()
        cast(slot)
        cnt_ref[0] = cnt_ref[0] + 1


def _gate_up_kernel(be_ref, br_ref, chg_ref, nxt_ref, wrap_ref, x_ref, w_hbm, bg_ref, bu_ref, a_ref,
                    wbuf, wgb, wub, sem, cnt_ref, *, nf, tf):
    f = pl.program_id(0)
    m = pl.program_id(1)
    rows = br_ref[m]
    half = MOE_TM // 2

    def copies(e, fi, slot):
        return [pltpu.make_async_copy(
            w_hbm.at[e, :, pl.ds(pl.multiple_of((part * nf + fi) * tf, tf), tf)],
            wbuf.at[slot, part], sem.at[slot, part]) for part in range(2)]

    def cast(slot):
        wgb[...] = wbuf[slot, 0].astype(BF16)
        wub[...] = wbuf[slot, 1].astype(BF16)

    _weight_pipeline(f, m, nf, chg_ref, nxt_ref, wrap_ref, cnt_ref, copies, cast)

    def act(x):
        hg = jnp.dot(x, wgb[...], preferred_element_type=F32) + bg_ref[0]
        hu = jnp.dot(x, wub[...], preferred_element_type=F32) + bu_ref[0]
        hg = jnp.minimum(hg, SWIGLU_LIMIT)
        hu = jnp.clip(hu, -SWIGLU_LIMIT, SWIGLU_LIMIT)
        return (hg * jax.nn.sigmoid(SWIGLU_ALPHA * hg) * (hu + 1.0)).astype(BF16)

    @pl.when(rows > half)
    def _():
        a_ref[...] = act(x_ref[...].astype(BF16))

    @pl.when((rows > 0) & (rows <= half))
    def _():
        a_ref[0:half, :] = act(x_ref[0:half, :].astype(BF16))
        a_ref[half:, :] = jnp.zeros((MOE_TM - half, a_ref.shape[1]), BF16)

    @pl.when(rows == 0)
    def _():
        a_ref[...] = jnp.zeros_like(a_ref)


def _down_kernel(be_ref, br_ref, chg_ref, nxt_ref, wrap_ref, a_ref, w_hbm, b_ref, y_ref,
                 wbuf, wb, sem, cnt_ref, *, nj, tn):
    j = pl.program_id(0)
    m = pl.program_id(1)
    rows = br_ref[m]
    half = MOE_TM // 2

    def copies(e, ji, slot):
        return [pltpu.make_async_copy(
            w_hbm.at[e, :, pl.ds(pl.multiple_of(ji * tn, tn), tn)], wbuf.at[slot], sem.at[slot])]

    def cast(slot):
        wb[...] = wbuf[slot].astype(BF16)

    _weight_pipeline(j, m, nj, chg_ref, nxt_ref, wrap_ref, cnt_ref, copies, cast)

    @pl.when(rows > half)
    def _():
        y_ref[...] = jnp.dot(a_ref[...], wb[...], preferred_element_type=F32) + b_ref[0]

    @pl.when((rows > 0) & (rows <= half))
    def _():
        y_ref[0:half, :] = jnp.dot(a_ref[0:half, :], wb[...], preferred_element_type=F32) + b_ref[0]
        y_ref[half:, :] = jnp.zeros((MOE_TM - half, y_ref.shape[1]), F32)

    @pl.when(rows == 0)
    def _():
        y_ref[...] = jnp.zeros_like(y_ref)


def _experts(plan, xr, w_gu, b_gu, w_down, b_down):
    block_e, block_rows, chg, nxt, wrap = plan
    rows, d = xr.shape
    d_ff = w_down.shape[1]
    n_blocks = rows // MOE_TM
    tf = _pick(d_ff, 1024, LANE)
    nf = d_ff // tf
    a = pl.pallas_call(
        functools.partial(_gate_up_kernel, nf=nf, tf=tf),
        out_shape=jax.ShapeDtypeStruct((rows, d_ff), BF16),
        grid_spec=pltpu.PrefetchScalarGridSpec(
            num_scalar_prefetch=5, grid=(nf, n_blocks),
            in_specs=[pl.BlockSpec((MOE_TM, d), lambda f, m, be, *_: (m, 0)),
                      pl.BlockSpec(memory_space=pl.ANY),
                      pl.BlockSpec((1, 1, tf), lambda f, m, be, *_: (be[m], 0, f)),
                      pl.BlockSpec((1, 1, tf), lambda f, m, be, *_: (be[m], 0, nf + f))],
            out_specs=pl.BlockSpec((MOE_TM, tf), lambda f, m, be, *_: (m, f)),
            scratch_shapes=[pltpu.VMEM((2, 2, d, tf), F32), pltpu.VMEM((d, tf), BF16),
                            pltpu.VMEM((d, tf), BF16), pltpu.SemaphoreType.DMA((2, 2)),
                            pltpu.SMEM((1,), I32)]),
        compiler_params=_cparams(("arbitrary", "arbitrary")),
    )(block_e, block_rows, chg, nxt, wrap, xr, w_gu, b_gu, b_gu)
    tn = _pick(d, 2048, LANE)
    nj = d // tn
    return pl.pallas_call(
        functools.partial(_down_kernel, nj=nj, tn=tn),
        out_shape=jax.ShapeDtypeStruct((rows, d), F32),
        grid_spec=pltpu.PrefetchScalarGridSpec(
            num_scalar_prefetch=5, grid=(nj, n_blocks),
            in_specs=[pl.BlockSpec((MOE_TM, d_ff), lambda j, m, be, *_: (m, 0)),
                      pl.BlockSpec(memory_space=pl.ANY),
                      pl.BlockSpec((1, 1, tn), lambda j, m, be, *_: (be[m], 0, j))],
            out_specs=pl.BlockSpec((MOE_TM, tn), lambda j, m, be, *_: (m, j)),
            scratch_shapes=[pltpu.VMEM((2, d_ff, tn), F32), pltpu.VMEM((d_ff, tn), BF16),
                            pltpu.SemaphoreType.DMA((2,)), pltpu.SMEM((1,), I32)]),
        compiler_params=_cparams(("arbitrary", "arbitrary")),
    )(block_e, block_rows, chg, nxt, wrap, a, w_down, b_down)


def _combine_kernel(x1_ref, y_ref, gt_ref, g_ref, b_ref, o_ref, *, dn_alpha):
    gt = gt_ref[...]
    moe = y_ref[0] * gt[:, 0:1]
    for k in range(1, TOP_K):
        moe = moe + y_ref[k] * gt[:, k:k + 1]
    o_ref[...] = _layer_norm(dn_alpha * x1_ref[...] + moe, g_ref[...], b_ref[...])


def _combine_ln(x1, yk, gates, ln_g, ln_b, dn_alpha, row0, n):
    d = x1.shape[1]
    tm = _pick(n, 256)
    r0 = row0 // tm
    return pl.pallas_call(
        functools.partial(_combine_kernel, dn_alpha=dn_alpha),
        out_shape=jax.ShapeDtypeStruct((n, d), F32),
        grid=(n // tm,),
        in_specs=[pl.BlockSpec((tm, d), lambda i: (r0 + i, 0)),
                  pl.BlockSpec((TOP_K, tm, d), lambda i: (0, r0 + i, 0)),
                  pl.BlockSpec((tm, LANE), lambda i: (r0 + i, 0)),
                  pl.BlockSpec((1, d), lambda i: (0, 0)),
                  pl.BlockSpec((1, d), lambda i: (0, 0))],
        out_specs=pl.BlockSpec((tm, d), lambda i: (i, 0)),
        compiler_params=_cparams(("parallel",)),
    )(x1, yk, gates, ln_g, ln_b)


def _route(top_e, n_experts):
    n = top_e.shape[0]
    n4 = n * TOP_K
    e_flat = top_e.reshape(-1)
    entry = jnp.arange(n4, dtype=I32)
    _, order = lax.sort((e_flat, entry), num_keys=1, is_stable=True)
    experts = jnp.arange(n_experts, dtype=I32)
    counts = jnp.sum((e_flat[:, None] == experts[None, :]).astype(I32), axis=0)
    ends = jnp.cumsum(counts)
    starts = ends - counts
    padded = (counts + MOE_TM - 1) // MOE_TM * MOE_TM
    pends = jnp.cumsum(padded)
    pstarts = pends - padded
    dest_s = entry + jnp.sum(jnp.where(entry[:, None] >= ends[None, :],
                                       (padded - counts)[None, :], 0), axis=1)
    _, pos = lax.sort((order, dest_s), num_keys=1)
    n_blocks = -(-n4 // MOE_TM) + n_experts
    block0 = jnp.arange(n_blocks, dtype=I32) * MOE_TM
    block_e = jnp.minimum(jnp.sum((pends[None, :] <= block0[:, None]).astype(I32), axis=1),
                          n_experts - 1)
    block_rows = jnp.clip(counts[block_e] - (block0 - pstarts[block_e]), 0, MOE_TM)
    in_block = jnp.arange(MOE_TM, dtype=I32)[None, :]
    src = (block0 + starts[block_e] - pstarts[block_e])[:, None] + in_block
    row_tok = jnp.where(in_block < block_rows[:, None],
                        order[jnp.clip(src, 0, n4 - 1)] // TOP_K, 0).reshape(-1)
    blk = jnp.arange(n_blocks, dtype=I32)
    prev_e = jnp.concatenate([block_e[:1] - 1, block_e[:-1]])
    chg = (block_rows > 0) & (block_e != prev_e)
    later = jnp.where(chg, blk, n_blocks)
    nxt_idx = lax.cummin(jnp.concatenate([later[1:], later[:1] * 0 + n_blocks]), reverse=True)
    wrap = nxt_idx >= n_blocks
    nxt_e = block_e[jnp.where(wrap, 0, nxt_idx)]
    nxt = jnp.concatenate([block_e[:1], nxt_e])
    plan = (block_e, block_rows.astype(I32), chg.astype(I32), nxt.astype(I32), wrap.astype(I32))
    return row_tok, pos.reshape(n, TOP_K), plan


def _layer(x_prompt, x_sample, cache, state_win, state_conv, page_table,
           w_in, w_out, conv_w, cmp_k_pe, cmp_k_w1, cmp_k_w2, cmp_v_pe, cmp_v_w1, cmp_v_w2,
           ln1_g, ln1_b, w_router, b_router, w_gu, b_gu, w_down, b_down, ln2_g, ln2_b, depth):
    nb, seq_len, d = x_prompt.shape
    db, dec_seq, _ = x_sample.shape
    n_planes = N_KV_COMP * N_KV_HEADS
    n_wplanes = 2 * N_KV_HEADS
    page = cache.shape[1] // n_planes
    n_pages = page_table.shape[1]
    past_len = n_pages * page
    win_buf = state_win.shape[1] // n_wplanes
    conv_dim = conv_w.shape[1]
    n_experts = w_router.shape[1]
    n_p, n_s = nb * seq_len, db * dec_seq
    dn_alpha = (2.0 * depth) ** 0.25
    assert page == LANE and dec_seq == SUBLANE and seq_len % Q_BLOCK == 0
    assert seq_len >= WINDOW + Q_BLOCK and win_buf == WINDOW and seq_len % win_buf == 0
    assert n_p % n_s == 0 and conv_dim % LANE == 0 and n_experts <= LANE
    assert past_len + dec_seq <= 2 * 32 * SEL_BLOCK and seq_len <= 32 * SEL_BLOCK

    off_gate = ATTN_WIDTH + N_KV_COMP * KV_WIDTH + 2 * KV_WIDTH
    off_conv = off_gate + N_GATE
    w_p = jnp.concatenate([w_in[:, :off_gate], w_in[:, off_conv:], w_in[:, off_gate:off_conv],
                           jnp.zeros((d, LANE - N_GATE), w_in.dtype)], axis=1).astype(BF16)
    n_cb = w_p.shape[1] // LANE
    xp, xs = x_prompt.reshape(n_p, d), x_sample.reshape(n_s, d)
    n_tok = n_p + n_s
    proj = _proj(xp, xs, w_p, _pick(n_s, 512), _pick(w_p.shape[1], 1152, LANE))

    kv_cb = ATTN_WIDTH // (n_planes * LANE)
    tb = _pick(n_p, 256)
    kv_prompt = _interleave(proj, n_p // tb, tb, n_planes, lambda i: i, kv_cb)
    win_cb = (ATTN_WIDTH + n_planes * LANE) // (n_wplanes * LANE)
    per_seq = seq_len // win_buf
    win_prompt = _interleave(proj, nb, win_buf, n_wplanes,
                             lambda i: i * per_seq + per_seq - 1, win_cb)

    w1k = jnp.concatenate(jnp.split(cmp_k_w1, CMP_RATIO, axis=0), axis=1).astype(BF16)
    w1v = jnp.concatenate(jnp.split(cmp_v_w1, CMP_RATIO, axis=0), axis=1).astype(BF16)
    bk = _pe_bias(cmp_k_pe, cmp_k_w1.astype(BF16))
    bv = _pe_bias(cmp_v_pe, cmp_v_w1.astype(BF16))
    cmp_w = (w1k, w1v, bk, bv, cmp_k_w2.astype(BF16), cmp_v_w2.astype(BF16))
    pp = seq_len // LANE
    prompt_ids = jnp.arange(nb * pp, dtype=I32).reshape(nb, pp)
    proj_pages = proj.reshape(n_tok // LANE, LANE, n_cb * LANE)
    kc_p, vc_p = _compress(prompt_ids, proj_pages, ATTN_WIDTH // (2 * KV_WIDTH), *cmp_w)

    o_p = _prompt_attention(proj, kc_p, vc_p, nb, seq_len, n_cb)
    o_s, kv_sample, win_sample = _sample_attention(page_table, cache, proj, state_win, cmp_w,
                                                   n_p, dec_seq, n_cb)

    c_p, u_p = _short_conv(proj, conv_w, 0, nb, seq_len, seq_len)
    z = jnp.zeros((db, dec_seq - 2, conv_dim), F32)
    s1 = jnp.concatenate([state_conv[:, 1:2], z, z[:, :1]], axis=1).reshape(n_s, conv_dim)
    s2 = jnp.concatenate([state_conv, z], axis=1).reshape(n_s, conv_dim)
    c_s, u_s = _short_conv(proj, conv_w, n_p // n_s, 1, n_s, dec_seq, state=(s1, s2))
    conv_prompt = u_p.reshape(nb, SUBLANE, conv_dim)[:, SUBLANE - (CONV_WIDTH - 1):]
    conv_sample = u_s.reshape(db, dec_seq, conv_dim)[:, dec_seq - (CONV_WIDTH - 1):]

    w_r = jnp.pad(w_router, ((0, 0), (0, LANE - n_experts)))
    b_r = jnp.pad(b_router, (0, LANE - n_experts)).reshape(1, LANE)
    x1, e_idx, gates = _mix_ln_router(
        (o_p, o_s), (c_p, c_s), (xp, xs), w_out[:ATTN_WIDTH].astype(BF16),
        w_out[ATTN_WIDTH:].astype(BF16), ln1_g.reshape(1, d), ln1_b.reshape(1, d), w_r, b_r,
        dn_alpha, n_experts)

    row_tok, pos, plan = _route(e_idx[:, :TOP_K], n_experts)
    y_rows = _experts(plan, x1[row_tok], w_gu, b_gu.reshape(n_experts, 1, -1),
                      w_down, b_down.reshape(n_experts, 1, -1))

    comb = (x1, y_rows[pos.T], gates, ln2_g.reshape(1, d), ln2_b.reshape(1, d), dn_alpha)
    y_prompt = _combine_ln(*comb, 0, n_p).reshape(nb, seq_len, d)
    y_sample = _combine_ln(*comb, n_p, n_s).reshape(db, dec_seq, d)
    kv_shape = (N_KV_COMP, N_KV_HEADS, HEAD_DIM)
    win_shape = (win_buf, 2, N_KV_HEADS, HEAD_DIM)
    return (y_prompt, y_sample,
            kv_prompt.reshape((nb, seq_len) + kv_shape), kv_sample.reshape((db, dec_seq) + kv_shape),
            win_prompt.reshape((nb,) + win_shape), win_sample.reshape((db,) + win_shape),
            conv_prompt, conv_sample)


def kernel(x_prompt, x_sample, cache_kv, state_win, state_conv, page_table, w_in, w_out, conv_w, cmp_k_pe, cmp_k_w1, cmp_k_w2, cmp_v_pe, cmp_v_w1, cmp_v_w2, ln1_g, ln1_b, w_router, b_router, w_gu, b_gu, w_down, b_down, ln2_g, ln2_b):
    depth, n_pool, page = cache_kv.shape[:3]
    db, win_buf = state_win.shape[1:3]
    cache_rows = cache_kv.reshape(depth * n_pool, page * N_KV_COMP * N_KV_HEADS, HEAD_DIM)
    win_rows = state_win.reshape(depth, db, win_buf * 2 * N_KV_HEADS, HEAD_DIM)
    y_prompt, y_sample = x_prompt, x_sample
    outs = [[] for _ in range(6)]
    for layer in range(depth):
        res = _layer(y_prompt, y_sample, cache_rows, win_rows[layer], state_conv[layer],
                     page_table + layer * n_pool,
                     w_in[layer], w_out[layer], conv_w[layer],
                     cmp_k_pe[layer], cmp_k_w1[layer], cmp_k_w2[layer],
                     cmp_v_pe[layer], cmp_v_w1[layer], cmp_v_w2[layer],
                     ln1_g[layer], ln1_b[layer], w_router[layer], b_router[layer],
                     w_gu[layer], b_gu[layer], w_down[layer], b_down[layer],
                     ln2_g[layer], ln2_b[layer], depth)
        y_prompt, y_sample = res[0], res[1]
        for lst, val in zip(outs, res[2:]):
            lst.append(val)
    return (y_prompt, y_sample) + tuple(jnp.stack(o) for o in outs)
```

```python
import functools
import math

import numpy as np
import jax
import jax.numpy as jnp
from jax import lax
from jax.experimental import pallas as pl
from jax.experimental.pallas import tpu as pltpu

F32 = jnp.float32
BF16 = jnp.bfloat16
I32 = jnp.int32

HEAD_DIM = 128
N_HEADS = 8
N_KV_HEADS = 2
GQA_REP = N_HEADS // N_KV_HEADS
N_KV_COMP = 4
KV_WIDTH = N_KV_HEADS * HEAD_DIM
ATTN_WIDTH = N_HEADS * HEAD_DIM
CONV_WIDTH = 3
CMP_BLOCK = 32
CMP_STRIDE = 16
CMP_RATIO = CMP_BLOCK // CMP_STRIDE
CMP_HIDDEN = 256
SEL_BLOCK = 64
SEL_TOPK = 8
WINDOW = 512
N_BRANCH = 3
Q_BLOCK = 128
TOP_K = 4
SWIGLU_LIMIT = 7.0
SWIGLU_ALPHA = 1.702
MOE_TM = 256
LN_EPS = 1e-5
NEG = -1e30
BIG = 1e30
LANE = 128
SUBLANE = 8
VMEM_LIMIT = 56 * 1024 * 1024
SEL_CHUNK = 512
SAMPLE_SUB = 2
MASK_M = -2.0 ** 100
AUX_HI = 64
AUX_LO = 65

CB_KV = ATTN_WIDTH // LANE
CB_WIN = CB_KV + N_KV_COMP * N_KV_HEADS
CB_CONV = CB_WIN + 2 * N_KV_HEADS
N_GATE = N_BRANCH * N_HEADS


def _alibi_slopes(n):
    def pow2(m):
        start = 2.0 ** (-8.0 / m)
        return [start ** (i + 1) for i in range(m)]
    if math.log2(n).is_integer():
        s = pow2(n)
    else:
        c = 2 ** math.floor(math.log2(n))
        s = pow2(c) + pow2(2 * c)[0::2][:n - c]
    return [float(np.float32(v)) for v in s]


SLOPES = _alibi_slopes(N_HEADS)


def _pick(n, target, mult=SUBLANE):
    for d in range(min(n, target), 0, -1):
        if n % d == 0 and d % mult == 0:
            return d
    raise ValueError((n, target, mult))


def _cparams(sem):
    return pltpu.CompilerParams(dimension_semantics=sem, vmem_limit_bytes=VMEM_LIMIT)


def _pair_specs(tm, width, n_first_blocks):
    return [pl.BlockSpec((tm, width), lambda i, *_: (jnp.minimum(i, n_first_blocks - 1), 0)),
            pl.BlockSpec((tm, width), lambda i, *_: (jnp.maximum(i - n_first_blocks, 0), 0))]


def _pair_load(first_ref, second_ref, n_first_blocks):
    return jnp.where(pl.program_id(0) < n_first_blocks, first_ref[...], second_ref[...])


def _proj_kernel(xp_ref, xs_ref, w_ref, o_ref, *, n_first):
    x = _pair_load(xp_ref, xs_ref, n_first)
    o_ref[...] = jnp.dot(x.astype(BF16), w_ref[...], preferred_element_type=F32)


def _proj(xp, xs, w, tm, tn):
    k = xp.shape[1]
    n = w.shape[1]
    n_first = xp.shape[0] // tm
    m = xp.shape[0] + xs.shape[0]
    return pl.pallas_call(
        functools.partial(_proj_kernel, n_first=n_first),
        out_shape=jax.ShapeDtypeStruct((m, n), F32),
        grid=(m // tm, n // tn),
        in_specs=_pair_specs(tm, k, n_first) + [pl.BlockSpec((k, tn), lambda i, j: (0, j))],
        out_specs=pl.BlockSpec((tm, tn), lambda i, j: (i, j)),
        compiler_params=_cparams(("parallel", "arbitrary")),
    )(xp, xs, w)


def _pe_bias_kernel(pe_ref, w1_ref, o_ref):
    o_ref[...] = jnp.dot(pe_ref[...].astype(BF16), w1_ref[...], preferred_element_type=F32)


def _pe_bias(pe, w1_bf):
    flat = jnp.broadcast_to(pe.reshape(1, -1), (SUBLANE, pe.size))
    return pl.pallas_call(
        _pe_bias_kernel,
        out_shape=jax.ShapeDtypeStruct((SUBLANE, CMP_HIDDEN), F32),
    )(flat, w1_bf)


S_PITCH = 136


def _scatter_rows(raw, plane, base, x):
    per = SUBLANE
    for j in range(x.shape[0] // per):
        c, s0 = divmod(j * per, CMP_STRIDE)
        raw[plane, pl.ds(s0 * S_PITCH + base + c, per, stride=S_PITCH), :] = x[j * per:(j + 1) * per]


def _compress_core(raw, weights, n_chunk, plane0=0):
    assert n_chunk <= S_PITCH
    outs = []
    for kv, (w1_ref, b_ref, w2_ref) in enumerate(weights):
        acc = jnp.zeros((N_KV_HEADS * n_chunk, CMP_RATIO * CMP_HIDDEN), F32)
        for sp in range(CMP_STRIDE // 2):
            per_g = []
            for g in range(N_KV_HEADS):
                halves = [raw[plane0 + kv * N_KV_HEADS + g, pl.ds(s * S_PITCH, n_chunk), :]
                          for s in (2 * sp, 2 * sp + 1)]
                per_g.append(jnp.concatenate(halves, axis=1))
            lhs = jnp.concatenate(per_g, axis=0).astype(BF16)
            acc = acc + jnp.dot(lhs, w1_ref[pl.ds(sp * 2 * HEAD_DIM, 2 * HEAD_DIM), :],
                                preferred_element_type=F32)
        per_kv = []
        for g in range(N_KV_HEADS):
            part0 = acc[g * n_chunk:(g + 1) * n_chunk, :CMP_HIDDEN]
            part1 = acc[g * n_chunk:(g + 1) * n_chunk, CMP_HIDDEN:]
            part1 = jnp.concatenate([part1[1:], part1[:1]], axis=0)
            pre = (b_ref[0:1, :] + part0) + part1
            per_kv.append(jnp.dot(jax.nn.gelu(pre).astype(BF16), w2_ref[...],
                                  preferred_element_type=F32))
        outs.append(per_kv)
    return outs


def _compress_kernel(ids_ref, *refs, n_pages):
    pages = refs[:n_pages]
    w1k_ref, w1v_ref, bk_ref, bv_ref, w2k_ref, w2v_ref, kc_ref, vc_ref, raw = refs[n_pages:]
    n_chunk = n_pages * (LANE // CMP_STRIDE)
    for cb in range(2 * N_KV_HEADS):
        for p in range(n_pages):
            _scatter_rows(raw, cb, p * (LANE // CMP_STRIDE),
                          pages[p][0, :, cb * HEAD_DIM:(cb + 1) * HEAD_DIM])
    kc, vc = _compress_core(raw, ((w1k_ref, bk_ref, w2k_ref), (w1v_ref, bv_ref, w2v_ref)), n_chunk)
    for g in range(N_KV_HEADS):
        kc_ref[0, g] = kc[g]
        vc_ref[0, g] = vc[g]


def _compress(page_ids, src, col_block, w1k, w1v, bk, bv, w2k, w2v):
    nb, n_pages = page_ids.shape
    n_chunk = n_pages * (LANE // CMP_STRIDE)
    width = 2 * KV_WIDTH

    def page_map(p):
        return lambda b, ids: (ids[b, p], 0, col_block)

    full = lambda shape: pl.BlockSpec(shape, lambda b, ids: (0,) * len(shape))
    in_specs = [pl.BlockSpec((1, LANE, width), page_map(p)) for p in range(n_pages)]
    in_specs += [full(w1k.shape), full(w1v.shape), full(bk.shape), full(bv.shape),
                 full(w2k.shape), full(w2v.shape)]
    out_spec = pl.BlockSpec((1, N_KV_HEADS, n_chunk, HEAD_DIM), lambda b, ids: (b, 0, 0, 0))
    out_sd = jax.ShapeDtypeStruct((nb, N_KV_HEADS, n_chunk, HEAD_DIM), F32)
    return pl.pallas_call(
        functools.partial(_compress_kernel, n_pages=n_pages),
        out_shape=(out_sd, out_sd),
        grid_spec=pltpu.PrefetchScalarGridSpec(
            num_scalar_prefetch=1, grid=(nb,), in_specs=in_specs,
            out_specs=(out_spec, out_spec),
            scratch_shapes=[pltpu.VMEM((2 * N_KV_HEADS, CMP_STRIDE * S_PITCH, HEAD_DIM), F32)]),
        compiler_params=_cparams(("parallel",)),
    )(page_ids, *([src] * n_pages), w1k, w1v, bk, bv, w2k, w2v)


def _nt_dot(a, b):
    return lax.dot_general(a, b, (((1,), (1,)), ((), ())), preferred_element_type=F32)


def _masked_softmax(s, mask):
    s = jnp.where(mask, s, NEG)
    e = jnp.where(mask, jnp.exp(s - jnp.max(s, axis=-1, keepdims=True)), 0.0)
    den = jnp.sum(e, axis=-1, keepdims=True)
    return e / jnp.where(den > 0, den, 1.0)


def _stack_heads(q, tq):
    q = q * (HEAD_DIM ** -0.5)
    return jnp.concatenate([q[:, r * HEAD_DIM:(r + 1) * HEAD_DIM] for r in range(GQA_REP)],
                           axis=0).astype(BF16)


def _row_consts(g, tq, p0):
    rows = GQA_REP * tq
    ridx = lax.broadcasted_iota(I32, (rows, 1), 0)
    slope = jnp.zeros((rows, 1), F32)
    for r in range(GQA_REP):
        slope = jnp.where(ridx // tq == r, SLOPES[g * GQA_REP + r], slope)
    t = p0 + ridx % tq
    return slope, t


def _cmp_branch(q4, kc, vc, slope, t, tq):
    n_chunk = kc.shape[0]
    s = _nt_dot(q4, kc.astype(BF16))
    c_end = lax.broadcasted_iota(I32, (1, n_chunk), 1) * CMP_STRIDE + (CMP_BLOCK - 1)
    rel = t - c_end
    s = s - slope * rel.astype(F32)
    p = _masked_softmax(s, rel >= 0)
    o_c = jnp.dot(p.astype(BF16), vc.astype(BF16), preferred_element_type=F32)
    p_sum = p[0:tq]
    for r in range(1, GQA_REP):
        p_sum = p_sum + p[r * tq:(r + 1) * tq]
    return o_c, p_sum


def _pos_aux(pos):
    pos = np.asarray(pos).reshape(-1, 1)
    lane = np.arange(LANE).reshape(1, -1)
    blk = pos // SEL_BLOCK
    aux = np.where(lane == blk, MASK_M, 0.0)
    aux = np.where(lane == AUX_HI, blk * SEL_BLOCK, aux)
    aux = np.where(lane == AUX_LO, pos % SEL_BLOCK, aux)
    assert blk.max() < AUX_HI
    return jnp.asarray(aux, F32).astype(BF16)


def _query_aux(slope, notsel4):
    lane = lax.broadcasted_iota(I32, (slope.shape[0], LANE), 1)
    base = jnp.zeros((slope.shape[0], LANE), F32) if notsel4 is None else notsel4
    return jnp.where((lane == AUX_HI) | (lane == AUX_LO), slope, base).astype(BF16)


def _not_selected(p_sum, t_row, n_sel):
    tq, n_chunk = p_sum.shape
    nj = -(-n_sel // SUBLANE) * SUBLANE
    cj = lax.broadcasted_iota(I32, (nj, n_chunk), 0) * SEL_BLOCK
    cn = lax.broadcasted_iota(I32, (nj, n_chunk), 1) * CMP_STRIDE
    overlap = jnp.maximum(jnp.minimum(cn + CMP_BLOCK, cj + SEL_BLOCK) - jnp.maximum(cn, cj), 0)
    overlap = overlap.astype(F32).astype(BF16)
    p_hi = p_sum.astype(BF16)
    p_lo = (p_sum - p_hi.astype(F32)).astype(BF16)
    imp = _nt_dot(overlap, p_hi) + _nt_dot(overlap, p_lo)
    row = lax.broadcasted_iota(I32, (nj, tq), 0)
    cur = t_row // SEL_BLOCK
    valid = (row <= cur) & (row < n_sel)
    forced = (row == 0) | (row == cur) | (row == cur - 1)
    score = jnp.where(valid, jnp.where(forced, BIG, imp), -1.0)
    notsel = jnp.ones((nj, tq), F32)
    for _ in range(min(SEL_TOPK, n_sel)):
        m = jnp.max(score, axis=0, keepdims=True)
        idx = jnp.min(jnp.where(score == m, row, nj), axis=0, keepdims=True)
        hit = row == idx
        notsel = jnp.where(hit & (m >= 0.0), 0.0, notsel)
        score = jnp.where(hit, -1.0, score)
    notsel = jnp.concatenate([notsel, jnp.ones((LANE - nj, tq), F32)], axis=0)
    if tq < LANE:
        notsel = jnp.concatenate([notsel, jnp.ones((LANE, LANE - tq), F32)], axis=1)
    return notsel.T


def _rep_rows(x):
    return jnp.concatenate([x] * GQA_REP, axis=0)


def _softmax_rows(s):
    e = jnp.exp(s - jnp.max(s, axis=-1, keepdims=True))
    return e / jnp.sum(e, axis=-1, keepdims=True)


def _gated_sum(sig, g, o_c, o_s, o_w, tq):
    outs = []
    for r in range(GQA_REP):
        h = g * GQA_REP + r
        sl = slice(r * tq, (r + 1) * tq)
        g0 = sig[:, N_BRANCH * h + 0:N_BRANCH * h + 1]
        g1 = sig[:, N_BRANCH * h + 1:N_BRANCH * h + 2]
        g2 = sig[:, N_BRANCH * h + 2:N_BRANCH * h + 3]
        outs.append(g0 * o_c[sl] + g1 * o_s[sl] + g2 * o_w[sl])
    return outs


def _prompt_attn_kernel(q_ref, gate_ref, kc_ref, vc_ref, ks_ref, vs_ref, kw_ref, vw_ref, aux_ref,
                        o_ref, ksb, vsb, kwb, vwb, *, tq, seq_len):
    i = pl.program_id(1)
    n_sel = -(-seq_len // SEL_BLOCK)
    heads = range(N_KV_HEADS)

    @pl.when(i == 0)
    def _():
        for g in heads:
            col = slice(g * HEAD_DIM, (g + 1) * HEAD_DIM)
            ksb[g, :, 0:HEAD_DIM] = ks_ref[:, col].astype(BF16)
            ksb[g, :, HEAD_DIM:] = aux_ref[...]
            kwb[g, :, 0:HEAD_DIM] = kw_ref[:, col].astype(BF16)
            kwb[g, :, HEAD_DIM:] = aux_ref[...]
            vsb[g] = vs_ref[:, col].astype(BF16)
            vwb[g] = vw_ref[:, col].astype(BF16)

    p0 = i * tq
    rows = GQA_REP * tq
    t_row = p0 + lax.broadcasted_iota(I32, (1, tq), 1)
    q_sel, q_win, o_c = [], [], []
    for g in heads:
        q4 = _stack_heads(q_ref[:, g * GQA_REP * HEAD_DIM:(g + 1) * GQA_REP * HEAD_DIM], tq)
        slope, t = _row_consts(g, tq, p0)
        oc, p_sum = _cmp_branch(q4, kc_ref[0, g], vc_ref[0, g], slope, t, tq)
        notsel4 = _rep_rows(_not_selected(p_sum, t_row, n_sel))
        q_sel.append(jnp.concatenate([q4, _query_aux(slope, notsel4)], axis=1))
        q_win.append(jnp.concatenate([q4, _query_aux(slope, None)], axis=1))
        o_c.append(oc)
    _, t = _row_consts(0, tq, p0)

    def step(g, c, carry, causal):
        m, l, acc = carry
        start = pl.multiple_of(c * SEL_CHUNK, SEL_CHUNK)
        s = _nt_dot(q_sel[g], ksb[g, pl.ds(start, SEL_CHUNK), :])
        if causal:
            pos = start + lax.broadcasted_iota(I32, (1, SEL_CHUNK), 1)
            s = jnp.where(pos <= t, s, NEG)
        m_new = jnp.maximum(m, jnp.max(s, axis=-1, keepdims=True))
        a = jnp.exp(m - m_new)
        e = jnp.exp(s - m_new)
        l = a * l + jnp.sum(e, axis=-1, keepdims=True)
        acc = a * acc + jnp.dot(e.astype(BF16), vsb[g, pl.ds(start, SEL_CHUNK), :],
                                preferred_element_type=F32)
        return m_new, l, acc

    def chunk(c, carries, causal):
        return tuple(step(g, c, carries[g], causal) for g in heads)

    last = (p0 + tq - 1) // SEL_CHUNK
    init = tuple((jnp.full((rows, 1), NEG, F32), jnp.zeros((rows, 1), F32),
                  jnp.zeros((rows, HEAD_DIM), F32)) for _ in heads)
    carries = lax.fori_loop(0, last, lambda c, cr: chunk(c, cr, False), init)
    carries = chunk(last, carries, True)

    n_win = WINDOW + tq
    start = pl.multiple_of(jnp.maximum(p0 - WINDOW, 0), tq)
    rel = ((p0 - start) + lax.broadcasted_iota(I32, (tq, n_win), 0)
           - lax.broadcasted_iota(I32, (tq, n_win), 1))
    band = _rep_rows(jnp.where((rel >= 0) & (rel < WINDOW), 0.0, NEG))
    sig = jax.nn.sigmoid(gate_ref[...])
    for g in heads:
        _, l, acc = carries[g]
        o_s = acc / l
        s = _nt_dot(q_win[g], kwb[g, pl.ds(start, n_win), :]) + band
        o_w = jnp.dot(_softmax_rows(s).astype(BF16), vwb[g, pl.ds(start, n_win), :],
                      preferred_element_type=F32)
        outs = _gated_sum(sig, g, o_c[g], o_s, o_w, tq)
        for r in range(GQA_REP):
            h = g * GQA_REP + r
            o_ref[:, h * HEAD_DIM:(h + 1) * HEAD_DIM] = outs[r]


def _prompt_attention(proj, kc, vc, n_batch, seq_len, n_cols_blocks):
    tq = Q_BLOCK
    nqb = seq_len // tq
    gate_cb = n_cols_blocks - 1
    aux = _pos_aux(np.arange(seq_len))
    kv_spec = lambda cb: pl.BlockSpec((seq_len, KV_WIDTH), lambda b, i: (b, cb * LANE // KV_WIDTH))
    c_spec = pl.BlockSpec((1,) + kc.shape[1:], lambda b, i: (b, 0, 0, 0))
    bf = lambda width: pltpu.VMEM((N_KV_HEADS, seq_len, width), BF16)
    return pl.pallas_call(
        functools.partial(_prompt_attn_kernel, tq=tq, seq_len=seq_len),
        out_shape=jax.ShapeDtypeStruct((n_batch * seq_len, ATTN_WIDTH), F32),
        grid=(n_batch, nqb),
        in_specs=[pl.BlockSpec((tq, ATTN_WIDTH), lambda b, i: (b * nqb + i, 0)),
                  pl.BlockSpec((tq, LANE), lambda b, i: (b * nqb + i, gate_cb)),
                  c_spec, c_spec,
                  kv_spec(CB_KV + 2 * N_KV_HEADS), kv_spec(CB_KV + 3 * N_KV_HEADS),
                  kv_spec(CB_WIN), kv_spec(CB_WIN + N_KV_HEADS),
                  pl.BlockSpec(aux.shape, lambda b, i: (0, 0))],
        out_specs=pl.BlockSpec((tq, ATTN_WIDTH), lambda b, i: (b * nqb + i, 0)),
        scratch_shapes=[bf(2 * HEAD_DIM), bf(HEAD_DIM), bf(2 * HEAD_DIM), bf(HEAD_DIM)],
        compiler_params=_cparams(("parallel", "arbitrary")),
    )(proj, proj, kc, vc, proj, proj, proj, proj, aux)


def _sample_attn_kernel(pt_ref, *refs, n_pages, n_sub, tq, past_len, win_buf):
    refs_rest = refs[n_sub * n_pages:]
    for u in range(n_sub):
        _sample_attn_one(u, refs[u * n_pages:(u + 1) * n_pages], refs_rest, tq, past_len, win_buf)


def _sample_attn_one(u, pages, refs, tq, past_len, win_buf):
    (q_ref, kvn_ref, winn_ref, gate_ref, win_ref, w1k_ref, w1v_ref, bk_ref, bv_ref, w2k_ref,
     w2v_ref, saux_ref, waux_ref, o_ref, kvo_ref, wino_ref, raw) = refs
    rs = slice(u * tq, (u + 1) * tq)
    plane0 = u * 2 * N_KV_HEADS
    n_pages = len(pages)
    seq_len = past_len + tq
    n_sel = -(-seq_len // SEL_BLOCK)
    n_planes = N_KV_COMP * N_KV_HEADS
    n_wplanes = 2 * N_KV_HEADS
    page = pages[0].shape[1] // n_planes
    n_chunk = past_len // CMP_STRIDE
    zpad = jnp.zeros((LANE - tq, HEAD_DIM), BF16)

    for c in range(n_planes):
        kvo_ref[pl.ds(u * tq * n_planes + c, tq, stride=n_planes), :] = kvn_ref[rs, c * HEAD_DIM:(c + 1) * HEAD_DIM]
    keep = (win_buf - tq) * n_wplanes
    wino_ref[u, 0:keep, :] = win_ref[u, tq * n_wplanes:win_buf * n_wplanes, :]
    for c in range(n_wplanes):
        wino_ref[u, pl.ds(keep + c, tq, stride=n_wplanes), :] = winn_ref[rs, c * HEAD_DIM:(c + 1) * HEAD_DIM]

    for cb in range(2 * N_KV_HEADS):
        for p in range(n_pages):
            _scatter_rows(raw, plane0 + cb, p * (page // CMP_STRIDE),
                          pages[p][0, pl.ds(cb, page, stride=n_planes), :])
    kc, vc = _compress_core(raw, ((w1k_ref, bk_ref, w2k_ref), (w1v_ref, bv_ref, w2v_ref)), n_chunk,
                            plane0)

    sig = jax.nn.sigmoid(gate_ref[rs, :])
    q4s, consts, o_cs, p_sums = [], [], [], []
    for g in range(N_KV_HEADS):
        q4 = _stack_heads(q_ref[rs, g * GQA_REP * HEAD_DIM:(g + 1) * GQA_REP * HEAD_DIM], tq)
        slope, t = _row_consts(g, tq, past_len)
        o_c, p_sum = _cmp_branch(q4, kc[g], vc[g], slope, t, tq)
        q4s.append(q4); consts.append((slope, t)); o_cs.append(o_c); p_sums.append(p_sum)
    t_row = past_len + lax.broadcasted_iota(I32, (1, N_KV_HEADS * tq), 1) % tq
    notsel = _not_selected(jnp.concatenate(p_sums, axis=0), t_row, n_sel)

    for g in range(N_KV_HEADS):
        q4, (slope, t), o_c = q4s[g], consts[g], o_cs[g]
        q_sel = jnp.concatenate([q4, _query_aux(slope, _rep_rows(notsel[g * tq:(g + 1) * tq]))], axis=1)
        q_win = jnp.concatenate([q4, _query_aux(slope, None)], axis=1)

        kpl = 2 * N_KV_HEADS + g
        vpl = 3 * N_KV_HEADS + g
        k = jnp.concatenate(
            [p[0, pl.ds(kpl, page, stride=n_planes), :].astype(BF16) for p in pages]
            + [kvn_ref[rs, kpl * HEAD_DIM:(kpl + 1) * HEAD_DIM].astype(BF16), zpad], axis=0)
        v = jnp.concatenate(
            [p[0, pl.ds(vpl, page, stride=n_planes), :].astype(BF16) for p in pages]
            + [kvn_ref[rs, vpl * HEAD_DIM:(vpl + 1) * HEAD_DIM].astype(BF16), zpad], axis=0)
        pos = lax.broadcasted_iota(I32, (1, k.shape[0]), 1)
        s = _nt_dot(q_sel, jnp.concatenate([k, saux_ref[...]], axis=1))
        p = _softmax_rows(jnp.where(pos <= t, s, NEG))
        o_s = jnp.dot(p.astype(BF16), v, preferred_element_type=F32)

        kw = jnp.concatenate([win_ref[u, pl.ds(g, win_buf, stride=n_wplanes), :].astype(BF16),
                              winn_ref[rs, g * HEAD_DIM:(g + 1) * HEAD_DIM].astype(BF16), zpad],
                             axis=0)
        vp = N_KV_HEADS + g
        vw = jnp.concatenate([win_ref[u, pl.ds(vp, win_buf, stride=n_wplanes), :].astype(BF16),
                              winn_ref[rs, vp * HEAD_DIM:(vp + 1) * HEAD_DIM].astype(BF16), zpad],
                             axis=0)
        kpos = (past_len - win_buf) + lax.broadcasted_iota(I32, (1, kw.shape[0]), 1)
        rel = t - kpos
        s = _nt_dot(q_win, jnp.concatenate([kw, waux_ref[...]], axis=1))
        p = _softmax_rows(jnp.where((rel >= 0) & (rel < WINDOW), s, NEG))
        o_w = jnp.dot(p.astype(BF16), vw, preferred_element_type=F32)

        outs = _gated_sum(sig, g, o_c, o_s, o_w, tq)
        for r in range(GQA_REP):
            h = g * GQA_REP + r
            o_ref[rs, h * HEAD_DIM:(h + 1) * HEAD_DIM] = outs[r]


def _sample_attention(page_table, cache, proj, state_win, cmp_w, n_prompt_rows, tq, n_cols_blocks):
    db, n_pages = page_table.shape
    n_planes = N_KV_COMP * N_KV_HEADS
    n_wplanes = 2 * N_KV_HEADS
    page = cache.shape[1] // n_planes
    past_len = n_pages * page
    win_buf = state_win.shape[1] // n_wplanes
    n_sub = SAMPLE_SUB if db % SAMPLE_SUB == 0 and (n_prompt_rows // tq) % SAMPLE_SUB == 0 else 1
    rows = n_sub * tq
    r0 = n_prompt_rows // rows
    gate_cb = n_cols_blocks - 1
    sel_aux = _pos_aux(np.arange(past_len + LANE))
    win_aux = _pos_aux(past_len - win_buf + np.arange(win_buf + LANE))

    def page_map(u, p):
        return lambda b, pt: (pt[n_sub * b + u, p], 0, 0)

    full = lambda a: pl.BlockSpec(a.shape, lambda b, pt: (0,) * a.ndim)
    in_specs = [pl.BlockSpec((1,) + cache.shape[1:], page_map(u, p))
                for u in range(n_sub) for p in range(n_pages)]
    in_specs += [
        pl.BlockSpec((rows, ATTN_WIDTH), lambda b, pt: (r0 + b, 0)),
        pl.BlockSpec((rows, N_KV_COMP * KV_WIDTH), lambda b, pt: (r0 + b, 1)),
        pl.BlockSpec((rows, 2 * KV_WIDTH), lambda b, pt: (r0 + b, CB_WIN * LANE // (2 * KV_WIDTH))),
        pl.BlockSpec((rows, LANE), lambda b, pt: (r0 + b, gate_cb)),
        pl.BlockSpec((n_sub,) + state_win.shape[1:], lambda b, pt: (b, 0, 0)),
    ] + [full(w) for w in cmp_w] + [full(sel_aux), full(win_aux)]
    return pl.pallas_call(
        functools.partial(_sample_attn_kernel, n_pages=n_pages, n_sub=n_sub, tq=tq,
                          past_len=past_len, win_buf=win_buf),
        out_shape=(jax.ShapeDtypeStruct((db * tq, ATTN_WIDTH), F32),
                   jax.ShapeDtypeStruct((db * tq * n_planes, HEAD_DIM), F32),
                   jax.ShapeDtypeStruct(state_win.shape, F32)),
        grid_spec=pltpu.PrefetchScalarGridSpec(
            num_scalar_prefetch=1, grid=(db // n_sub,), in_specs=in_specs,
            out_specs=(pl.BlockSpec((rows, ATTN_WIDTH), lambda b, pt: (b, 0)),
                       pl.BlockSpec((rows * n_planes, HEAD_DIM), lambda b, pt: (b, 0)),
                       pl.BlockSpec((n_sub,) + state_win.shape[1:], lambda b, pt: (b, 0, 0))),
            scratch_shapes=[pltpu.VMEM((n_sub * 2 * N_KV_HEADS, CMP_STRIDE * S_PITCH, HEAD_DIM),
                                       F32)]),
        compiler_params=_cparams(("parallel",)),
    )(page_table, *([cache] * (n_sub * n_pages)), proj, proj, proj, proj, state_win, *cmp_w, sel_aux, win_aux)


def _interleave_kernel(x_ref, o_ref, *, n_planes):
    rows = x_ref.shape[0]
    for c in range(n_planes):
        o_ref[pl.ds(c, rows, stride=n_planes), :] = x_ref[:, c * LANE:(c + 1) * LANE]


def _interleave(proj, n_blocks, block_rows, n_planes, row_block_of, col_block):
    return pl.pallas_call(
        functools.partial(_interleave_kernel, n_planes=n_planes),
        out_shape=jax.ShapeDtypeStruct((n_blocks * block_rows * n_planes, LANE), F32),
        grid=(n_blocks,),
        in_specs=[pl.BlockSpec((block_rows, n_planes * LANE), lambda i: (row_block_of(i), col_block))],
        out_specs=pl.BlockSpec((block_rows * n_planes, LANE), lambda i: (i, 0)),
        compiler_params=_cparams(("parallel",)),
    )(proj)


def _conv_kernel(*refs, seq_len, has_state):
    if has_state:
        b_ref, c_ref, h_ref, w_ref, s1_ref, s2_ref, y_ref, u_ref = refs
    else:
        b_ref, c_ref, h_ref, w_ref, y_ref, u_ref = refs
    u = c_ref[...] * h_ref[...]
    rows = u.shape[0]
    t = lax.broadcasted_iota(I32, (rows, 1), 0) % seq_len
    p1 = jnp.where(t >= 1, pltpu.roll(u, 1, 0), 0.0)
    p2 = jnp.where(t >= 2, pltpu.roll(u, 2, 0), 0.0)
    if has_state:
        p1 = p1 + s1_ref[...]
        p2 = p2 + s2_ref[...]
    y = w_ref[0:1, :] * p2
    y = y + w_ref[1:2, :] * p1
    y = y + w_ref[2:3, :] * u
    y_ref[...] = b_ref[...] * y
    u_ref[...] = u[rows - u_ref.shape[0]:, :]


def _short_conv(proj, conv_w, row_block0, n_blocks, block_rows, seq_len, state=None):
    conv_dim = conv_w.shape[1]
    n_ct = conv_dim // LANE
    tail = SUBLANE if state is None else block_rows
    spec = lambda cb: pl.BlockSpec((block_rows, LANE), lambda r, c: (row_block0 + r, cb + c))
    in_specs = [spec(CB_CONV), spec(CB_CONV + n_ct), spec(CB_CONV + 2 * n_ct),
                pl.BlockSpec((CONV_WIDTH, LANE), lambda r, c: (0, c))]
    args = [proj, proj, proj, conv_w]
    if state is not None:
        st_spec = pl.BlockSpec((block_rows, LANE), lambda r, c: (r, c))
        in_specs += [st_spec, st_spec]
        args += list(state)
    return pl.pallas_call(
        functools.partial(_conv_kernel, seq_len=seq_len, has_state=state is not None),
        out_shape=(jax.ShapeDtypeStruct((n_blocks * block_rows, conv_dim), F32),
                   jax.ShapeDtypeStruct((n_blocks * tail, conv_dim), F32)),
        grid=(n_blocks, n_ct),
        in_specs=in_specs,
        out_specs=(pl.BlockSpec((block_rows, LANE), lambda r, c: (r, c)),
                   pl.BlockSpec((tail, LANE), lambda r, c: (r, c))),
        compiler_params=_cparams(("parallel", "parallel")),
    )(*args)


def _layer_norm(x, g, b):
    mu = jnp.mean(x, axis=-1, keepdims=True)
    var = jnp.mean(jnp.square(x - mu), axis=-1, keepdims=True)
    return (x - mu) * lax.rsqrt(var + LN_EPS) * g + b


def _mix_kernel(oa_p, oa_s, oc_p, oc_s, x_p, x_s, wa_ref, wc_ref, g_ref, b_ref, wr_ref, br_ref,
                x1_ref, e_ref, gt_ref, *, dn_alpha, n_experts, n_first):
    o_attn = _pair_load(oa_p, oa_s, n_first)
    o_conv = _pair_load(oc_p, oc_s, n_first)
    mix = jnp.dot(o_attn.astype(BF16), wa_ref[...], preferred_element_type=F32)
    mix = mix + jnp.dot(o_conv.astype(BF16), wc_ref[...], preferred_element_type=F32)
    x1 = _layer_norm(dn_alpha * _pair_load(x_p, x_s, n_first) + mix, g_ref[...], b_ref[...])
    x1_ref[...] = x1
    x_hi = x1.astype(BF16)
    x_lo = (x1 - x_hi.astype(F32)).astype(BF16)
    w_r = wr_ref[...]
    w_hi = w_r.astype(BF16)
    w_lo = (w_r - w_hi.astype(F32)).astype(BF16)
    logits = (jnp.dot(x_hi, w_hi, preferred_element_type=F32)
              + jnp.dot(x_lo, w_hi, preferred_element_type=F32)
              + jnp.dot(x_hi, w_lo, preferred_element_type=F32)) + br_ref[...]
    tm = logits.shape[0]
    lane = lax.broadcasted_iota(I32, (tm, LANE), 1)
    score = jnp.where(lane < n_experts, logits, -jnp.inf)
    e_out = jnp.zeros((tm, LANE), I32)
    v_out = jnp.zeros((tm, LANE), F32)
    v0 = None
    den = jnp.zeros((tm, 1), F32)
    for k in range(TOP_K):
        m = jnp.max(score, axis=-1, keepdims=True)
        idx = jnp.min(jnp.where(score == m, lane, LANE), axis=-1, keepdims=True)
        if k == 0:
            v0 = m
        ex = jnp.exp(m - v0)
        den = den + ex
        e_out = jnp.where(lane == k, idx, e_out)
        v_out = jnp.where(lane == k, ex, v_out)
        score = jnp.where(lane == idx, -jnp.inf, score)
    e_ref[...] = e_out
    gt_ref[...] = v_out / den


def _mix_ln_router(o_attn, o_conv, x, w_out_a, w_out_c, ln_g, ln_b, w_r, b_r, dn_alpha, n_experts):
    n = x[0].shape[0] + x[1].shape[0]
    d = x[0].shape[1]
    tm = _pick(x[1].shape[0], 256)
    n_first = x[0].shape[0] // tm
    row = lambda w: pl.BlockSpec((tm, w), lambda i: (i, 0))
    full = lambda a: pl.BlockSpec(a.shape, lambda i: (0, 0))
    pairs = lambda p: _pair_specs(tm, p[0].shape[1], n_first)
    return pl.pallas_call(
        functools.partial(_mix_kernel, dn_alpha=dn_alpha, n_experts=n_experts, n_first=n_first),
        out_shape=(jax.ShapeDtypeStruct((n, d), F32),
                   jax.ShapeDtypeStruct((n, LANE), I32), jax.ShapeDtypeStruct((n, LANE), F32)),
        grid=(n // tm,),
        in_specs=pairs(o_attn) + pairs(o_conv) + pairs(x) + [
            full(w_out_a), full(w_out_c), full(ln_g), full(ln_b), full(w_r), full(b_r)],
        out_specs=(row(d), row(LANE), row(LANE)),
        compiler_params=_cparams(("parallel",)),
    )(*o_attn, *o_conv, *x, w_out_a, w_out_c, ln_g, ln_b, w_r, b_r)


def _weight_pipeline(f, m, nf, chg_ref, nxt_ref, wrap_ref, cnt_ref, copies, cast):
    @pl.when((f == 0) & (m == 0))
    def _():
        cnt_ref[0] = 0
        for c in copies(nxt_ref[0], 0, 0):
            c.start()

    @pl.when(chg_ref[m] == 1)
    def _():
        slot = cnt_ref[0] % 2
        f_next = f + wrap_ref[m]

        @pl.when(f_next < nf)
        def _():
            for c in copies(nxt_ref[m + 1], f_next, 1 - slot):
                c.start()

        for c in copies(0, 0, slot):
            c.wait()
        cast(slot)
        cnt_ref[0] = cnt_ref[0] + 1


def _gate_up_kernel(be_ref, br_ref, chg_ref, nxt_ref, wrap_ref, x_ref, w_hbm, bg_ref, bu_ref, a_ref,
                    wbuf, wgb, wub, sem, cnt_ref, *, nf, tf):
    f = pl.program_id(0)
    m = pl.program_id(1)
    rows = br_ref[m]
    half = MOE_TM // 2

    def copies(e, fi, slot):
        return [pltpu.make_async_copy(
            w_hbm.at[e, :, pl.ds(pl.multiple_of((part * nf + fi) * tf, tf), tf)],
            wbuf.at[slot, part], sem.at[slot, part]) for part in range(2)]

    def cast(slot):
        wgb[...] = wbuf[slot, 0].astype(BF16)
        wub[...] = wbuf[slot, 1].astype(BF16)

    _weight_pipeline(f, m, nf, chg_ref, nxt_ref, wrap_ref, cnt_ref, copies, cast)

    def act(x):
        hg = jnp.dot(x, wgb[...], preferred_element_type=F32) + bg_ref[0]
        hu = jnp.dot(x, wub[...], preferred_element_type=F32) + bu_ref[0]
        hg = jnp.minimum(hg, SWIGLU_LIMIT)
        hu = jnp.clip(hu, -SWIGLU_LIMIT, SWIGLU_LIMIT)
        return (hg * jax.nn.sigmoid(SWIGLU_ALPHA * hg) * (hu + 1.0)).astype(BF16)

    @pl.when(rows > half)
    def _():
        a_ref[...] = act(x_ref[...].astype(BF16))

    @pl.when((rows > 0) & (rows <= half))
    def _():
        a_ref[0:half, :] = act(x_ref[0:half, :].astype(BF16))
        a_ref[half:, :] = jnp.zeros((MOE_TM - half, a_ref.shape[1]), BF16)

    @pl.when(rows == 0)
    def _():
        a_ref[...] = jnp.zeros_like(a_ref)


def _down_kernel(be_ref, br_ref, chg_ref, nxt_ref, wrap_ref, a_ref, w_hbm, b_ref, y_ref,
                 wbuf, wb, sem, cnt_ref, *, nj, tn):
    j = pl.program_id(0)
    m = pl.program_id(1)
    rows = br_ref[m]
    half = MOE_TM // 2

    def copies(e, ji, slot):
        return [pltpu.make_async_copy(
            w_hbm.at[e, :, pl.ds(pl.multiple_of(ji * tn, tn), tn)], wbuf.at[slot], sem.at[slot])]

    def cast(slot):
        wb[...] = wbuf[slot].astype(BF16)

    _weight_pipeline(j, m, nj, chg_ref, nxt_ref, wrap_ref, cnt_ref, copies, cast)

    @pl.when(rows > half)
    def _():
        y_ref[...] = jnp.dot(a_ref[...], wb[...], preferred_element_type=F32) + b_ref[0]

    @pl.when((rows > 0) & (rows <= half))
    def _():
        y_ref[0:half, :] = jnp.dot(a_ref[0:half, :], wb[...], preferred_element_type=F32) + b_ref[0]
        y_ref[half:, :] = jnp.zeros((MOE_TM - half, y_ref.shape[1]), F32)

    @pl.when(rows == 0)
    def _():
        y_ref[...] = jnp.zeros_like(y_ref)


def _experts(plan, xr, w_gu, b_gu, w_down, b_down):
    block_e, block_rows, chg, nxt, wrap = plan
    rows, d = xr.shape
    d_ff = w_down.shape[1]
    n_blocks = rows // MOE_TM
    tf = _pick(d_ff, 1024, LANE)
    nf = d_ff // tf
    a = pl.pallas_call(
        functools.partial(_gate_up_kernel, nf=nf, tf=tf),
        out_shape=jax.ShapeDtypeStruct((rows, d_ff), BF16),
        grid_spec=pltpu.PrefetchScalarGridSpec(
            num_scalar_prefetch=5, grid=(nf, n_blocks),
            in_specs=[pl.BlockSpec((MOE_TM, d), lambda f, m, be, *_: (m, 0)),
                      pl.BlockSpec(memory_space=pl.ANY),
                      pl.BlockSpec((1, 1, tf), lambda f, m, be, *_: (be[m], 0, f)),
                      pl.BlockSpec((1, 1, tf), lambda f, m, be, *_: (be[m], 0, nf + f))],
            out_specs=pl.BlockSpec((MOE_TM, tf), lambda f, m, be, *_: (m, f)),
            scratch_shapes=[pltpu.VMEM((2, 2, d, tf), F32), pltpu.VMEM((d, tf), BF16),
                            pltpu.VMEM((d, tf), BF16), pltpu.SemaphoreType.DMA((2, 2)),
                            pltpu.SMEM((1,), I32)]),
        compiler_params=_cparams(("arbitrary", "arbitrary")),
    )(block_e, block_rows, chg, nxt, wrap, xr, w_gu, b_gu, b_gu)
    tn = _pick(d, 2048, LANE)
    nj = d // tn
    return pl.pallas_call(
        functools.partial(_down_kernel, nj=nj, tn=tn),
        out_shape=jax.ShapeDtypeStruct((rows, d), F32),
        grid_spec=pltpu.PrefetchScalarGridSpec(
            num_scalar_prefetch=5, grid=(nj, n_blocks),
            in_specs=[pl.BlockSpec((MOE_TM, d_ff), lambda j, m, be, *_: (m, 0)),
                      pl.BlockSpec(memory_space=pl.ANY),
                      pl.BlockSpec((1, 1, tn), lambda j, m, be, *_: (be[m], 0, j))],
            out_specs=pl.BlockSpec((MOE_TM, tn), lambda j, m, be, *_: (m, j)),
            scratch_shapes=[pltpu.VMEM((2, d_ff, tn), F32), pltpu.VMEM((d_ff, tn), BF16),
                            pltpu.SemaphoreType.DMA((2,)), pltpu.SMEM((1,), I32)]),
        compiler_params=_cparams(("arbitrary", "arbitrary")),
    )(block_e, block_rows, chg, nxt, wrap, a, w_down, b_down)


def _combine_kernel(x1_ref, y_ref, gt_ref, g_ref, b_ref, o_ref, *, dn_alpha):
    gt = gt_ref[...]
    moe = y_ref[0] * gt[:, 0:1]
    for k in range(1, TOP_K):
        moe = moe + y_ref[k] * gt[:, k:k + 1]
    o_ref[...] = _layer_norm(dn_alpha * x1_ref[...] + moe, g_ref[...], b_ref[...])


def _combine_ln(x1, yk, gates, ln_g, ln_b, dn_alpha, row0, n):
    d = x1.shape[1]
    tm = _pick(n, 256)
    r0 = row0 // tm
    return pl.pallas_call(
        functools.partial(_combine_kernel, dn_alpha=dn_alpha),
        out_shape=jax.ShapeDtypeStruct((n, d), F32),
        grid=(n // tm,),
        in_specs=[pl.BlockSpec((tm, d), lambda i: (r0 + i, 0)),
                  pl.BlockSpec((TOP_K, tm, d), lambda i: (0, r0 + i, 0)),
                  pl.BlockSpec((tm, LANE), lambda i: (r0 + i, 0)),
                  pl.BlockSpec((1, d), lambda i: (0, 0)),
                  pl.BlockSpec((1, d), lambda i: (0, 0))],
        out_specs=pl.BlockSpec((tm, d), lambda i: (i, 0)),
        compiler_params=_cparams(("parallel",)),
    )(x1, yk, gates, ln_g, ln_b)


def _route(top_e, n_experts):
    n = top_e.shape[0]
    n4 = n * TOP_K
    e_flat = top_e.reshape(-1)
    entry = jnp.arange(n4, dtype=I32)
    _, order = lax.sort((e_flat, entry), num_keys=1, is_stable=True)
    experts = jnp.arange(n_experts, dtype=I32)
    counts = jnp.sum((e_flat[:, None] == experts[None, :]).astype(I32), axis=0)
    ends = jnp.cumsum(counts)
    starts = ends - counts
    padded = (counts + MOE_TM - 1) // MOE_TM * MOE_TM
    pends = jnp.cumsum(padded)
    pstarts = pends - padded
    dest_s = entry + jnp.sum(jnp.where(entry[:, None] >= ends[None, :],
                                       (padded - counts)[None, :], 0), axis=1)
    _, pos = lax.sort((order, dest_s), num_keys=1)
    n_blocks = -(-n4 // MOE_TM) + n_experts
    block0 = jnp.arange(n_blocks, dtype=I32) * MOE_TM
    block_e = jnp.minimum(jnp.sum((pends[None, :] <= block0[:, None]).astype(I32), axis=1),
                          n_experts - 1)
    block_rows = jnp.clip(counts[block_e] - (block0 - pstarts[block_e]), 0, MOE_TM)
    in_block = jnp.arange(MOE_TM, dtype=I32)[None, :]
    src = (block0 + starts[block_e] - pstarts[block_e])[:, None] + in_block
    row_tok = jnp.where(in_block < block_rows[:, None],
                        order[jnp.clip(src, 0, n4 - 1)] // TOP_K, 0).reshape(-1)
    blk = jnp.arange(n_blocks, dtype=I32)
    prev_e = jnp.concatenate([block_e[:1] - 1, block_e[:-1]])
    chg = (block_rows > 0) & (block_e != prev_e)
    later = jnp.where(chg, blk, n_blocks)
    nxt_idx = lax.cummin(jnp.concatenate([later[1:], later[:1] * 0 + n_blocks]), reverse=True)
    wrap = nxt_idx >= n_blocks
    nxt_e = block_e[jnp.where(wrap, 0, nxt_idx)]
    nxt = jnp.concatenate([block_e[:1], nxt_e])
    plan = (block_e, block_rows.astype(I32), chg.astype(I32), nxt.astype(I32), wrap.astype(I32))
    return row_tok, pos.reshape(n, TOP_K), plan


def _layer(x_prompt, x_sample, cache, state_win, state_conv, page_table,
           w_in, w_out, conv_w, cmp_k_pe, cmp_k_w1, cmp_k_w2, cmp_v_pe, cmp_v_w1, cmp_v_w2,
           ln1_g, ln1_b, w_router, b_router, w_gu, b_gu, w_down, b_down, ln2_g, ln2_b, depth):
    nb, seq_len, d = x_prompt.shape
    db, dec_seq, _ = x_sample.shape
    n_planes = N_KV_COMP * N_KV_HEADS
    n_wplanes = 2 * N_KV_HEADS
    page = cache.shape[1] // n_planes
    n_pages = page_table.shape[1]
    past_len = n_pages * page
    win_buf = state_win.shape[1] // n_wplanes
    conv_dim = conv_w.shape[1]
    n_experts = w_router.shape[1]
    n_p, n_s = nb * seq_len, db * dec_seq
    dn_alpha = (2.0 * depth) ** 0.25
    assert page == LANE and dec_seq == SUBLANE and seq_len % Q_BLOCK == 0
    assert seq_len >= WINDOW + Q_BLOCK and win_buf == WINDOW and seq_len % win_buf == 0
    assert n_p % n_s == 0 and conv_dim % LANE == 0 and n_experts <= LANE
    assert past_len + dec_seq <= 2 * 32 * SEL_BLOCK and seq_len <= 32 * SEL_BLOCK

    off_gate = ATTN_WIDTH + N_KV_COMP * KV_WIDTH + 2 * KV_WIDTH
    off_conv = off_gate + N_GATE
    w_p = jnp.concatenate([w_in[:, :off_gate], w_in[:, off_conv:], w_in[:, off_gate:off_conv],
                           jnp.zeros((d, LANE - N_GATE), w_in.dtype)], axis=1).astype(BF16)
    n_cb = w_p.shape[1] // LANE
    xp, xs = x_prompt.reshape(n_p, d), x_sample.reshape(n_s, d)
    n_tok = n_p + n_s
    proj = _proj(xp, xs, w_p, _pick(n_s, 512), _pick(w_p.shape[1], 1920, LANE))

    kv_cb = ATTN_WIDTH // (n_planes * LANE)
    tb = _pick(n_p, 256)
    kv_prompt = _interleave(proj, n_p // tb, tb, n_planes, lambda i: i, kv_cb)
    win_cb = (ATTN_WIDTH + n_planes * LANE) // (n_wplanes * LANE)
    per_seq = seq_len // win_buf
    win_prompt = _interleave(proj, nb, win_buf, n_wplanes,
                             lambda i: i * per_seq + per_seq - 1, win_cb)

    w1k = jnp.concatenate(jnp.split(cmp_k_w1, CMP_RATIO, axis=0), axis=1).astype(BF16)
    w1v = jnp.concatenate(jnp.split(cmp_v_w1, CMP_RATIO, axis=0), axis=1).astype(BF16)
    bk = _pe_bias(cmp_k_pe, cmp_k_w1.astype(BF16))
    bv = _pe_bias(cmp_v_pe, cmp_v_w1.astype(BF16))
    cmp_w = (w1k, w1v, bk, bv, cmp_k_w2.astype(BF16), cmp_v_w2.astype(BF16))
    pp = seq_len // LANE
    prompt_ids = jnp.arange(nb * pp, dtype=I32).reshape(nb, pp)
    proj_pages = proj.reshape(n_tok // LANE, LANE, n_cb * LANE)
    kc_p, vc_p = _compress(prompt_ids, proj_pages, ATTN_WIDTH // (2 * KV_WIDTH), *cmp_w)

    o_p = _prompt_attention(proj, kc_p, vc_p, nb, seq_len, n_cb)
    o_s, kv_sample, win_sample = _sample_attention(page_table, cache, proj, state_win, cmp_w,
                                                   n_p, dec_seq, n_cb)

    c_p, u_p = _short_conv(proj, conv_w, 0, nb, seq_len, seq_len)
    z = jnp.zeros((db, dec_seq - 2, conv_dim), F32)
    s1 = jnp.concatenate([state_conv[:, 1:2], z, z[:, :1]], axis=1).reshape(n_s, conv_dim)
    s2 = jnp.concatenate([state_conv, z], axis=1).reshape(n_s, conv_dim)
    c_s, u_s = _short_conv(proj, conv_w, n_p // n_s, 1, n_s, dec_seq, state=(s1, s2))
    conv_prompt = u_p.reshape(nb, SUBLANE, conv_dim)[:, SUBLANE - (CONV_WIDTH - 1):]
    conv_sample = u_s.reshape(db, dec_seq, conv_dim)[:, dec_seq - (CONV_WIDTH - 1):]

    w_r = jnp.pad(w_router, ((0, 0), (0, LANE - n_experts)))
    b_r = jnp.pad(b_router, (0, LANE - n_experts)).reshape(1, LANE)
    x1, e_idx, gates = _mix_ln_router(
        (o_p, o_s), (c_p, c_s), (xp, xs), w_out[:ATTN_WIDTH].astype(BF16),
        w_out[ATTN_WIDTH:].astype(BF16), ln1_g.reshape(1, d), ln1_b.reshape(1, d), w_r, b_r,
        dn_alpha, n_experts)

    row_tok, pos, plan = _route(e_idx[:, :TOP_K], n_experts)
    y_rows = _experts(plan, x1[row_tok], w_gu, b_gu.reshape(n_experts, 1, -1),
                      w_down, b_down.reshape(n_experts, 1, -1))

    comb = (x1, y_rows[pos.T], gates, ln2_g.reshape(1, d), ln2_b.reshape(1, d), dn_alpha)
    y_prompt = _combine_ln(*comb, 0, n_p).reshape(nb, seq_len, d)
    y_sample = _combine_ln(*comb, n_p, n_s).reshape(db, dec_seq, d)
    kv_shape = (N_KV_COMP, N_KV_HEADS, HEAD_DIM)
    win_shape = (win_buf, 2, N_KV_HEADS, HEAD_DIM)
    return (y_prompt, y_sample,
            kv_prompt.reshape((nb, seq_len) + kv_shape), kv_sample.reshape((db, dec_seq) + kv_shape),
            win_prompt.reshape((nb,) + win_shape), win_sample.reshape((db,) + win_shape),
            conv_prompt, conv_sample)


def kernel(x_prompt, x_sample, cache_kv, state_win, state_conv, page_table, w_in, w_out, conv_w, cmp_k_pe, cmp_k_w1, cmp_k_w2, cmp_v_pe, cmp_v_w1, cmp_v_w2, ln1_g, ln1_b, w_router, b_router, w_gu, b_gu, w_down, b_down, ln2_g, ln2_b):
    depth, n_pool, page = cache_kv.shape[:3]
    db, win_buf = state_win.shape[1:3]
    cache_rows = cache_kv.reshape(depth * n_pool, page * N_KV_COMP * N_KV_HEADS, HEAD_DIM)
    win_rows = state_win.reshape(depth, db, win_buf * 2 * N_KV_HEADS, HEAD_DIM)
    y_prompt, y_sample = x_prompt, x_sample
    outs = [[] for _ in range(6)]
    for layer in range(depth):
        res = _layer(y_prompt, y_sample, cache_rows, win_rows[layer], state_conv[layer],
                     page_table + layer * n_pool,
                     w_in[layer], w_out[layer], conv_w[layer],
                     cmp_k_pe[layer], cmp_k_w1[layer], cmp_k_w2[layer],
                     cmp_v_pe[layer], cmp_v_w1[layer], cmp_v_w2[layer],
                     ln1_g[layer], ln1_b[layer], w_router[layer], b_router[layer],
                     w_gu[layer], b_gu[layer], w_down[layer], b_down[layer],
                     ln2_g[layer], ln2_b[layer], depth)
        y_prompt, y_sample = res[0], res[1]
        for lst, val in zip(outs, res[2:]):
            lst.append(val)
    return (y_prompt, y_sample) + tuple(jnp.stack(o) for o in outs)
```

```python
import functools
import math

import numpy as np
import jax
import jax.numpy as jnp
from jax import lax
from jax.experimental import pallas as pl
from jax.experimental.pallas import tpu as pltpu

F32 = jnp.float32
BF16 = jnp.bfloat16
I32 = jnp.int32

HEAD_DIM = 128
N_HEADS = 8
N_KV_HEADS = 2
GQA_REP = N_HEADS // N_KV_HEADS
N_KV_COMP = 4
KV_WIDTH = N_KV_HEADS * HEAD_DIM
ATTN_WIDTH = N_HEADS * HEAD_DIM
CONV_WIDTH = 3
CMP_BLOCK = 32
CMP_STRIDE = 16
CMP_RATIO = CMP_BLOCK // CMP_STRIDE
CMP_HIDDEN = 256
SEL_BLOCK = 64
SEL_TOPK = 8
WINDOW = 512
N_BRANCH = 3
Q_BLOCK = 128
TOP_K = 4
SWIGLU_LIMIT = 7.0
SWIGLU_ALPHA = 1.702
MOE_TM = 256
LN_EPS = 1e-5
NEG = -1e30
BIG = 1e30
LANE = 128
SUBLANE = 8
VMEM_LIMIT = 56 * 1024 * 1024
SEL_CHUNK = 512
SAMPLE_SUB = 2
MASK_M = -2.0 ** 100
AUX_HI = 64
AUX_LO = 65

CB_KV = ATTN_WIDTH // LANE
CB_WIN = CB_KV + N_KV_COMP * N_KV_HEADS
CB_CONV = CB_WIN + 2 * N_KV_HEADS
N_GATE = N_BRANCH * N_HEADS


def _alibi_slopes(n):
    def pow2(m):
        start = 2.0 ** (-8.0 / m)
        return [start ** (i + 1) for i in range(m)]
    if math.log2(n).is_integer():
        s = pow2(n)
    else:
        c = 2 ** math.floor(math.log2(n))
        s = pow2(c) + pow2(2 * c)[0::2][:n - c]
    return [float(np.float32(v)) for v in s]


SLOPES = _alibi_slopes(N_HEADS)


def _pick(n, target, mult=SUBLANE):
    for d in range(min(n, target), 0, -1):
        if n % d == 0 and d % mult == 0:
            return d
    raise ValueError((n, target, mult))


def _cparams(sem):
    return pltpu.CompilerParams(dimension_semantics=sem, vmem_limit_bytes=VMEM_LIMIT)


def _pair_specs(tm, width, n_first_blocks):
    return [pl.BlockSpec((tm, width), lambda i, *_: (jnp.minimum(i, n_first_blocks - 1), 0)),
            pl.BlockSpec((tm, width), lambda i, *_: (jnp.maximum(i - n_first_blocks, 0), 0))]


def _pair_load(first_ref, second_ref, n_first_blocks):
    return jnp.where(pl.program_id(0) < n_first_blocks, first_ref[...], second_ref[...])


def _proj_kernel(xp_ref, xs_ref, w_ref, o_ref, *, n_first):
    x = _pair_load(xp_ref, xs_ref, n_first)
    o_ref[...] = jnp.dot(x.astype(BF16), w_ref[...], preferred_element_type=F32)


def _proj(xp, xs, w, tm, tn):
    k = xp.shape[1]
    n = w.shape[1]
    n_first = xp.shape[0] // tm
    m = xp.shape[0] + xs.shape[0]
    return pl.pallas_call(
        functools.partial(_proj_kernel, n_first=n_first),
        out_shape=jax.ShapeDtypeStruct((m, n), F32),
        grid=(m // tm, n // tn),
        in_specs=_pair_specs(tm, k, n_first) + [pl.BlockSpec((k, tn), lambda i, j: (0, j))],
        out_specs=pl.BlockSpec((tm, tn), lambda i, j: (i, j)),
        compiler_params=_cparams(("parallel", "arbitrary")),
    )(xp, xs, w)


def _pe_bias_kernel(pe_ref, w1_ref, o_ref):
    o_ref[...] = jnp.dot(pe_ref[...].astype(BF16), w1_ref[...], preferred_element_type=F32)


def _pe_bias(pe, w1_bf):
    flat = jnp.broadcast_to(pe.reshape(1, -1), (SUBLANE, pe.size))
    return pl.pallas_call(
        _pe_bias_kernel,
        out_shape=jax.ShapeDtypeStruct((SUBLANE, CMP_HIDDEN), F32),
    )(flat, w1_bf)


S_PITCH = 136


def _scatter_rows(raw, plane, base, x):
    per = SUBLANE
    for j in range(x.shape[0] // per):
        c, s0 = divmod(j * per, CMP_STRIDE)
        raw[plane, pl.ds(s0 * S_PITCH + base + c, per, stride=S_PITCH), :] = x[j * per:(j + 1) * per]


def _compress_core(raw, weights, n_chunk, plane0=0):
    assert n_chunk <= S_PITCH
    outs = []
    for kv, (w1_ref, b_ref, w2_ref) in enumerate(weights):
        acc = jnp.zeros((N_KV_HEADS * n_chunk, CMP_RATIO * CMP_HIDDEN), F32)
        for sp in range(CMP_STRIDE // 2):
            per_g = []
            for g in range(N_KV_HEADS):
                halves = [raw[plane0 + kv * N_KV_HEADS + g, pl.ds(s * S_PITCH, n_chunk), :]
                          for s in (2 * sp, 2 * sp + 1)]
                per_g.append(jnp.concatenate(halves, axis=1))
            lhs = jnp.concatenate(per_g, axis=0).astype(BF16)
            acc = acc + jnp.dot(lhs, w1_ref[pl.ds(sp * 2 * HEAD_DIM, 2 * HEAD_DIM), :],
                                preferred_element_type=F32)
        per_kv = []
        for g in range(N_KV_HEADS):
            part0 = acc[g * n_chunk:(g + 1) * n_chunk, :CMP_HIDDEN]
            part1 = acc[g * n_chunk:(g + 1) * n_chunk, CMP_HIDDEN:]
            part1 = jnp.concatenate([part1[1:], part1[:1]], axis=0)
            pre = (b_ref[0:1, :] + part0) + part1
            per_kv.append(jnp.dot(jax.nn.gelu(pre).astype(BF16), w2_ref[...],
                                  preferred_element_type=F32))
        outs.append(per_kv)
    return outs


def _compress_kernel(ids_ref, *refs, n_pages):
    pages = refs[:n_pages]
    w1k_ref, w1v_ref, bk_ref, bv_ref, w2k_ref, w2v_ref, kc_ref, vc_ref, raw = refs[n_pages:]
    n_chunk = n_pages * (LANE // CMP_STRIDE)
    for cb in range(2 * N_KV_HEADS):
        for p in range(n_pages):
            _scatter_rows(raw, cb, p * (LANE // CMP_STRIDE),
                          pages[p][0, :, cb * HEAD_DIM:(cb + 1) * HEAD_DIM])
    kc, vc = _compress_core(raw, ((w1k_ref, bk_ref, w2k_ref), (w1v_ref, bv_ref, w2v_ref)), n_chunk)
    for g in range(N_KV_HEADS):
        kc_ref[0, g] = kc[g]
        vc_ref[0, g] = vc[g]


def _compress(page_ids, src, col_block, w1k, w1v, bk, bv, w2k, w2v):
    nb, n_pages = page_ids.shape
    n_chunk = n_pages * (LANE // CMP_STRIDE)
    width = 2 * KV_WIDTH

    def page_map(p):
        return lambda b, ids: (ids[b, p], 0, col_block)

    full = lambda shape: pl.BlockSpec(shape, lambda b, ids: (0,) * len(shape))
    in_specs = [pl.BlockSpec((1, LANE, width), page_map(p)) for p in range(n_pages)]
    in_specs += [full(w1k.shape), full(w1v.shape), full(bk.shape), full(bv.shape),
                 full(w2k.shape), full(w2v.shape)]
    out_spec = pl.BlockSpec((1, N_KV_HEADS, n_chunk, HEAD_DIM), lambda b, ids: (b, 0, 0, 0))
    out_sd = jax.ShapeDtypeStruct((nb, N_KV_HEADS, n_chunk, HEAD_DIM), F32)
    return pl.pallas_call(
        functools.partial(_compress_kernel, n_pages=n_pages),
        out_shape=(out_sd, out_sd),
        grid_spec=pltpu.PrefetchScalarGridSpec(
            num_scalar_prefetch=1, grid=(nb,), in_specs=in_specs,
            out_specs=(out_spec, out_spec),
            scratch_shapes=[pltpu.VMEM((2 * N_KV_HEADS, CMP_STRIDE * S_PITCH, HEAD_DIM), F32)]),
        compiler_params=_cparams(("parallel",)),
    )(page_ids, *([src] * n_pages), w1k, w1v, bk, bv, w2k, w2v)


def _nt_dot(a, b):
    return lax.dot_general(a, b, (((1,), (1,)), ((), ())), preferred_element_type=F32)


def _masked_softmax(s, mask):
    s = jnp.where(mask, s, NEG)
    e = jnp.where(mask, jnp.exp(s - jnp.max(s, axis=-1, keepdims=True)), 0.0)
    den = jnp.sum(e, axis=-1, keepdims=True)
    return e / jnp.where(den > 0, den, 1.0)


def _stack_heads(q, tq):
    q = q * (HEAD_DIM ** -0.5)
    return jnp.concatenate([q[:, r * HEAD_DIM:(r + 1) * HEAD_DIM] for r in range(GQA_REP)],
                           axis=0).astype(BF16)


def _row_consts(g, tq, p0):
    rows = GQA_REP * tq
    ridx = lax.broadcasted_iota(I32, (rows, 1), 0)
    slope = jnp.zeros((rows, 1), F32)
    for r in range(GQA_REP):
        slope = jnp.where(ridx // tq == r, SLOPES[g * GQA_REP + r], slope)
    t = p0 + ridx % tq
    return slope, t


def _cmp_branch(q4, kc, vc, slope, t, tq):
    n_chunk = kc.shape[0]
    s = _nt_dot(q4, kc.astype(BF16))
    c_end = lax.broadcasted_iota(I32, (1, n_chunk), 1) * CMP_STRIDE + (CMP_BLOCK - 1)
    rel = t - c_end
    s = s - slope * rel.astype(F32)
    p = _masked_softmax(s, rel >= 0)
    o_c = jnp.dot(p.astype(BF16), vc.astype(BF16), preferred_element_type=F32)
    p_sum = p[0:tq]
    for r in range(1, GQA_REP):
        p_sum = p_sum + p[r * tq:(r + 1) * tq]
    return o_c, p_sum


def _pos_aux(pos):
    pos = np.asarray(pos).reshape(-1, 1)
    lane = np.arange(LANE).reshape(1, -1)
    blk = pos // SEL_BLOCK
    aux = np.where(lane == blk, MASK_M, 0.0)
    aux = np.where(lane == AUX_HI, blk * SEL_BLOCK, aux)
    aux = np.where(lane == AUX_LO, pos % SEL_BLOCK, aux)
    assert blk.max() < AUX_HI
    return jnp.asarray(aux, F32).astype(BF16)


def _query_aux(slope, notsel4):
    lane = lax.broadcasted_iota(I32, (slope.shape[0], LANE), 1)
    base = jnp.zeros((slope.shape[0], LANE), F32) if notsel4 is None else notsel4
    return jnp.where((lane == AUX_HI) | (lane == AUX_LO), slope, base).astype(BF16)


def _not_selected(p_sum, t_row, n_sel):
    tq, n_chunk = p_sum.shape
    nj = -(-n_sel // SUBLANE) * SUBLANE
    cj = lax.broadcasted_iota(I32, (nj, n_chunk), 0) * SEL_BLOCK
    cn = lax.broadcasted_iota(I32, (nj, n_chunk), 1) * CMP_STRIDE
    overlap = jnp.maximum(jnp.minimum(cn + CMP_BLOCK, cj + SEL_BLOCK) - jnp.maximum(cn, cj), 0)
    overlap = overlap.astype(F32).astype(BF16)
    p_hi = p_sum.astype(BF16)
    p_lo = (p_sum - p_hi.astype(F32)).astype(BF16)
    imp = _nt_dot(overlap, p_hi) + _nt_dot(overlap, p_lo)
    row = lax.broadcasted_iota(I32, (nj, tq), 0)
    cur = t_row // SEL_BLOCK
    valid = (row <= cur) & (row < n_sel)
    forced = (row == 0) | (row == cur) | (row == cur - 1)
    score = jnp.where(valid, jnp.where(forced, BIG, imp), -1.0)
    notsel = jnp.ones((nj, tq), F32)
    for _ in range(min(SEL_TOPK, n_sel)):
        m = jnp.max(score, axis=0, keepdims=True)
        idx = jnp.min(jnp.where(score == m, row, nj), axis=0, keepdims=True)
        hit = row == idx
        notsel = jnp.where(hit & (m >= 0.0), 0.0, notsel)
        score = jnp.where(hit, -1.0, score)
    notsel = jnp.concatenate([notsel, jnp.ones((LANE - nj, tq), F32)], axis=0)
    if tq < LANE:
        notsel = jnp.concatenate([notsel, jnp.ones((LANE, LANE - tq), F32)], axis=1)
    return notsel.T


def _rep_rows(x):
    return jnp.concatenate([x] * GQA_REP, axis=0)


def _softmax_rows(s):
    e = jnp.exp(s - jnp.max(s, axis=-1, keepdims=True))
    return e / jnp.sum(e, axis=-1, keepdims=True)


def _gated_sum(sig, g, o_c, o_s, o_w, tq):
    outs = []
    for r in range(GQA_REP):
        h = g * GQA_REP + r
        sl = slice(r * tq, (r + 1) * tq)
        g0 = sig[:, N_BRANCH * h + 0:N_BRANCH * h + 1]
        g1 = sig[:, N_BRANCH * h + 1:N_BRANCH * h + 2]
        g2 = sig[:, N_BRANCH * h + 2:N_BRANCH * h + 3]
        outs.append(g0 * o_c[sl] + g1 * o_s[sl] + g2 * o_w[sl])
    return outs


def _prompt_attn_kernel(q_ref, gate_ref, kc_ref, vc_ref, ks_ref, vs_ref, kw_ref, vw_ref, aux_ref,
                        o_ref, ksb, vsb, kwb, vwb, *, tq, seq_len):
    i = pl.program_id(1)
    n_sel = -(-seq_len // SEL_BLOCK)
    heads = range(N_KV_HEADS)

    @pl.when(i == 0)
    def _():
        for g in heads:
            col = slice(g * HEAD_DIM, (g + 1) * HEAD_DIM)
            ksb[g, :, 0:HEAD_DIM] = ks_ref[:, col].astype(BF16)
            ksb[g, :, HEAD_DIM:] = aux_ref[...]
            kwb[g, :, 0:HEAD_DIM] = kw_ref[:, col].astype(BF16)
            kwb[g, :, HEAD_DIM:] = aux_ref[...]
            vsb[g] = vs_ref[:, col].astype(BF16)
            vwb[g] = vw_ref[:, col].astype(BF16)

    p0 = i * tq
    rows = GQA_REP * tq
    t_row = p0 + lax.broadcasted_iota(I32, (1, tq), 1)
    q_sel, q_win, o_c = [], [], []
    for g in heads:
        q4 = _stack_heads(q_ref[:, g * GQA_REP * HEAD_DIM:(g + 1) * GQA_REP * HEAD_DIM], tq)
        slope, t = _row_consts(g, tq, p0)
        oc, p_sum = _cmp_branch(q4, kc_ref[0, g], vc_ref[0, g], slope, t, tq)
        notsel4 = _rep_rows(_not_selected(p_sum, t_row, n_sel))
        q_sel.append(jnp.concatenate([q4, _query_aux(slope, notsel4)], axis=1))
        q_win.append(jnp.concatenate([q4, _query_aux(slope, None)], axis=1))
        o_c.append(oc)
    _, t = _row_consts(0, tq, p0)

    def step(g, c, carry, causal):
        m, l, acc = carry
        start = pl.multiple_of(c * SEL_CHUNK, SEL_CHUNK)
        s = _nt_dot(q_sel[g], ksb[g, pl.ds(start, SEL_CHUNK), :])
        if causal:
            pos = start + lax.broadcasted_iota(I32, (1, SEL_CHUNK), 1)
            s = jnp.where(pos <= t, s, NEG)
        m_new = jnp.maximum(m, jnp.max(s, axis=-1, keepdims=True))
        a = jnp.exp(m - m_new)
        e = jnp.exp(s - m_new)
        l = a * l + jnp.sum(e, axis=-1, keepdims=True)
        acc = a * acc + jnp.dot(e.astype(BF16), vsb[g, pl.ds(start, SEL_CHUNK), :],
                                preferred_element_type=F32)
        return m_new, l, acc

    def chunk(c, carries, causal):
        return tuple(step(g, c, carries[g], causal) for g in heads)

    last = (p0 + tq - 1) // SEL_CHUNK
    init = tuple((jnp.full((rows, 1), NEG, F32), jnp.zeros((rows, 1), F32),
                  jnp.zeros((rows, HEAD_DIM), F32)) for _ in heads)
    carries = lax.fori_loop(0, last, lambda c, cr: chunk(c, cr, False), init)
    carries = chunk(last, carries, True)

    n_win = WINDOW + tq
    start = pl.multiple_of(jnp.maximum(p0 - WINDOW, 0), tq)
    rel = ((p0 - start) + lax.broadcasted_iota(I32, (tq, n_win), 0)
           - lax.broadcasted_iota(I32, (tq, n_win), 1))
    band = _rep_rows(jnp.where((rel >= 0) & (rel < WINDOW), 0.0, NEG))
    sig = jax.nn.sigmoid(gate_ref[...])
    for g in heads:
        _, l, acc = carries[g]
        o_s = acc / l
        s = _nt_dot(q_win[g], kwb[g, pl.ds(start, n_win), :]) + band
        o_w = jnp.dot(_softmax_rows(s).astype(BF16), vwb[g, pl.ds(start, n_win), :],
                      preferred_element_type=F32)
        outs = _gated_sum(sig, g, o_c[g], o_s, o_w, tq)
        for r in range(GQA_REP):
            h = g * GQA_REP + r
            o_ref[:, h * HEAD_DIM:(h + 1) * HEAD_DIM] = outs[r]


def _prompt_attention(proj, kc, vc, n_batch, seq_len, n_cols_blocks):
    tq = Q_BLOCK
    nqb = seq_len // tq
    gate_cb = n_cols_blocks - 1
    aux = _pos_aux(np.arange(seq_len))
    kv_spec = lambda cb: pl.BlockSpec((seq_len, KV_WIDTH), lambda b, i: (b, cb * LANE // KV_WIDTH))
    c_spec = pl.BlockSpec((1,) + kc.shape[1:], lambda b, i: (b, 0, 0, 0))
    bf = lambda width: pltpu.VMEM((N_KV_HEADS, seq_len, width), BF16)
    return pl.pallas_call(
        functools.partial(_prompt_attn_kernel, tq=tq, seq_len=seq_len),
        out_shape=jax.ShapeDtypeStruct((n_batch * seq_len, ATTN_WIDTH), F32),
        grid=(n_batch, nqb),
        in_specs=[pl.BlockSpec((tq, ATTN_WIDTH), lambda b, i: (b * nqb + i, 0)),
                  pl.BlockSpec((tq, LANE), lambda b, i: (b * nqb + i, gate_cb)),
                  c_spec, c_spec,
                  kv_spec(CB_KV + 2 * N_KV_HEADS), kv_spec(CB_KV + 3 * N_KV_HEADS),
                  kv_spec(CB_WIN), kv_spec(CB_WIN + N_KV_HEADS),
                  pl.BlockSpec(aux.shape, lambda b, i: (0, 0))],
        out_specs=pl.BlockSpec((tq, ATTN_WIDTH), lambda b, i: (b * nqb + i, 0)),
        scratch_shapes=[bf(2 * HEAD_DIM), bf(HEAD_DIM), bf(2 * HEAD_DIM), bf(HEAD_DIM)],
        compiler_params=_cparams(("parallel", "arbitrary")),
    )(proj, proj, kc, vc, proj, proj, proj, proj, aux)


def _sample_attn_kernel(pt_ref, *refs, n_pages, n_sub, tq, past_len, win_buf):
    refs_rest = refs[n_sub * n_pages:]
    for u in range(n_sub):
        _sample_attn_one(u, refs[u * n_pages:(u + 1) * n_pages], refs_rest, tq, past_len, win_buf)


def _sample_attn_one(u, pages, refs, tq, past_len, win_buf):
    (q_ref, kvn_ref, winn_ref, gate_ref, win_ref, w1k_ref, w1v_ref, bk_ref, bv_ref, w2k_ref,
     w2v_ref, saux_ref, waux_ref, o_ref, kvo_ref, wino_ref, raw) = refs
    rs = slice(u * tq, (u + 1) * tq)
    plane0 = u * 2 * N_KV_HEADS
    n_pages = len(pages)
    seq_len = past_len + tq
    n_sel = -(-seq_len // SEL_BLOCK)
    n_planes = N_KV_COMP * N_KV_HEADS
    n_wplanes = 2 * N_KV_HEADS
    page = pages[0].shape[1] // n_planes
    n_chunk = past_len // CMP_STRIDE
    zpad = jnp.zeros((LANE - tq, HEAD_DIM), BF16)

    for c in range(n_planes):
        kvo_ref[pl.ds(u * tq * n_planes + c, tq, stride=n_planes), :] = kvn_ref[rs, c * HEAD_DIM:(c + 1) * HEAD_DIM]
    keep = (win_buf - tq) * n_wplanes
    wino_ref[u, 0:keep, :] = win_ref[u, tq * n_wplanes:win_buf * n_wplanes, :]
    for c in range(n_wplanes):
        wino_ref[u, pl.ds(keep + c, tq, stride=n_wplanes), :] = winn_ref[rs, c * HEAD_DIM:(c + 1) * HEAD_DIM]

    for cb in range(2 * N_KV_HEADS):
        for p in range(n_pages):
            _scatter_rows(raw, plane0 + cb, p * (page // CMP_STRIDE),
                          pages[p][0, pl.ds(cb, page, stride=n_planes), :])
    kc, vc = _compress_core(raw, ((w1k_ref, bk_ref, w2k_ref), (w1v_ref, bv_ref, w2v_ref)), n_chunk,
                            plane0)

    sig = jax.nn.sigmoid(gate_ref[rs, :])
    q4s, consts, o_cs, p_sums = [], [], [], []
    for g in range(N_KV_HEADS):
        q4 = _stack_heads(q_ref[rs, g * GQA_REP * HEAD_DIM:(g + 1) * GQA_REP * HEAD_DIM], tq)
        slope, t = _row_consts(g, tq, past_len)
        o_c, p_sum = _cmp_branch(q4, kc[g], vc[g], slope, t, tq)
        q4s.append(q4); consts.append((slope, t)); o_cs.append(o_c); p_sums.append(p_sum)
    t_row = past_len + lax.broadcasted_iota(I32, (1, N_KV_HEADS * tq), 1) % tq
    notsel = _not_selected(jnp.concatenate(p_sums, axis=0), t_row, n_sel)

    for g in range(N_KV_HEADS):
        q4, (slope, t), o_c = q4s[g], consts[g], o_cs[g]
        q_sel = jnp.concatenate([q4, _query_aux(slope, _rep_rows(notsel[g * tq:(g + 1) * tq]))], axis=1)
        q_win = jnp.concatenate([q4, _query_aux(slope, None)], axis=1)

        kpl = 2 * N_KV_HEADS + g
        vpl = 3 * N_KV_HEADS + g
        k = jnp.concatenate(
            [p[0, pl.ds(kpl, page, stride=n_planes), :].astype(BF16) for p in pages]
            + [kvn_ref[rs, kpl * HEAD_DIM:(kpl + 1) * HEAD_DIM].astype(BF16), zpad], axis=0)
        v = jnp.concatenate(
            [p[0, pl.ds(vpl, page, stride=n_planes), :].astype(BF16) for p in pages]
            + [kvn_ref[rs, vpl * HEAD_DIM:(vpl + 1) * HEAD_DIM].astype(BF16), zpad], axis=0)
        pos = lax.broadcasted_iota(I32, (1, k.shape[0]), 1)
        s = _nt_dot(q_sel, jnp.concatenate([k, saux_ref[...]], axis=1))
        p = _softmax_rows(jnp.where(pos <= t, s, NEG))
        o_s = jnp.dot(p.astype(BF16), v, preferred_element_type=F32)

        kw = jnp.concatenate([win_ref[u, pl.ds(g, win_buf, stride=n_wplanes), :].astype(BF16),
                              winn_ref[rs, g * HEAD_DIM:(g + 1) * HEAD_DIM].astype(BF16), zpad],
                             axis=0)
        vp = N_KV_HEADS + g
        vw = jnp.concatenate([win_ref[u, pl.ds(vp, win_buf, stride=n_wplanes), :].astype(BF16),
                              winn_ref[rs, vp * HEAD_DIM:(vp + 1) * HEAD_DIM].astype(BF16), zpad],
                             axis=0)
        kpos = (past_len - win_buf) + lax.broadcasted_iota(I32, (1, kw.shape[0]), 1)
        rel = t - kpos
        s = _nt_dot(q_win, jnp.concatenate([kw, waux_ref[...]], axis=1))
        p = _softmax_rows(jnp.where((rel >= 0) & (rel < WINDOW), s, NEG))
        o_w = jnp.dot(p.astype(BF16), vw, preferred_element_type=F32)

        outs = _gated_sum(sig, g, o_c, o_s, o_w, tq)
        for r in range(GQA_REP):
            h = g * GQA_REP + r
            o_ref[rs, h * HEAD_DIM:(h + 1) * HEAD_DIM] = outs[r]


def _sample_attention(page_table, cache, proj, state_win, cmp_w, n_prompt_rows, tq, n_cols_blocks):
    db, n_pages = page_table.shape
    n_planes = N_KV_COMP * N_KV_HEADS
    n_wplanes = 2 * N_KV_HEADS
    page = cache.shape[1] // n_planes
    past_len = n_pages * page
    win_buf = state_win.shape[1] // n_wplanes
    n_sub = SAMPLE_SUB if db % SAMPLE_SUB == 0 and (n_prompt_rows // tq) % SAMPLE_SUB == 0 else 1
    rows = n_sub * tq
    r0 = n_prompt_rows // rows
    gate_cb = n_cols_blocks - 1
    sel_aux = _pos_aux(np.arange(past_len + LANE))
    win_aux = _pos_aux(past_len - win_buf + np.arange(win_buf + LANE))

    def page_map(u, p):
        return lambda b, pt: (pt[n_sub * b + u, p], 0, 0)

    full = lambda a: pl.BlockSpec(a.shape, lambda b, pt: (0,) * a.ndim)
    in_specs = [pl.BlockSpec((1,) + cache.shape[1:], page_map(u, p))
                for u in range(n_sub) for p in range(n_pages)]
    in_specs += [
        pl.BlockSpec((rows, ATTN_WIDTH), lambda b, pt: (r0 + b, 0)),
        pl.BlockSpec((rows, N_KV_COMP * KV_WIDTH), lambda b, pt: (r0 + b, 1)),
        pl.BlockSpec((rows, 2 * KV_WIDTH), lambda b, pt: (r0 + b, CB_WIN * LANE // (2 * KV_WIDTH))),
        pl.BlockSpec((rows, LANE), lambda b, pt: (r0 + b, gate_cb)),
        pl.BlockSpec((n_sub,) + state_win.shape[1:], lambda b, pt: (b, 0, 0)),
    ] + [full(w) for w in cmp_w] + [full(sel_aux), full(win_aux)]
    return pl.pallas_call(
        functools.partial(_sample_attn_kernel, n_pages=n_pages, n_sub=n_sub, tq=tq,
                          past_len=past_len, win_buf=win_buf),
        out_shape=(jax.ShapeDtypeStruct((db * tq, ATTN_WIDTH), F32),
                   jax.ShapeDtypeStruct((db * tq * n_planes, HEAD_DIM), F32),
                   jax.ShapeDtypeStruct(state_win.shape, F32)),
        grid_spec=pltpu.PrefetchScalarGridSpec(
            num_scalar_prefetch=1, grid=(db // n_sub,), in_specs=in_specs,
            out_specs=(pl.BlockSpec((rows, ATTN_WIDTH), lambda b, pt: (b, 0)),
                       pl.BlockSpec((rows * n_planes, HEAD_DIM), lambda b, pt: (b, 0)),
                       pl.BlockSpec((n_sub,) + state_win.shape[1:], lambda b, pt: (b, 0, 0))),
            scratch_shapes=[pltpu.VMEM((n_sub * 2 * N_KV_HEADS, CMP_STRIDE * S_PITCH, HEAD_DIM),
                                       F32)]),
        compiler_params=_cparams(("parallel",)),
    )(page_table, *([cache] * (n_sub * n_pages)), proj, proj, proj, proj, state_win, *cmp_w, sel_aux, win_aux)


def _interleave_kernel(x_ref, o_ref, *, n_planes):
    rows = x_ref.shape[0]
    for c in range(n_planes):
        o_ref[pl.ds(c, rows, stride=n_planes), :] = x_ref[:, c * LANE:(c + 1) * LANE]


def _interleave(proj, n_blocks, block_rows, n_planes, row_block_of, col_block):
    return pl.pallas_call(
        functools.partial(_interleave_kernel, n_planes=n_planes),
        out_shape=jax.ShapeDtypeStruct((n_blocks * block_rows * n_planes, LANE), F32),
        grid=(n_blocks,),
        in_specs=[pl.BlockSpec((block_rows, n_planes * LANE), lambda i: (row_block_of(i), col_block))],
        out_specs=pl.BlockSpec((block_rows * n_planes, LANE), lambda i: (i, 0)),
        compiler_params=_cparams(("parallel",)),
    )(proj)


def _conv_kernel(*refs, seq_len, has_state):
    if has_state:
        b_ref, c_ref, h_ref, w_ref, s1_ref, s2_ref, y_ref, u_ref = refs
    else:
        b_ref, c_ref, h_ref, w_ref, y_ref, u_ref = refs
    u = c_ref[...] * h_ref[...]
    rows = u.shape[0]
    t = lax.broadcasted_iota(I32, (rows, 1), 0) % seq_len
    p1 = jnp.where(t >= 1, pltpu.roll(u, 1, 0), 0.0)
    p2 = jnp.where(t >= 2, pltpu.roll(u, 2, 0), 0.0)
    if has_state:
        p1 = p1 + s1_ref[...]
        p2 = p2 + s2_ref[...]
    y = w_ref[0:1, :] * p2
    y = y + w_ref[1:2, :] * p1
    y = y + w_ref[2:3, :] * u
    y_ref[...] = b_ref[...] * y
    u_ref[...] = u[rows - u_ref.shape[0]:, :]


def _short_conv(proj, conv_w, row_block0, n_blocks, block_rows, seq_len, state=None):
    conv_dim = conv_w.shape[1]
    n_ct = conv_dim // LANE
    tail = SUBLANE if state is None else block_rows
    spec = lambda cb: pl.BlockSpec((block_rows, LANE), lambda r, c: (row_block0 + r, cb + c))
    in_specs = [spec(CB_CONV), spec(CB_CONV + n_ct), spec(CB_CONV + 2 * n_ct),
                pl.BlockSpec((CONV_WIDTH, LANE), lambda r, c: (0, c))]
    args = [proj, proj, proj, conv_w]
    if state is not None:
        st_spec = pl.BlockSpec((block_rows, LANE), lambda r, c: (r, c))
        in_specs += [st_spec, st_spec]
        args += list(state)
    return pl.pallas_call(
        functools.partial(_conv_kernel, seq_len=seq_len, has_state=state is not None),
        out_shape=(jax.ShapeDtypeStruct((n_blocks * block_rows, conv_dim), F32),
                   jax.ShapeDtypeStruct((n_blocks * tail, conv_dim), F32)),
        grid=(n_blocks, n_ct),
        in_specs=in_specs,
        out_specs=(pl.BlockSpec((block_rows, LANE), lambda r, c: (r, c)),
                   pl.BlockSpec((tail, LANE), lambda r, c: (r, c))),
        compiler_params=_cparams(("parallel", "parallel")),
    )(*args)


def _layer_norm(x, g, b):
    mu = jnp.mean(x, axis=-1, keepdims=True)
    var = jnp.mean(jnp.square(x - mu), axis=-1, keepdims=True)
    return (x - mu) * lax.rsqrt(var + LN_EPS) * g + b


def _mix_kernel(oa_p, oa_s, oc_p, oc_s, x_p, x_s, wa_ref, wc_ref, g_ref, b_ref, wr_ref, br_ref,
                x1_ref, e_ref, gt_ref, *, dn_alpha, n_experts, n_first):
    o_attn = _pair_load(oa_p, oa_s, n_first)
    o_conv = _pair_load(oc_p, oc_s, n_first)
    mix = jnp.dot(o_attn.astype(BF16), wa_ref[...], preferred_element_type=F32)
    mix = mix + jnp.dot(o_conv.astype(BF16), wc_ref[...], preferred_element_type=F32)
    x1 = _layer_norm(dn_alpha * _pair_load(x_p, x_s, n_first) + mix, g_ref[...], b_ref[...])
    x1_ref[...] = x1
    x_hi = x1.astype(BF16)
    x_lo = (x1 - x_hi.astype(F32)).astype(BF16)
    w_r = wr_ref[...]
    w_hi = w_r.astype(BF16)
    w_lo = (w_r - w_hi.astype(F32)).astype(BF16)
    logits = (jnp.dot(x_hi, w_hi, preferred_element_type=F32)
              + jnp.dot(x_lo, w_hi, preferred_element_type=F32)
              + jnp.dot(x_hi, w_lo, preferred_element_type=F32)) + br_ref[...]
    tm = logits.shape[0]
    lane = lax.broadcasted_iota(I32, (tm, LANE), 1)
    score = jnp.where(lane < n_experts, logits, -jnp.inf)
    e_out = jnp.zeros((tm, LANE), I32)
    v_out = jnp.zeros((tm, LANE), F32)
    v0 = None
    den = jnp.zeros((tm, 1), F32)
    for k in range(TOP_K):
        m = jnp.max(score, axis=-1, keepdims=True)
        idx = jnp.min(jnp.where(score == m, lane, LANE), axis=-1, keepdims=True)
        if k == 0:
            v0 = m
        ex = jnp.exp(m - v0)
        den = den + ex
        e_out = jnp.where(lane == k, idx, e_out)
        v_out = jnp.where(lane == k, ex, v_out)
        score = jnp.where(lane == idx, -jnp.inf, score)
    e_ref[...] = e_out
    gt_ref[...] = v_out / den


def _mix_ln_router(o_attn, o_conv, x, w_out_a, w_out_c, ln_g, ln_b, w_r, b_r, dn_alpha, n_experts):
    n = x[0].shape[0] + x[1].shape[0]
    d = x[0].shape[1]
    tm = _pick(x[1].shape[0], 256)
    n_first = x[0].shape[0] // tm
    row = lambda w: pl.BlockSpec((tm, w), lambda i: (i, 0))
    full = lambda a: pl.BlockSpec(a.shape, lambda i: (0, 0))
    pairs = lambda p: _pair_specs(tm, p[0].shape[1], n_first)
    return pl.pallas_call(
        functools.partial(_mix_kernel, dn_alpha=dn_alpha, n_experts=n_experts, n_first=n_first),
        out_shape=(jax.ShapeDtypeStruct((n, d), F32),
                   jax.ShapeDtypeStruct((n, LANE), I32), jax.ShapeDtypeStruct((n, LANE), F32)),
        grid=(n // tm,),
        in_specs=pairs(o_attn) + pairs(o_conv) + pairs(x) + [
            full(w_out_a), full(w_out_c), full(ln_g), full(ln_b), full(w_r), full(b_r)],
        out_specs=(row(d), row(LANE), row(LANE)),
        compiler_params=_cparams(("parallel",)),
    )(*o_attn, *o_conv, *x, w_out_a, w_out_c, ln_g, ln_b, w_r, b_r)


def _weight_pipeline(f, m, nf, chg_ref, nxt_ref, wrap_ref, cnt_ref, copies, cast):
    @pl.when((f == 0) & (m == 0))
    def _():
        cnt_ref[0] = 0
        for c in copies(nxt_ref[0], 0, 0):
            c.start()

    @pl.when(chg_ref[m] == 1)
    def _():
        slot = cnt_ref[0] % 2
        f_next = f + wrap_ref[m]

        @pl.when(f_next < nf)
        def _():
            for c in copies(nxt_ref[m + 1], f_next, 1 - slot):
                c.start()

        for c in copies(0, 0, slot):
            c.wait()
        cast(slot)
        cnt_ref[0] = cnt_ref[0] + 1


def _gate_up_kernel(be_ref, br_ref, chg_ref, nxt_ref, wrap_ref, x_ref, w_hbm, bg_ref, bu_ref, a_ref,
                    wbuf, wgb, wub, sem, cnt_ref, *, nf, tf):
    f = pl.program_id(0)
    m = pl.program_id(1)
    rows = br_ref[m]
    half = MOE_TM // 2

    def copies(e, fi, slot):
        return [pltpu.make_async_copy(
            w_hbm.at[e, :, pl.ds(pl.multiple_of((part * nf + fi) * tf, tf), tf)],
            wbuf.at[slot, part], sem.at[slot, part]) for part in range(2)]

    def cast(slot):
        wgb[...] = wbuf[slot, 0].astype(BF16)
        wub[...] = wbuf[slot, 1].astype(BF16)

    _weight_pipeline(f, m, nf, chg_ref, nxt_ref, wrap_ref, cnt_ref, copies, cast)

    def act(x):
        hg = jnp.dot(x, wgb[...], preferred_element_type=F32) + bg_ref[0]
        hu = jnp.dot(x, wub[...], preferred_element_type=F32) + bu_ref[0]
        hg = jnp.minimum(hg, SWIGLU_LIMIT)
        hu = jnp.clip(hu, -SWIGLU_LIMIT, SWIGLU_LIMIT)
        return (hg * jax.nn.sigmoid(SWIGLU_ALPHA * hg) * (hu + 1.0)).astype(BF16)

    @pl.when(rows > half)
    def _():
        a_ref[...] = act(x_ref[...].astype(BF16))

    @pl.when((rows > 0) & (rows <= half))
    def _():
        a_ref[0:half, :] = act(x_ref[0:half, :].astype(BF16))
        a_ref[half:, :] = jnp.zeros((MOE_TM - half, a_ref.shape[1]), BF16)

    @pl.when(rows == 0)
    def _():
        a_ref[...] = jnp.zeros_like(a_ref)


def _down_kernel(be_ref, br_ref, chg_ref, nxt_ref, wrap_ref, a_ref, w_hbm, b_ref, y_ref,
                 wbuf, wb, sem, cnt_ref, *, nj, tn):
    j = pl.program_id(0)
    m = pl.program_id(1)
    rows = br_ref[m]
    half = MOE_TM // 2

    def copies(e, ji, slot):
        return [pltpu.make_async_copy(
            w_hbm.at[e, :, pl.ds(pl.multiple_of(ji * tn, tn), tn)], wbuf.at[slot], sem.at[slot])]

    def cast(slot):
        wb[...] = wbuf[slot].astype(BF16)

    _weight_pipeline(j, m, nj, chg_ref, nxt_ref, wrap_ref, cnt_ref, copies, cast)

    @pl.when(rows > half)
    def _():
        y_ref[...] = jnp.dot(a_ref[...], wb[...], preferred_element_type=F32) + b_ref[0]

    @pl.when((rows > 0) & (rows <= half))
    def _():
        y_ref[0:half, :] = jnp.dot(a_ref[0:half, :], wb[...], preferred_element_type=F32) + b_ref[0]
        y_ref[half:, :] = jnp.zeros((MOE_TM - half, y_ref.shape[1]), F32)

    @pl.when(rows == 0)
    def _():
        y_ref[...] = jnp.zeros_like(y_ref)


def _experts(plan, xr, w_gu, b_gu, w_down, b_down):
    block_e, block_rows, chg, nxt, wrap = plan
    rows, d = xr.shape
    d_ff = w_down.shape[1]
    n_blocks = rows // MOE_TM
    tf = _pick(d_ff, 1024, LANE)
    nf = d_ff // tf
    a = pl.pallas_call(
        functools.partial(_gate_up_kernel, nf=nf, tf=tf),
        out_shape=jax.ShapeDtypeStruct((rows, d_ff), BF16),
        grid_spec=pltpu.PrefetchScalarGridSpec(
            num_scalar_prefetch=5, grid=(nf, n_blocks),
            in_specs=[pl.BlockSpec((MOE_TM, d), lambda f, m, be, *_: (m, 0)),
                      pl.BlockSpec(memory_space=pl.ANY),
                      pl.BlockSpec((1, 1, tf), lambda f, m, be, *_: (be[m], 0, f)),
                      pl.BlockSpec((1, 1, tf), lambda f, m, be, *_: (be[m], 0, nf + f))],
            out_specs=pl.BlockSpec((MOE_TM, tf), lambda f, m, be, *_: (m, f)),
            scratch_shapes=[pltpu.VMEM((2, 2, d, tf), F32), pltpu.VMEM((d, tf), BF16),
                            pltpu.VMEM((d, tf), BF16), pltpu.SemaphoreType.DMA((2, 2)),
                            pltpu.SMEM((1,), I32)]),
        compiler_params=_cparams(("arbitrary", "arbitrary")),
    )(block_e, block_rows, chg, nxt, wrap, xr, w_gu, b_gu, b_gu)
    tn = _pick(d, 2048, LANE)
    nj = d // tn
    return pl.pallas_call(
        functools.partial(_down_kernel, nj=nj, tn=tn),
        out_shape=jax.ShapeDtypeStruct((rows, d), F32),
        grid_spec=pltpu.PrefetchScalarGridSpec(
            num_scalar_prefetch=5, grid=(nj, n_blocks),
            in_specs=[pl.BlockSpec((MOE_TM, d_ff), lambda j, m, be, *_: (m, 0)),
                      pl.BlockSpec(memory_space=pl.ANY),
                      pl.BlockSpec((1, 1, tn), lambda j, m, be, *_: (be[m], 0, j))],
            out_specs=pl.BlockSpec((MOE_TM, tn), lambda j, m, be, *_: (m, j)),
            scratch_shapes=[pltpu.VMEM((2, d_ff, tn), F32), pltpu.VMEM((d_ff, tn), BF16),
                            pltpu.SemaphoreType.DMA((2,)), pltpu.SMEM((1,), I32)]),
        compiler_params=_cparams(("arbitrary", "arbitrary")),
    )(block_e, block_rows, chg, nxt, wrap, a, w_down, b_down)


def _combine_kernel(x1_ref, y_ref, gt_ref, g_ref, b_ref, o_ref, *, dn_alpha):
    gt = gt_ref[...]
    moe = y_ref[0] * gt[:, 0:1]
    for k in range(1, TOP_K):
        moe = moe + y_ref[k] * gt[:, k:k + 1]
    o_ref[...] = _layer_norm(dn_alpha * x1_ref[...] + moe, g_ref[...], b_ref[...])


def _combine_ln(x1, yk, gates, ln_g, ln_b, dn_alpha, row0, n):
    d = x1.shape[1]
    tm = _pick(n, 512)
    r0 = row0 // tm
    return pl.pallas_call(
        functools.partial(_combine_kernel, dn_alpha=dn_alpha),
        out_shape=jax.ShapeDtypeStruct((n, d), F32),
        grid=(n // tm,),
        in_specs=[pl.BlockSpec((tm, d), lambda i: (r0 + i, 0)),
                  pl.BlockSpec((TOP_K, tm, d), lambda i: (0, r0 + i, 0)),
                  pl.BlockSpec((tm, LANE), lambda i: (r0 + i, 0)),
                  pl.BlockSpec((1, d), lambda i: (0, 0)),
                  pl.BlockSpec((1, d), lambda i: (0, 0))],
        out_specs=pl.BlockSpec((tm, d), lambda i: (i, 0)),
        compiler_params=_cparams(("parallel",)),
    )(x1, yk, gates, ln_g, ln_b)


def _route(top_e, n_experts):
    n = top_e.shape[0]
    n4 = n * TOP_K
    e_flat = top_e.reshape(-1)
    entry = jnp.arange(n4, dtype=I32)
    _, order = lax.sort((e_flat, entry), num_keys=1, is_stable=True)
    experts = jnp.arange(n_experts, dtype=I32)
    counts = jnp.sum((e_flat[:, None] == experts[None, :]).astype(I32), axis=0)
    ends = jnp.cumsum(counts)
    starts = ends - counts
    padded = (counts + MOE_TM - 1) // MOE_TM * MOE_TM
    pends = jnp.cumsum(padded)
    pstarts = pends - padded
    dest_s = entry + jnp.sum(jnp.where(entry[:, None] >= ends[None, :],
                                       (padded - counts)[None, :], 0), axis=1)
    _, pos = lax.sort((order, dest_s), num_keys=1)
    n_blocks = -(-n4 // MOE_TM) + n_experts
    block0 = jnp.arange(n_blocks, dtype=I32) * MOE_TM
    block_e = jnp.minimum(jnp.sum((pends[None, :] <= block0[:, None]).astype(I32), axis=1),
                          n_experts - 1)
    block_rows = jnp.clip(counts[block_e] - (block0 - pstarts[block_e]), 0, MOE_TM)
    in_block = jnp.arange(MOE_TM, dtype=I32)[None, :]
    src = (block0 + starts[block_e] - pstarts[block_e])[:, None] + in_block
    row_tok = jnp.where(in_block < block_rows[:, None],
                        order[jnp.clip(src, 0, n4 - 1)] // TOP_K, 0).reshape(-1)
    blk = jnp.arange(n_blocks, dtype=I32)
    prev_e = jnp.concatenate([block_e[:1] - 1, block_e[:-1]])
    chg = (block_rows > 0) & (block_e != prev_e)
    later = jnp.where(chg, blk, n_blocks)
    nxt_idx = lax.cummin(jnp.concatenate([later[1:], later[:1] * 0 + n_blocks]), reverse=True)
    wrap = nxt_idx >= n_blocks
    nxt_e = block_e[jnp.where(wrap, 0, nxt_idx)]
    nxt = jnp.concatenate([block_e[:1], nxt_e])
    plan = (block_e, block_rows.astype(I32), chg.astype(I32), nxt.astype(I32), wrap.astype(I32))
    return row_tok, pos.reshape(n, TOP_K), plan


def _layer(x_prompt, x_sample, cache, state_win, state_conv, page_table,
           w_in, w_out, conv_w, cmp_k_pe, cmp_k_w1, cmp_k_w2, cmp_v_pe, cmp_v_w1, cmp_v_w2,
           ln1_g, ln1_b, w_router, b_router, w_gu, b_gu, w_down, b_down, ln2_g, ln2_b, depth):
    nb, seq_len, d = x_prompt.shape
    db, dec_seq, _ = x_sample.shape
    n_planes = N_KV_COMP * N_KV_HEADS
    n_wplanes = 2 * N_KV_HEADS
    page = cache.shape[1] // n_planes
    n_pages = page_table.shape[1]
    past_len = n_pages * page
    win_buf = state_win.shape[1] // n_wplanes
    conv_dim = conv_w.shape[1]
    n_experts = w_router.shape[1]
    n_p, n_s = nb * seq_len, db * dec_seq
    dn_alpha = (2.0 * depth) ** 0.25
    assert page == LANE and dec_seq == SUBLANE and seq_len % Q_BLOCK == 0
    assert seq_len >= WINDOW + Q_BLOCK and win_buf == WINDOW and seq_len % win_buf == 0
    assert n_p % n_s == 0 and conv_dim % LANE == 0 and n_experts <= LANE
    assert past_len + dec_seq <= 2 * 32 * SEL_BLOCK and seq_len <= 32 * SEL_BLOCK

    off_gate = ATTN_WIDTH + N_KV_COMP * KV_WIDTH + 2 * KV_WIDTH
    off_conv = off_gate + N_GATE
    w_p = jnp.concatenate([w_in[:, :off_gate], w_in[:, off_conv:], w_in[:, off_gate:off_conv],
                           jnp.zeros((d, LANE - N_GATE), w_in.dtype)], axis=1).astype(BF16)
    n_cb = w_p.shape[1] // LANE
    xp, xs = x_prompt.reshape(n_p, d), x_sample.reshape(n_s, d)
    n_tok = n_p + n_s
    proj = _proj(xp, xs, w_p, _pick(n_s, 512), _pick(w_p.shape[1], 1920, LANE))

    kv_cb = ATTN_WIDTH // (n_planes * LANE)
    tb = _pick(n_p, 256)
    kv_prompt = _interleave(proj, n_p // tb, tb, n_planes, lambda i: i, kv_cb)
    win_cb = (ATTN_WIDTH + n_planes * LANE) // (n_wplanes * LANE)
    per_seq = seq_len // win_buf
    win_prompt = _interleave(proj, nb, win_buf, n_wplanes,
                             lambda i: i * per_seq + per_seq - 1, win_cb)

    w1k = jnp.concatenate(jnp.split(cmp_k_w1, CMP_RATIO, axis=0), axis=1).astype(BF16)
    w1v = jnp.concatenate(jnp.split(cmp_v_w1, CMP_RATIO, axis=0), axis=1).astype(BF16)
    bk = _pe_bias(cmp_k_pe, cmp_k_w1.astype(BF16))
    bv = _pe_bias(cmp_v_pe, cmp_v_w1.astype(BF16))
    cmp_w = (w1k, w1v, bk, bv, cmp_k_w2.astype(BF16), cmp_v_w2.astype(BF16))
    pp = seq_len // LANE
    prompt_ids = jnp.arange(nb * pp, dtype=I32).reshape(nb, pp)
    proj_pages = proj.reshape(n_tok // LANE, LANE, n_cb * LANE)
    kc_p, vc_p = _compress(prompt_ids, proj_pages, ATTN_WIDTH // (2 * KV_WIDTH), *cmp_w)

    o_p = _prompt_attention(proj, kc_p, vc_p, nb, seq_len, n_cb)
    o_s, kv_sample, win_sample = _sample_attention(page_table, cache, proj, state_win, cmp_w,
                                                   n_p, dec_seq, n_cb)

    c_p, u_p = _short_conv(proj, conv_w, 0, nb, seq_len, seq_len)
    z = jnp.zeros((db, dec_seq - 2, conv_dim), F32)
    s1 = jnp.concatenate([state_conv[:, 1:2], z, z[:, :1]], axis=1).reshape(n_s, conv_dim)
    s2 = jnp.concatenate([state_conv, z], axis=1).reshape(n_s, conv_dim)
    c_s, u_s = _short_conv(proj, conv_w, n_p // n_s, 1, n_s, dec_seq, state=(s1, s2))
    conv_prompt = u_p.reshape(nb, SUBLANE, conv_dim)[:, SUBLANE - (CONV_WIDTH - 1):]
    conv_sample = u_s.reshape(db, dec_seq, conv_dim)[:, dec_seq - (CONV_WIDTH - 1):]

    w_r = jnp.pad(w_router, ((0, 0), (0, LANE - n_experts)))
    b_r = jnp.pad(b_router, (0, LANE - n_experts)).reshape(1, LANE)
    x1, e_idx, gates = _mix_ln_router(
        (o_p, o_s), (c_p, c_s), (xp, xs), w_out[:ATTN_WIDTH].astype(BF16),
        w_out[ATTN_WIDTH:].astype(BF16), ln1_g.reshape(1, d), ln1_b.reshape(1, d), w_r, b_r,
        dn_alpha, n_experts)

    row_tok, pos, plan = _route(e_idx[:, :TOP_K], n_experts)
    y_rows = _experts(plan, x1[row_tok], w_gu, b_gu.reshape(n_experts, 1, -1),
                      w_down, b_down.reshape(n_experts, 1, -1))

    comb = (x1, y_rows[pos.T], gates, ln2_g.reshape(1, d), ln2_b.reshape(1, d), dn_alpha)
    y_prompt = _combine_ln(*comb, 0, n_p).reshape(nb, seq_len, d)
    y_sample = _combine_ln(*comb, n_p, n_s).reshape(db, dec_seq, d)
    kv_shape = (N_KV_COMP, N_KV_HEADS, HEAD_DIM)
    win_shape = (win_buf, 2, N_KV_HEADS, HEAD_DIM)
    return (y_prompt, y_sample,
            kv_prompt.reshape((nb, seq_len) + kv_shape), kv_sample.reshape((db, dec_seq) + kv_shape),
            win_prompt.reshape((nb,) + win_shape), win_sample.reshape((db,) + win_shape),
            conv_prompt, conv_sample)


def kernel(x_prompt, x_sample, cache_kv, state_win, state_conv, page_table, w_in, w_out, conv_w, cmp_k_pe, cmp_k_w1, cmp_k_w2, cmp_v_pe, cmp_v_w1, cmp_v_w2, ln1_g, ln1_b, w_router, b_router, w_gu, b_gu, w_down, b_down, ln2_g, ln2_b):
    depth, n_pool, page = cache_kv.shape[:3]
    db, win_buf = state_win.shape[1:3]
    cache_rows = cache_kv.reshape(depth * n_pool, page * N_KV_COMP * N_KV_HEADS, HEAD_DIM)
    win_rows = state_win.reshape(depth, db, win_buf * 2 * N_KV_HEADS, HEAD_DIM)
    y_prompt, y_sample = x_prompt, x_sample
    outs = [[] for _ in range(6)]
    for layer in range(depth):
        res = _layer(y_prompt, y_sample, cache_rows, win_rows[layer], state_conv[layer],
                     page_table + layer * n_pool,
                     w_in[layer], w_out[layer], conv_w[layer],
                     cmp_k_pe[layer], cmp_k_w1[layer], cmp_k_w2[layer],
                     cmp_v_pe[layer], cmp_v_w1[layer], cmp_v_w2[layer],
                     ln1_g[layer], ln1_b[layer], w_router[layer], b_router[layer],
                     w_gu[layer], b_gu[layer], w_down[layer], b_down[layer],
                     ln2_g[layer], ln2_b[layer], depth)
        y_prompt, y_sample = res[0], res[1]
        for lst, val in zip(outs, res[2:]):
            lst.append(val)
    return (y_prompt, y_sample) + tuple(jnp.stack(o) for o in outs)
```
